```python
import math
import jax, jax.numpy as jnp
from jax import lax
import numpy as np

D_MODEL = 4096
BATCH = 4
SEQ = 2048
DEPTH = 2

N_MEM = 256
ROPE_THETA = 10000.0
NORM_EPS = 1e-6
BLOCK_Q = 128

SELF_WIDTH = 3 * D_MODEL // 4
CROSS_HEADS = 4
CROSS_HEAD_DIM = (D_MODEL // 4) // CROSS_HEADS
CROSS_WIDTH = CROSS_HEADS * CROSS_HEAD_DIM

DIFF_HEAD_DIM = 128
DIFF_V_DIM = 2 * DIFF_HEAD_DIM
DIFF_HEADS = SELF_WIDTH // DIFF_V_DIM

MLA_NOPE_DIM = 128
MLA_ROPE_DIM = 64
MLA_V_DIM = 128
MLA_HEADS = SELF_WIDTH // MLA_V_DIM
MLA_QK_DIM = MLA_NOPE_DIM + MLA_ROPE_DIM
MLA_Q_RANK = 3 * D_MODEL // 16
MLA_KV_RANK = D_MODEL // 8

N_GROUPS = 8
EXPERTS_PER_GROUP = 8
N_EXPERTS = N_GROUPS * EXPERTS_PER_GROUP
D_EXPERT = 3 * D_MODEL // 32
TOP_K_IN_GROUP = 2

DIFF_IN_COLS = 3 * SELF_WIDTH + CROSS_WIDTH
MLA_IN_COLS = MLA_Q_RANK + MLA_KV_RANK + MLA_ROPE_DIM + CROSS_WIDTH
OUT_ROWS = SELF_WIDTH + CROSS_WIDTH

kernel_name = "hybrid_diffattn_mla_memxattn_hmoe"

F32 = jnp.float32


def rms_norm(x, g):
    xf = x.astype(F32)
    y = xf * lax.rsqrt(jnp.mean(xf * xf, axis=-1, keepdims=True) + NORM_EPS)
    return (y * g.astype(F32)).astype(x.dtype)


def rope_tables(positions, dim):
    inv_freq = ROPE_THETA ** (-jnp.arange(0, dim, 2, dtype=F32) / dim)
    ang = positions.astype(F32)[..., None] * inv_freq
    return jnp.cos(ang), jnp.sin(ang)


def apply_rope(x, cos, sin):
    xf = x.astype(F32)
    half = xf.shape[-1] // 2
    x1, x2 = xf[..., :half], xf[..., half:]
    c, s = cos[:, :, None, :], sin[:, :, None, :]
    return jnp.concatenate([x1 * c - x2 * s, x2 * c + x1 * s], axis=-1).astype(x.dtype)


def causal_block_attention(qs, ks, v, positions, scale, combine):
    B, S, H, dv = v.shape
    nb = S // BLOCK_Q

    def to_blocks(t):
        return jnp.moveaxis(t.reshape((B, nb, BLOCK_Q) + t.shape[2:]), 1, 0)

    q_blocks = tuple(to_blocks(q) for q in qs)
    pos_blocks = to_blocks(positions)

    def one_block(args):
        qbs, pq = args
        mask = positions[:, None, None, :] <= pq[:, None, :, None]
        probs = []
        for qb, k in zip(qbs, ks):
            s = jnp.einsum('bqhd,bkhd->bhqk', qb, k, preferred_element_type=F32) * scale
            probs.append(jax.nn.softmax(jnp.where(mask, s, -jnp.inf), axis=-1))
        w = combine(probs)
        return jnp.einsum('bhqk,bkhd->bqhd', w.astype(v.dtype), v)

    out = lax.map(one_block, (q_blocks, pos_blocks))
    return jnp.moveaxis(out, 0, 1).reshape(B, S, H, dv)


def memory_cross_attention(q_mem, mem_k, mem_v, q_norm_g):
    B, S, _ = q_mem.shape
    q = rms_norm(q_mem.reshape(B, S, CROSS_HEADS, CROSS_HEAD_DIM), q_norm_g)
    s = jnp.einsum('bqhd,bmhd->bhqm', q, mem_k, preferred_element_type=F32) * (CROSS_HEAD_DIM ** -0.5)
    p = jax.nn.softmax(s, axis=-1)
    o = jnp.einsum('bhqm,bmhd->bqhd', p.astype(mem_v.dtype), mem_v)
    return o.reshape(B, S, CROSS_WIDTH)


def diff_attention_mixer(h, cos, sin, positions, w_in, q_norm_g, k_norm_g,
                         lambda_q1, lambda_k1, lambda_q2, lambda_k2, subln_g, layer_idx):
    B, S, _ = h.shape
    proj = h @ w_in
    q, k, v, q_mem = jnp.split(proj, [SELF_WIDTH, 2 * SELF_WIDTH, 3 * SELF_WIDTH], axis=-1)
    q = apply_rope(rms_norm(q.reshape(B, S, 2 * DIFF_HEADS, DIFF_HEAD_DIM), q_norm_g), cos, sin)
    k = apply_rope(rms_norm(k.reshape(B, S, 2 * DIFF_HEADS, DIFF_HEAD_DIM), k_norm_g), cos, sin)
    v = v.reshape(B, S, DIFF_HEADS, DIFF_V_DIM)
    lam_init = 0.8 - 0.6 * math.exp(-0.3 * layer_idx)
    lam = (jnp.exp(jnp.sum(lambda_q1.astype(F32) * lambda_k1.astype(F32)))
           - jnp.exp(jnp.sum(lambda_q2.astype(F32) * lambda_k2.astype(F32))) + lam_init)
    o = causal_block_attention((q[:, :, 0::2], q[:, :, 1::2]), (k[:, :, 0::2], k[:, :, 1::2]), v,
                               positions, DIFF_HEAD_DIM ** -0.5,
                               lambda ps: ps[0] - lam * ps[1])
    o = rms_norm(o, subln_g) * (1.0 - lam_init)
    return o.reshape(B, S, SELF_WIDTH), q_mem


def mla_mixer(h, cos, sin, positions, w_in, cq_norm_g, ckv_norm_g, w_uq, w_ukv, q_norm_g, k_norm_g):
    B, S, _ = h.shape
    proj = h @ w_in
    c_q, c_kv, k_pe, q_mem = jnp.split(
        proj, [MLA_Q_RANK, MLA_Q_RANK + MLA_KV_RANK, MLA_Q_RANK + MLA_KV_RANK + MLA_ROPE_DIM], axis=-1)
    q = (rms_norm(c_q, cq_norm_g) @ w_uq).reshape(B, S, MLA_HEADS, MLA_QK_DIM)
    kv = (rms_norm(c_kv, ckv_norm_g) @ w_ukv).reshape(B, S, MLA_HEADS, MLA_NOPE_DIM + MLA_V_DIM)
    k_nope, v = kv[..., :MLA_NOPE_DIM], kv[..., MLA_NOPE_DIM:]
    k_pe = jnp.broadcast_to(k_pe[:, :, None, :], (B, S, MLA_HEADS, MLA_ROPE_DIM))
    k = jnp.concatenate([k_nope, k_pe], axis=-1)
    q = rms_norm(q, q_norm_g)
    k = rms_norm(k, k_norm_g)
    q = jnp.concatenate([q[..., :MLA_NOPE_DIM], apply_rope(q[..., MLA_NOPE_DIM:], cos, sin)], axis=-1)
    k = jnp.concatenate([k[..., :MLA_NOPE_DIM], apply_rope(k[..., MLA_NOPE_DIM:], cos, sin)], axis=-1)
    o = causal_block_attention((q,), (k,), v, positions, MLA_QK_DIM ** -0.5, lambda ps: ps[0])
    return o.reshape(B, S, SELF_WIDTH), q_mem


def hierarchical_moe(h, w_group, b_group, w_router, b_router, w_gate, w_up, w_down):
    B, S, D = h.shape
    t = h.reshape(-1, D)
    T = t.shape[0]
    g_logits = (t @ w_group).astype(F32) + b_group.astype(F32)
    g_prob = jax.nn.softmax(g_logits, axis=-1)
    g_sel = jnp.argmax(g_logits, axis=-1)
    g_gate = jnp.take_along_axis(g_prob, g_sel[:, None], axis=-1)
    e_logits = ((t @ w_router).astype(F32) + b_router.astype(F32)).reshape(T, N_GROUPS, EXPERTS_PER_GROUP)
    e_logits = jnp.take_along_axis(e_logits, g_sel[:, None, None], axis=1)[:, 0]
    top_v, top_i = lax.top_k(e_logits, TOP_K_IN_GROUP)
    top_w = jax.nn.softmax(top_v, axis=-1) * g_gate
    in_group = jnp.sum(jax.nn.one_hot(top_i, EXPERTS_PER_GROUP, dtype=F32) * top_w[..., None], axis=1)
    combine = (jax.nn.one_hot(g_sel, N_GROUPS, dtype=F32)[:, :, None] * in_group[:, None, :]).astype(t.dtype)
    out = jnp.zeros_like(t)
    for g in range(N_GROUPS):
        a = jnp.einsum('td,edf->tef', t, w_gate[g])
        u = jnp.einsum('td,edf->tef', t, w_up[g])
        hid = jax.nn.silu(a) * u * combine[:, g, :, None]
        out = out + jnp.einsum('tef,efd->td', hid, w_down[g])
    return out.reshape(B, S, D)


def setup_inputs(seed: int = 0) -> dict:
    key = jax.random.key(seed)
    ks = iter(jax.random.split(key, 64))

    def nrm(shape, scale):
        return jax.random.normal(next(ks), shape, F32) * scale

    def gain(n):
        return 1.0 + 0.02 * jax.random.normal(next(ks), (n,), F32)

    inp = {}
    inp["x"] = nrm((BATCH, SEQ, D_MODEL), 1.0)
    inp["mem"] = nrm((BATCH, N_MEM, D_MODEL), 1.0)
    offset = jax.random.randint(next(ks), (BATCH, 1), 0, 1024, dtype=jnp.int32)
    inp["positions"] = (jnp.arange(SEQ, dtype=jnp.int32)[None, :] + offset).astype(jnp.int32)
    inp["mem_norm_g"] = gain(D_MODEL)
    inp["w_mem_kv"] = nrm((D_MODEL, 2 * CROSS_WIDTH), D_MODEL ** -0.5)
    inp["mem_k_norm_g"] = gain(CROSS_HEAD_DIM)

    def moe_params(prefix):
        inp[prefix + "ffn_norm_g"] = gain(D_MODEL)
        inp[prefix + "w_group"] = nrm((D_MODEL, N_GROUPS), D_MODEL ** -0.5)
        inp[prefix + "b_group"] = nrm((N_GROUPS,), 0.01)
        inp[prefix + "w_router"] = nrm((D_MODEL, N_EXPERTS), D_MODEL ** -0.5)
        inp[prefix + "b_router"] = nrm((N_EXPERTS,), 0.01)
        inp[prefix + "w_gate"] = nrm((N_GROUPS, EXPERTS_PER_GROUP, D_MODEL, D_EXPERT), D_MODEL ** -0.5)
        inp[prefix + "w_up"] = nrm((N_GROUPS, EXPERTS_PER_GROUP, D_MODEL, D_EXPERT), D_MODEL ** -0.5)
        inp[prefix + "w_down"] = nrm((N_GROUPS, EXPERTS_PER_GROUP, D_EXPERT, D_MODEL), D_EXPERT ** -0.5)

    inp["l0_attn_norm_g"] = gain(D_MODEL)
    inp["l0_w_in"] = nrm((D_MODEL, DIFF_IN_COLS), D_MODEL ** -0.5)
    inp["l0_q_norm_g"] = gain(DIFF_HEAD_DIM)
    inp["l0_k_norm_g"] = gain(DIFF_HEAD_DIM)
    inp["l0_lambda_q1"] = nrm((DIFF_HEAD_DIM,), 0.1)
    inp["l0_lambda_k1"] = nrm((DIFF_HEAD_DIM,), 0.1)
    inp["l0_lambda_q2"] = nrm((DIFF_HEAD_DIM,), 0.1)
    inp["l0_lambda_k2"] = nrm((DIFF_HEAD_DIM,), 0.1)
    inp["l0_subln_g"] = gain(DIFF_V_DIM)
    inp["l0_cross_q_norm_g"] = gain(CROSS_HEAD_DIM)
    inp["l0_w_out"] = nrm((OUT_ROWS, D_MODEL), OUT_ROWS ** -0.5)
    moe_params("l0_")
    inp["l1_attn_norm_g"] = gain(D_MODEL)
    inp["l1_w_in"] = nrm((D_MODEL, MLA_IN_COLS), D_MODEL ** -0.5)
    inp["l1_cq_norm_g"] = gain(MLA_Q_RANK)
    inp["l1_ckv_norm_g"] = gain(MLA_KV_RANK)
    inp["l1_w_uq"] = nrm((MLA_Q_RANK, MLA_HEADS * MLA_QK_DIM), MLA_Q_RANK ** -0.5)
    inp["l1_w_ukv"] = nrm((MLA_KV_RANK, MLA_HEADS * (MLA_NOPE_DIM + MLA_V_DIM)), MLA_KV_RANK ** -0.5)
    inp["l1_q_norm_g"] = gain(MLA_QK_DIM)
    inp["l1_k_norm_g"] = gain(MLA_QK_DIM)
    inp["l1_cross_q_norm_g"] = gain(CROSS_HEAD_DIM)
    inp["l1_w_out"] = nrm((OUT_ROWS, D_MODEL), OUT_ROWS ** -0.5)
    moe_params("l1_")
    return inp


def reference(x, mem, positions, mem_norm_g, w_mem_kv, mem_k_norm_g,
              l0_attn_norm_g, l0_w_in, l0_q_norm_g, l0_k_norm_g, l0_lambda_q1, l0_lambda_k1,
              l0_lambda_q2, l0_lambda_k2, l0_subln_g, l0_cross_q_norm_g, l0_w_out,
              l0_ffn_norm_g, l0_w_group, l0_b_group, l0_w_router, l0_b_router, l0_w_gate, l0_w_up, l0_w_down,
              l1_attn_norm_g, l1_w_in, l1_cq_norm_g, l1_ckv_norm_g, l1_w_uq, l1_w_ukv, l1_q_norm_g,
              l1_k_norm_g, l1_cross_q_norm_g, l1_w_out,
              l1_ffn_norm_g, l1_w_group, l1_b_group, l1_w_router, l1_b_router, l1_w_gate, l1_w_up, l1_w_down):
    B, S, D = x.shape
    mem_kv = (rms_norm(mem, mem_norm_g) @ w_mem_kv).reshape(B, N_MEM, 2, CROSS_HEADS, CROSS_HEAD_DIM)
    mem_k = rms_norm(mem_kv[:, :, 0], mem_k_norm_g)
    mem_v = mem_kv[:, :, 1]
    cos_full, sin_full = rope_tables(positions, DIFF_HEAD_DIM)
    cos_pe, sin_pe = rope_tables(positions, MLA_ROPE_DIM)

    layers = [
        dict(attn_norm_g=l0_attn_norm_g, cross_q_norm_g=l0_cross_q_norm_g, w_out=l0_w_out,
             ffn_norm_g=l0_ffn_norm_g, moe=(l0_w_group, l0_b_group, l0_w_router, l0_b_router,
                                            l0_w_gate, l0_w_up, l0_w_down)),
        dict(attn_norm_g=l1_attn_norm_g, cross_q_norm_g=l1_cross_q_norm_g, w_out=l1_w_out,
             ffn_norm_g=l1_ffn_norm_g, moe=(l1_w_group, l1_b_group, l1_w_router, l1_b_router,
                                            l1_w_gate, l1_w_up, l1_w_down)),
    ]
    for i in range(DEPTH):
        p = layers[i]
        h = rms_norm(x, p["attn_norm_g"])
        if i % 2 == 0:
            self_out, q_mem = diff_attention_mixer(h, cos_full, sin_full, positions, l0_w_in,
                                                   l0_q_norm_g, l0_k_norm_g, l0_lambda_q1, l0_lambda_k1,
                                                   l0_lambda_q2, l0_lambda_k2, l0_subln_g, i)
        else:
            self_out, q_mem = mla_mixer(h, cos_pe, sin_pe, positions, l1_w_in, l1_cq_norm_g,
                                        l1_ckv_norm_g, l1_w_uq, l1_w_ukv, l1_q_norm_g, l1_k_norm_g)
        cross_out = memory_cross_attention(q_mem, mem_k, mem_v, p["cross_q_norm_g"])
        x = x + jnp.concatenate([self_out.astype(x.dtype), cross_out.astype(x.dtype)], axis=-1) @ p["w_out"]
        x = x + hierarchical_moe(rms_norm(x, p["ffn_norm_g"]), *p["moe"])
    return x
```

```python
import functools
import math

import jax
import jax.numpy as jnp
from jax import lax
from jax.experimental import pallas as pl
from jax.experimental.pallas import tpu as pltpu

F32 = jnp.float32
BF16 = jnp.bfloat16
I32 = jnp.int32

D_MODEL = 4096
N_MEM = 256
ROPE_THETA = 10000.0
NORM_EPS = 1e-6

SELF_WIDTH = 3 * D_MODEL // 4
CROSS_HEADS = 4
CROSS_HEAD_DIM = (D_MODEL // 4) // CROSS_HEADS
CROSS_WIDTH = CROSS_HEADS * CROSS_HEAD_DIM

DIFF_HEAD_DIM = 128
DIFF_V_DIM = 2 * DIFF_HEAD_DIM
DIFF_HEADS = SELF_WIDTH // DIFF_V_DIM

MLA_NOPE_DIM = 128
MLA_ROPE_DIM = 64
MLA_V_DIM = 128
MLA_HEADS = SELF_WIDTH // MLA_V_DIM
MLA_QK_DIM = MLA_NOPE_DIM + MLA_ROPE_DIM
MLA_Q_RANK = 3 * D_MODEL // 16
MLA_KV_RANK = D_MODEL // 8
MLA_PAD_DIM = 256

N_GROUPS = 8
EXPERTS_PER_GROUP = 8
N_EXPERTS = N_GROUPS * EXPERTS_PER_GROUP
D_EXPERT = 3 * D_MODEL // 32
TOP_K = 2

LANES = 128
MOE_TM = 384
MOE_TF = 128
ATTN_TQ = 256
VMEM_LIMIT = 56 * 1024 * 1024


def _cparams(n_axes, vmem=None):
    return pltpu.CompilerParams(dimension_semantics=("arbitrary",) * n_axes,
                                vmem_limit_bytes=vmem)


def _rms(x):
    return x * lax.rsqrt(jnp.mean(x * x, axis=-1, keepdims=True) + NORM_EPS)


def _dot_nt(a, b):
    return lax.dot_general(a, b, (((1,), (1,)), ((), ())), preferred_element_type=F32)


def _norm_kernel(x_ref, g_ref, o_ref):
    o_ref[...] = (_rms(x_ref[...]) * g_ref[...]).astype(o_ref.dtype)


def _norm_call(x, g, tm=256):
    m, d = x.shape
    return pl.pallas_call(
        _norm_kernel,
        grid=(m // tm,),
        in_specs=[pl.BlockSpec((tm, d), lambda i: (i, 0)),
                  pl.BlockSpec((1, d), lambda i: (0, 0))],
        out_specs=pl.BlockSpec((tm, d), lambda i: (i, 0)),
        out_shape=jax.ShapeDtypeStruct((m, d), BF16),
        compiler_params=_cparams(1),
        name="rmsnorm",
    )(x, g.reshape(1, d))


def _epilogue(kind, acc, gain_ref, cos_ref, sin_ref, o_ref):
    tn = acc.shape[1]
    if kind == "plain":
        o_ref[...] = acc.astype(o_ref.dtype)
    elif kind == "head128_rope":
        c = cos_ref[...]
        s = sin_ref[...]
        for j in range(tn // 128):
            sl = slice(j * 128, (j + 1) * 128)
            y = _rms(acc[:, sl]) * gain_ref[:, sl]
            y = y * c + pltpu.roll(y, 64, axis=1) * s
            o_ref[:, sl] = y.astype(o_ref.dtype)
    elif kind == "head256":
        for j in range(tn // 256):
            sl = slice(j * 256, (j + 1) * 256)
            o_ref[:, sl] = (_rms(acc[:, sl]) * gain_ref[:, sl]).astype(o_ref.dtype)
    else:
        raise ValueError(kind)


def _mm_kernel(x_ref, w_ref, gain_ref, cos_ref, sin_ref, o_ref, wbf_ref, *, kinds):
    n = pl.program_id(0)

    @pl.when(pl.program_id(1) == 0)
    def _():
        wbf_ref[...] = w_ref[...].astype(BF16)

    acc = jnp.dot(x_ref[...], wbf_ref[...], preferred_element_type=F32)
    if len(kinds) == 1:
        _epilogue(kinds[0][2], acc, gain_ref, cos_ref, sin_ref, o_ref)
    else:
        for lo, hi, kind in kinds:
            pl.when((n >= lo) & (n < hi))(
                functools.partial(_epilogue, kind, acc, gain_ref, cos_ref, sin_ref, o_ref))


def _mm_call(x, w, gain, cos, sin, kinds, *, tm, tn, name):
    m, k = x.shape
    n = w.shape[1]
    return pl.pallas_call(
        functools.partial(_mm_kernel, kinds=kinds),
        grid=(n // tn, m // tm),
        in_specs=[pl.BlockSpec((tm, k), lambda j, i: (i, 0)),
                  pl.BlockSpec((k, tn), lambda j, i: (0, j)),
                  pl.BlockSpec((1, tn), lambda j, i: (0, j)),
                  pl.BlockSpec((tm, LANES), lambda j, i: (i, 0)),
                  pl.BlockSpec((tm, LANES), lambda j, i: (i, 0))],
        out_specs=pl.BlockSpec((tm, tn), lambda j, i: (i, j)),
        out_shape=jax.ShapeDtypeStruct((m, n), BF16),
        scratch_shapes=[pltpu.VMEM((k, tn), BF16)],
        compiler_params=_cparams(2, VMEM_LIMIT),
        name=name,
    )(x, w, gain, cos, sin)


def _attn_kernel(*refs, n_maps, dk, dv, tq, lam_init):
    if n_maps == 2:
        (q_ref, k_ref, v_ref, posq_ref, posk_ref, lq1_ref, lk1_ref, lq2_ref, lk2_ref, subg_ref,
         o_ref, m_sc, l_sc, acc_sc) = refs
    else:
        q_ref, k_ref, v_ref, posq_ref, posk_ref, o_ref, m_sc, l_sc, acc_sc = refs
    qi = pl.program_id(2)

    m_sc[...] = jnp.full(m_sc.shape, -jnp.inf, F32)
    l_sc[...] = jnp.zeros(l_sc.shape, F32)
    acc_sc[...] = jnp.zeros(acc_sc.shape, F32)

    def block(j, mask):
        rows = pl.ds(pl.multiple_of(j * tq, tq), tq)
        kk = k_ref[rows, :]
        vv = v_ref[rows, :]
        for mi in range(n_maps):
            s = _dot_nt(q_ref[:, mi * dk:(mi + 1) * dk], kk[:, mi * dk:(mi + 1) * dk])
            if mask is not None:
                s = jnp.where(mask, s, -jnp.inf)
            m_old = m_sc[mi]
            m_new = jnp.maximum(m_old, jnp.max(s, axis=-1, keepdims=True))
            alpha = jnp.exp(m_old - m_new)
            p = jnp.exp(s - m_new)
            l_sc[mi] = alpha * l_sc[mi] + jnp.sum(p, axis=-1, keepdims=True)
            acc_sc[mi] = alpha * acc_sc[mi] + jnp.dot(p.astype(BF16), vv,
                                                      preferred_element_type=F32)
            m_sc[mi] = m_new

    def past(j, carry):
        block(j, None)
        return carry

    lax.fori_loop(0, qi, past, 0)
    block(qi, posk_ref[...] <= posq_ref[...])

    if n_maps == 2:
        lam = (jnp.exp(jnp.sum(lq1_ref[...] * lk1_ref[...], axis=-1, keepdims=True))
               - jnp.exp(jnp.sum(lq2_ref[...] * lk2_ref[...], axis=-1, keepdims=True))
               + lam_init)
        o = acc_sc[0] * (1.0 / l_sc[0]) - lam * (acc_sc[1] * (1.0 / l_sc[1]))
        o = _rms(o) * subg_ref[...] * (1.0 - lam_init)
    else:
        o = acc_sc[0] * (1.0 / l_sc[0])
    o_ref[...] = o.astype(o_ref.dtype)


def _attn_call(q, k, v, posq, posk, extra, *, batch, seq, heads, n_maps, dk, dv,
               qcol, kcol, vcol, lam_init, name):
    tq = ATTN_TQ
    nq = seq // tq
    in_specs = [
        pl.BlockSpec((tq, n_maps * dk), lambda b, h, i: (b * nq + i, qcol + h)),
        pl.BlockSpec((seq, n_maps * dk), lambda b, h, i: (b, kcol + h)),
        pl.BlockSpec((seq, dv), lambda b, h, i: (b, vcol + h)),
        pl.BlockSpec((tq, 1), lambda b, h, i: (b * nq + i, 0)),
        pl.BlockSpec((1, tq), lambda b, h, i: (0, b * nq + i)),
    ] + [pl.BlockSpec(e.shape, lambda b, h, i: (0, 0)) for e in extra]
    return pl.pallas_call(
        functools.partial(_attn_kernel, n_maps=n_maps, dk=dk, dv=dv, tq=tq, lam_init=lam_init),
        grid=(batch, heads, nq),
        in_specs=in_specs,
        out_specs=pl.BlockSpec((tq, dv), lambda b, h, i: (b * nq + i, h)),
        out_shape=jax.ShapeDtypeStruct((batch * seq, heads * dv), BF16),
        scratch_shapes=[pltpu.VMEM((n_maps, tq, 1), F32),
                        pltpu.VMEM((n_maps, tq, 1), F32),
                        pltpu.VMEM((n_maps, tq, dv), F32)],
        compiler_params=_cparams(3),
        name=name,
    )(q, k, v, posq, posk, *extra)


def _cross_kernel(q_ref, k_ref, v_ref, o_ref):
    for h in range(CROSS_HEADS):
        sl = slice(h * CROSS_HEAD_DIM, (h + 1) * CROSS_HEAD_DIM)
        s = _dot_nt(q_ref[:, sl], k_ref[:, sl])
        p = jnp.exp(s - jnp.max(s, axis=-1, keepdims=True))
        inv = 1.0 / jnp.sum(p, axis=-1, keepdims=True)
        o = jnp.dot(p.astype(BF16), v_ref[:, sl], preferred_element_type=F32)
        o_ref[:, sl] = (o * inv).astype(o_ref.dtype)


def _cross_call(q, memkv, *, batch, seq, qcol, tq=512):
    nq = seq // tq
    return pl.pallas_call(
        _cross_kernel,
        grid=(batch, nq),
        in_specs=[pl.BlockSpec((tq, CROSS_WIDTH), lambda b, i: (b * nq + i, qcol)),
                  pl.BlockSpec((N_MEM, CROSS_WIDTH), lambda b, i: (b, 0)),
                  pl.BlockSpec((N_MEM, CROSS_WIDTH), lambda b, i: (b, 1))],
        out_specs=pl.BlockSpec((tq, CROSS_WIDTH), lambda b, i: (b * nq + i, 0)),
        out_shape=jax.ShapeDtypeStruct((batch * seq, CROSS_WIDTH), BF16),
        compiler_params=_cparams(2),
        name="cross_attn",
    )(q, memkv, memkv)


def _outproj_kernel(os_ref, oc_ref, w_ref, x_ref, o_ref, wbf_ref):
    @pl.when(pl.program_id(1) == 0)
    def _():
        wbf_ref[...] = w_ref[...].astype(BF16)

    acc = jnp.dot(os_ref[...], wbf_ref[:SELF_WIDTH, :], preferred_element_type=F32)
    acc = acc + jnp.dot(oc_ref[...], wbf_ref[SELF_WIDTH:, :], preferred_element_type=F32)
    o_ref[...] = x_ref[...] + acc


def _outproj_call(o_self, o_cross, w_out, x, *, tm=1024, tn=512):
    m = x.shape[0]
    return pl.pallas_call(
        _outproj_kernel,
        grid=(D_MODEL // tn, m // tm),
        in_specs=[pl.BlockSpec((tm, SELF_WIDTH), lambda j, i: (i, 0)),
                  pl.BlockSpec((tm, CROSS_WIDTH), lambda j, i: (i, 0)),
                  pl.BlockSpec((D_MODEL, tn), lambda j, i: (0, j)),
                  pl.BlockSpec((tm, tn), lambda j, i: (i, j))],
        out_specs=pl.BlockSpec((tm, tn), lambda j, i: (i, j)),
        out_shape=jax.ShapeDtypeStruct((m, D_MODEL), F32),
        scratch_shapes=[pltpu.VMEM((D_MODEL, tn), BF16)],
        compiler_params=_cparams(2, VMEM_LIMIT),
        name="out_proj",
    )(o_self, o_cross, w_out, x)


def _mla_in_kernel(h_ref, w_ref, gcq_ref, gckv_ref, gqm_ref, cq_ref, ckv_ref, qm_ref, kpe_ref):
    acc = jnp.dot(h_ref[...], w_ref[...], preferred_element_type=F32)
    a = MLA_Q_RANK
    b = a + MLA_KV_RANK
    c = b + CROSS_WIDTH
    cq_ref[...] = (_rms(acc[:, :a]) * gcq_ref[...]).astype(cq_ref.dtype)
    ckv_ref[...] = (_rms(acc[:, a:b]) * gckv_ref[...]).astype(ckv_ref.dtype)
    for j in range(CROSS_HEADS):
        sl = slice(j * CROSS_HEAD_DIM, (j + 1) * CROSS_HEAD_DIM)
        qm_ref[:, sl] = (_rms(acc[:, b + j * CROSS_HEAD_DIM:b + (j + 1) * CROSS_HEAD_DIM])
                         * gqm_ref[:, sl]).astype(qm_ref.dtype)
    kpe_ref[...] = acc[:, c:]


def _mla_in_call(h, wcat, gcq, gckv, gqm, *, tm=256):
    m = h.shape[0]
    ncat = wcat.shape[1]
    row = lambda w: pl.BlockSpec((tm, w), lambda i: (i, 0))
    full = lambda r, w: pl.BlockSpec((r, w), lambda i: (0, 0))
    return pl.pallas_call(
        _mla_in_kernel,
        grid=(m // tm,),
        in_specs=[row(D_MODEL), full(D_MODEL, ncat), full(1, MLA_Q_RANK), full(1, MLA_KV_RANK),
                  full(1, CROSS_WIDTH)],
        out_specs=[row(MLA_Q_RANK), row(MLA_KV_RANK), row(CROSS_WIDTH), row(LANES)],
        out_shape=[jax.ShapeDtypeStruct((m, MLA_Q_RANK), BF16),
                   jax.ShapeDtypeStruct((m, MLA_KV_RANK), BF16),
                   jax.ShapeDtypeStruct((m, CROSS_WIDTH), BF16),
                   jax.ShapeDtypeStruct((m, LANES), F32)],
        compiler_params=_cparams(1, VMEM_LIMIT),
        name="mla_in_proj",
    )(h, wcat, gcq, gckv, gqm)


MLA_GROUP = 4


def _rope_pe(x, c, s):
    lane = lax.broadcasted_iota(I32, x.shape, 1)
    partner = jnp.where((lane & 63) < 32, pltpu.roll(x, 96, axis=1), pltpu.roll(x, 32, axis=1))
    return x * c + partner * s


def _mla_q_kernel(cq_ref, wn_ref, wp_ref, gn_ref, gp_ref, cos_ref, sin_ref, o_ref):
    x = cq_ref[...]
    an = jnp.dot(x, wn_ref[...], preferred_element_type=F32)
    ap = jnp.dot(x, wp_ref[...], preferred_element_type=F32)
    lane = lax.broadcasted_iota(I32, (x.shape[0], LANES), 1)
    low = lane < 64
    for j in range(MLA_GROUP):
        nj = an[:, j * 128:(j + 1) * 128]
        pc = ap[:, (j // 2) * 128:(j // 2 + 1) * 128]
        mine = low if j % 2 == 0 else jnp.logical_not(low)
        ss = (jnp.sum(nj * nj, axis=-1, keepdims=True)
              + jnp.sum(jnp.where(mine, pc * pc, 0.0), axis=-1, keepdims=True))
        rs = lax.rsqrt(ss * (1.0 / MLA_QK_DIM) + NORM_EPS)
        pe = _rope_pe(pc * gp_ref[...], cos_ref[...], sin_ref[...])
        if j % 2 == 1:
            pe = pltpu.roll(pe, 64, axis=1)
        pe = jnp.where(low, pe * rs, 0.0)
        o_ref[:, j * 256:j * 256 + 128] = (nj * rs * gn_ref[...]).astype(o_ref.dtype)
        o_ref[:, j * 256 + 128:(j + 1) * 256] = pe.astype(o_ref.dtype)


def _mla_q_call(cq, w_nope, w_pe, gn, gp, cos, sin, *, tm=512):
    m = cq.shape[0]
    g = MLA_HEADS // MLA_GROUP
    return pl.pallas_call(
        _mla_q_kernel,
        grid=(m // tm, g),
        in_specs=[pl.BlockSpec((tm, MLA_Q_RANK), lambda i, j: (i, 0)),
                  pl.BlockSpec((MLA_Q_RANK, MLA_GROUP * 128), lambda i, j: (0, j)),
                  pl.BlockSpec((MLA_Q_RANK, MLA_GROUP * 64), lambda i, j: (0, j)),
                  pl.BlockSpec((1, LANES), lambda i, j: (0, 0)),
                  pl.BlockSpec((1, LANES), lambda i, j: (0, 0)),
                  pl.BlockSpec((tm, LANES), lambda i, j: (i, 0)),
                  pl.BlockSpec((tm, LANES), lambda i, j: (i, 0))],
        out_specs=pl.BlockSpec((tm, MLA_GROUP * MLA_PAD_DIM), lambda i, j: (i, j)),
        out_shape=jax.ShapeDtypeStruct((m, MLA_HEADS * MLA_PAD_DIM), BF16),
        compiler_params=_cparams(2),
        name="mla_q_up",
    )(cq, w_nope, w_pe, gn, gp, cos, sin)


def _mla_kv_kernel(ckv_ref, w_ref, kpe_ref, gn_ref, gp_ref, cos_ref, sin_ref, k_ref, v_ref):
    acc = jnp.dot(ckv_ref[...], w_ref[...], preferred_element_type=F32)
    kpe = kpe_ref[...]
    ss_pe = jnp.sum(kpe * kpe, axis=-1, keepdims=True)
    pe = _rope_pe(kpe * gp_ref[...], cos_ref[...], sin_ref[...])
    for j in range(MLA_GROUP):
        kn = acc[:, j * 256:j * 256 + 128]
        ss = jnp.sum(kn * kn, axis=-1, keepdims=True) + ss_pe
        rs = lax.rsqrt(ss * (1.0 / MLA_QK_DIM) + NORM_EPS)
        k_ref[:, j * 256:j * 256 + 128] = (kn * rs * gn_ref[...]).astype(k_ref.dtype)
        k_ref[:, j * 256 + 128:(j + 1) * 256] = (pe * rs).astype(k_ref.dtype)
        v_ref[:, j * 128:(j + 1) * 128] = acc[:, j * 256 + 128:(j + 1) * 256].astype(v_ref.dtype)


def _mla_kv_call(ckv, w_ukv, kpe, gn, gp, cos, sin, *, tm=512):
    m = ckv.shape[0]
    g = MLA_HEADS // MLA_GROUP
    return pl.pallas_call(
        _mla_kv_kernel,
        grid=(m // tm, g),
        in_specs=[pl.BlockSpec((tm, MLA_KV_RANK), lambda i, j: (i, 0)),
                  pl.BlockSpec((MLA_KV_RANK, MLA_GROUP * 256), lambda i, j: (0, j)),
                  pl.BlockSpec((tm, LANES), lambda i, j: (i, 0)),
                  pl.BlockSpec((1, LANES), lambda i, j: (0, 0)),
                  pl.BlockSpec((1, LANES), lambda i, j: (0, 0)),
                  pl.BlockSpec((tm, LANES), lambda i, j: (i, 0)),
                  pl.BlockSpec((tm, LANES), lambda i, j: (i, 0))],
        out_specs=[pl.BlockSpec((tm, MLA_GROUP * MLA_PAD_DIM), lambda i, j: (i, j)),
                   pl.BlockSpec((tm, MLA_GROUP * MLA_V_DIM), lambda i, j: (i, j))],
        out_shape=[jax.ShapeDtypeStruct((m, MLA_HEADS * MLA_PAD_DIM), BF16),
                   jax.ShapeDtypeStruct((m, MLA_HEADS * MLA_V_DIM), BF16)],
        compiler_params=_cparams(2),
        name="mla_kv_up",
    )(ckv, w_ukv, kpe, gn, gp, cos, sin)


def _route_kernel(x_ref, g_ref, w_ref, b_ref, xn_ref, ids_ref, wts_ref):
    xn = _rms(x_ref[...]) * g_ref[...]
    xn_ref[...] = xn
    x_hi = xn.astype(BF16)
    x_lo = (xn - x_hi.astype(F32)).astype(BF16)
    w = w_ref[...]
    w_hi = w.astype(BF16)
    w_lo = (w - w_hi.astype(F32)).astype(BF16)
    lg = _dot_nt(w_hi, x_hi) + _dot_nt(w_hi, x_lo) + _dot_nt(w_lo, x_hi) + b_ref[...]
    tm = lg.shape[1]
    iota = lax.broadcasted_iota(I32, (EXPERTS_PER_GROUP, tm), 0)

    def first_argmax(v):
        mx = jnp.max(v, axis=0, keepdims=True)
        idx = jnp.min(jnp.where(v == mx, iota, EXPERTS_PER_GROUP), axis=0, keepdims=True)
        return mx, idx

    gl = lg[0:N_GROUPS, :]
    gmax, gsel = first_argmax(gl)
    g_gate = 1.0 / jnp.sum(jnp.exp(gl - gmax), axis=0, keepdims=True)
    el = jnp.zeros((EXPERTS_PER_GROUP, tm), F32)
    for g in range(N_GROUPS):
        lo = N_GROUPS + g * EXPERTS_PER_GROUP
        el = jnp.where(gsel == g, lg[lo:lo + EXPERTS_PER_GROUP, :], el)
    v1, i1 = first_argmax(el)
    v2, i2 = first_argmax(jnp.where(iota == i1, -jnp.inf, el))
    e = jnp.exp(v2 - v1)
    w1 = g_gate / (1.0 + e)
    w2 = g_gate * e / (1.0 + e)
    ids_ref[...] = jnp.concatenate([gsel * EXPERTS_PER_GROUP + i1,
                                    gsel * EXPERTS_PER_GROUP + i2], axis=0)
    wts_ref[...] = jnp.concatenate([w1, w2], axis=0)


def _route_call(x, g, w_t, b_col, *, tm=256):
    m = x.shape[0]
    return pl.pallas_call(
        _route_kernel,
        grid=(m // tm,),
        in_specs=[pl.BlockSpec((tm, D_MODEL), lambda i: (i, 0)),
                  pl.BlockSpec((1, D_MODEL), lambda i: (0, 0)),
                  pl.BlockSpec((LANES, D_MODEL), lambda i: (0, 0)),
                  pl.BlockSpec((LANES, 1), lambda i: (0, 0))],
        out_specs=[pl.BlockSpec((tm, D_MODEL), lambda i: (i, 0)),
                   pl.BlockSpec((TOP_K, tm), lambda i: (0, i)),
                   pl.BlockSpec((TOP_K, tm), lambda i: (0, i))],
        out_shape=[jax.ShapeDtypeStruct((m, D_MODEL), F32),
                   jax.ShapeDtypeStruct((TOP_K, m), I32),
                   jax.ShapeDtypeStruct((TOP_K, m), F32)],
        compiler_params=_cparams(1, VMEM_LIMIT),
        name="moe_route",
    )(x, g, w_t, b_col)


def _sort_kernel(ids_ref, pos_ref, te_ref, nu_ref, tc_ref, *, ts):
    n_tok = ids_ref.shape[1]
    nck = n_tok // ts
    iota_e = lax.broadcasted_iota(I32, (N_EXPERTS, ts), 0)

    def chunk(c):
        return pl.ds(pl.multiple_of(c * ts, ts), ts)

    def count_body(c, acc):
        for k in range(TOP_K):
            oh = jnp.where(iota_e == ids_ref[pl.ds(k, 1), chunk(c)], 1.0, 0.0)
            acc = acc + jnp.sum(oh, axis=1, keepdims=True)
        return acc

    counts = lax.fori_loop(0, nck, count_body, jnp.zeros((N_EXPERTS, 1), F32))
    tiles = jnp.floor((counts + (MOE_TM - 0.5)) * (1.0 / MOE_TM))
    er = lax.broadcasted_iota(I32, (N_EXPERTS, N_EXPERTS), 0)
    ec = lax.broadcasted_iota(I32, (N_EXPERTS, N_EXPERTS), 1)
    strict_lower = jnp.where(ec < er, 1.0, 0.0).astype(BF16)
    tiles_b = jnp.broadcast_to(tiles, (N_EXPERTS, LANES))
    first = jnp.dot(strict_lower, tiles_b.astype(BF16), preferred_element_type=F32)
    ends = first + tiles_b
    total = jnp.max(ends, axis=0, keepdims=True)
    tile_i = lax.broadcasted_iota(I32, (N_EXPERTS, LANES), 1).astype(F32)
    tile_c = jnp.minimum(tile_i, total - 1.0)
    te = jnp.sum(jnp.where(ends <= tile_c, 1.0, 0.0), axis=0, keepdims=True)
    te_ref[...] = te.astype(I32)
    nu_ref[...] = total.astype(I32)
    mine = lax.broadcasted_iota(I32, (N_EXPERTS, LANES), 0).astype(F32) == te
    left = jnp.sum(jnp.where(mine, counts - (tile_i - first) * MOE_TM, 0.0), axis=0, keepdims=True)
    left = jnp.where(tile_i[0:1, :] < total, jnp.clip(left, 0.0, float(MOE_TM)), 0.0)
    tc_ref[...] = left.astype(I32)

    row_base = first[:, 0:1] * MOE_TM
    ur = lax.broadcasted_iota(I32, (ts, ts), 0)
    uc = lax.broadcasted_iota(I32, (ts, ts), 1)
    upper = jnp.where(ur <= uc, 1.0, 0.0).astype(BF16)

    def pos_body(k):
        def body(c, carry):
            hit = iota_e == ids_ref[pl.ds(k, 1), chunk(c)]
            incl = jnp.dot(jnp.where(hit, 1.0, 0.0).astype(BF16), upper,
                           preferred_element_type=F32)
            val = row_base + carry + incl - 1.0
            p = jnp.sum(jnp.where(hit, val, 0.0), axis=0, keepdims=True)
            pos_ref[pl.ds(k, 1), chunk(c)] = p.astype(I32)
            return carry + incl[:, ts - 1:ts]
        return body

    carry = jnp.zeros((N_EXPERTS, 1), F32)
    for k in range(TOP_K):
        carry = lax.fori_loop(0, nck, pos_body(k), carry)


def _sort_call(ids, *, ts=512):
    n_tok = ids.shape[1]
    ts = min(ts, n_tok)
    return pl.pallas_call(
        functools.partial(_sort_kernel, ts=ts),
        out_shape=[jax.ShapeDtypeStruct((TOP_K, n_tok), I32),
                   jax.ShapeDtypeStruct((1, LANES), I32),
                   jax.ShapeDtypeStruct((1, LANES), I32),
                   jax.ShapeDtypeStruct((1, LANES), I32)],
        name="moe_sort",
    )(ids)


def _moe_tiles(n_tok):
    return -(-(TOP_K * n_tok + N_EXPERTS * (MOE_TM - 1)) // MOE_TM)


def _invert_kernel(pos_ref, inv_ref):
    def init(i, c):
        inv_ref[i] = jnp.int32(-1)
        return c

    lax.fori_loop(0, inv_ref.shape[0], init, 0, unroll=8)

    def put(s, c):
        inv_ref[pos_ref[s]] = s
        return c

    lax.fori_loop(0, pos_ref.shape[0], put, 0, unroll=8)


def _invert_call(pos_flat, n_rows):
    return pl.pallas_call(
        _invert_kernel,
        in_specs=[pl.BlockSpec(memory_space=pltpu.SMEM)],
        out_specs=pl.BlockSpec(memory_space=pltpu.SMEM),
        out_shape=jax.ShapeDtypeStruct((n_rows,), I32),
        name="moe_invert",
    )(pos_flat)


def _moe_kernel(te_ref, nu_ref, tc_ref, inv_ref, xn_hbm, wg_ref, wu_ref, wd_ref, y_hbm,
                xbuf, xb, acc, wgu, gsem, ssem, *, n_tok):
    i = pl.program_id(0)
    f = pl.program_id(1)
    nf = pl.num_programs(1)
    n_used = nu_ref[0]
    valid = i < n_used
    slot = i % 2

    def issue_gather(tile, sl):
        def body(r, c):
            s = inv_ref[tile * MOE_TM + r]
            tok = jnp.where(s >= n_tok, s - n_tok, s)
            pltpu.make_async_copy(xn_hbm.at[pl.ds(tok, 1)], xbuf.at[sl, pl.ds(r, 1)],
                                  gsem.at[sl]).start()
            return c
        lax.fori_loop(0, tc_ref[tile], body, 0)

    def wait_rows(n, copy_of):
        n8 = pl.multiple_of((n >> 3) << 3, 8)

        @pl.when(n8 > 0)
        def _():
            copy_of(n8).wait()

        def body(r, c):
            copy_of(1).wait()
            return c
        lax.fori_loop(0, n - n8, body, 0)

    def wait_gather(tile, sl):
        wait_rows(tc_ref[tile], lambda n: pltpu.make_async_copy(
            xn_hbm.at[pl.ds(0, n)], xbuf.at[sl, pl.ds(0, n)], gsem.at[sl]))

    def issue_scatter(tile):
        def body(r, c):
            s = inv_ref[tile * MOE_TM + r]
            pltpu.make_async_copy(acc.at[pl.ds(r, 1)], y_hbm.at[pl.ds(s, 1)], ssem.at[0]).start()
            return c
        lax.fori_loop(0, tc_ref[tile], body, 0)

    def wait_scatter(tile):
        wait_rows(tc_ref[tile], lambda n: pltpu.make_async_copy(
            acc.at[pl.ds(0, n)], y_hbm.at[pl.ds(0, n)], ssem.at[0]))

    @pl.when((i == 0) & (f == 0))
    def _():
        xbuf[...] = jnp.zeros(xbuf.shape, xbuf.dtype)
        issue_gather(0, 0)

    @pl.when(valid & (f == 0))
    def _():
        wait_gather(i, slot)

        @pl.when(i + 1 < n_used)
        def _():
            issue_gather(i + 1, 1 - slot)

        xb[...] = xbuf[slot].astype(BF16)

    @pl.when(valid)
    def _():
        wgu[:, :MOE_TF] = wg_ref[0].astype(BF16)
        wgu[:, MOE_TF:] = wu_ref[0].astype(BF16)
        au = jnp.dot(xb[...], wgu[...], preferred_element_type=F32)
        a = au[:, :MOE_TF]
        hid = (a * jax.nn.sigmoid(a)) * au[:, MOE_TF:]
        part = jnp.dot(hid.astype(BF16), wd_ref[0].astype(BF16), preferred_element_type=F32)

        @pl.when(f == 0)
        def _():
            @pl.when(i > 0)
            def _():
                wait_scatter(i - 1)
            acc[...] = part

        @pl.when(f > 0)
        def _():
            acc[...] += part

        @pl.when(f == nf - 1)
        def _():
            issue_scatter(i)

            @pl.when(i == n_used - 1)
            def _():
                wait_scatter(i)


def _moe_call(te, nu, tc, inv, xn, w_gate, w_up, w_down):
    n_tok = xn.shape[0]
    n_tiles = inv.shape[0] // MOE_TM
    nf = D_EXPERT // MOE_TF
    last_f = nf - 1

    def w_in_map(i, f, te_r, nu_r, tc_r, inv_r):
        return te_r[i], 0, jnp.where(i < nu_r[0], f, last_f)

    def w_out_map(i, f, te_r, nu_r, tc_r, inv_r):
        return te_r[i], jnp.where(i < nu_r[0], f, last_f), 0

    grid_spec = pltpu.PrefetchScalarGridSpec(
        num_scalar_prefetch=4,
        grid=(n_tiles, nf),
        in_specs=[
            pl.BlockSpec(memory_space=pl.ANY),
            pl.BlockSpec((1, D_MODEL, MOE_TF), w_in_map),
            pl.BlockSpec((1, D_MODEL, MOE_TF), w_in_map),
            pl.BlockSpec((1, MOE_TF, D_MODEL), w_out_map),
        ],
        out_specs=pl.BlockSpec(memory_space=pl.ANY),
        scratch_shapes=[pltpu.VMEM((2, MOE_TM, D_MODEL), F32),
                        pltpu.VMEM((MOE_TM, D_MODEL), BF16),
                        pltpu.VMEM((MOE_TM, D_MODEL), F32),
                        pltpu.VMEM((D_MODEL, 2 * MOE_TF), BF16),
                        pltpu.SemaphoreType.DMA((2,)),
                        pltpu.SemaphoreType.DMA((1,))],
    )
    return pl.pallas_call(
        functools.partial(_moe_kernel, n_tok=n_tok),
        grid_spec=grid_spec,
        out_shape=jax.ShapeDtypeStruct((TOP_K * n_tok, D_MODEL), F32),
        compiler_params=_cparams(2, VMEM_LIMIT),
        name="moe_experts",
    )(te, nu, tc, inv, xn, w_gate, w_up, w_down)


def _combine_kernel(x_ref, y0_ref, y1_ref, w_ref, g_ref, xo_ref, *h_ref):
    w = w_ref[...]
    x = x_ref[...] + w[:, 0:1] * y0_ref[...] + w[:, 1:2] * y1_ref[...]
    xo_ref[...] = x
    if h_ref:
        h_ref[0][...] = (_rms(x) * g_ref[...]).astype(BF16)


def _combine_call(x, y, wts_t, g_next, *, tm=256):
    m, d = x.shape
    nb = m // tm
    with_norm = g_next is not None
    g = g_next.reshape(1, d) if with_norm else jnp.ones((1, d), F32)
    row = pl.BlockSpec((tm, d), lambda i: (i, 0))
    out_specs = [row, row] if with_norm else [row]
    out_shape = [jax.ShapeDtypeStruct((m, d), F32)]
    if with_norm:
        out_shape.append(jax.ShapeDtypeStruct((m, d), BF16))
    return pl.pallas_call(
        _combine_kernel,
        grid=(nb,),
        in_specs=[row, row,
                  pl.BlockSpec((tm, d), lambda i: (i + nb, 0)),
                  pl.BlockSpec((tm, TOP_K), lambda i: (i, 0)),
                  pl.BlockSpec((1, d), lambda i: (0, 0))],
        out_specs=out_specs,
        out_shape=out_shape,
        compiler_params=_cparams(1, VMEM_LIMIT),
        name="moe_combine",
    )(x, y, y, wts_t, g)


def _moe_layer(x, norm_g, w_group, b_group, w_router, b_router, w_gate, w_up, w_down, g_next):
    n_tok = x.shape[0]
    pad_rows = LANES - N_GROUPS - N_EXPERTS
    w_t = jnp.concatenate([w_group.T, w_router.T, jnp.zeros((pad_rows, D_MODEL), F32)], axis=0)
    b_col = jnp.concatenate([b_group, b_router, jnp.zeros((pad_rows,), F32)]).reshape(LANES, 1)
    xn, ids, wts = _route_call(x, norm_g.reshape(1, D_MODEL), w_t, b_col)
    pos, te, nu, tc = _sort_call(ids)
    inv = _invert_call(pos.reshape(-1), _moe_tiles(n_tok) * MOE_TM)
    y = _moe_call(te.reshape(-1), nu.reshape(-1)[:1], tc.reshape(-1), inv,
                  xn, w_gate.reshape(N_EXPERTS, D_MODEL, D_EXPERT),
                  w_up.reshape(N_EXPERTS, D_MODEL, D_EXPERT),
                  w_down.reshape(N_EXPERTS, D_EXPERT, D_MODEL))
    return _combine_call(x, y, wts.T, g_next)


def _rope_tables(pos_flat, dim):
    inv_freq = ROPE_THETA ** (-jnp.arange(0, dim, 2, dtype=F32) / dim)
    ang = pos_flat.astype(F32)[:, None] * inv_freq
    return jnp.cos(ang), jnp.sin(ang)


def kernel(x, mem, positions, mem_norm_g, w_mem_kv, mem_k_norm_g, l0_attn_norm_g, l0_w_in, l0_q_norm_g, l0_k_norm_g, l0_lambda_q1, l0_lambda_k1, l0_lambda_q2, l0_lambda_k2, l0_subln_g, l0_cross_q_norm_g, l0_w_out, l0_ffn_norm_g, l0_w_group, l0_b_group, l0_w_router, l0_b_router, l0_w_gate, l0_w_up, l0_w_down, l1_attn_norm_g, l1_w_in, l1_cq_norm_g, l1_ckv_norm_g, l1_w_uq, l1_w_ukv, l1_q_norm_g, l1_k_norm_g, l1_cross_q_norm_g, l1_w_out, l1_ffn_norm_g, l1_w_group, l1_b_group, l1_w_router, l1_b_router, l1_w_gate, l1_w_up, l1_w_down):
    batch, seq, d = x.shape
    n_tok = batch * seq
    xf = x.reshape(n_tok, d)
    pos_flat = positions.reshape(n_tok)
    posq = pos_flat.reshape(n_tok, 1)
    posk = pos_flat.reshape(1, n_tok)
    ones128 = jnp.ones((n_tok, LANES), F32)

    c64, s64 = _rope_tables(pos_flat, DIFF_HEAD_DIM)
    cos_full = jnp.concatenate([c64, c64], axis=1)
    sin_full = jnp.concatenate([-s64, s64], axis=1)
    c32, s32 = _rope_tables(pos_flat, MLA_ROPE_DIM)
    z32 = jnp.zeros_like(c32)
    cos_q = jnp.concatenate([c32, c32, c32, c32], axis=1)
    sin_q = jnp.concatenate([-s32, s32, -s32, s32], axis=1)
    cos_k = jnp.concatenate([c32, c32, z32, z32], axis=1)
    sin_k = jnp.concatenate([-s32, s32, z32, z32], axis=1)

    cross_scale = CROSS_HEAD_DIM ** -0.5

    memn = _norm_call(mem.reshape(batch * N_MEM, d), mem_norm_g)
    mem_gain = jnp.concatenate([jnp.tile(mem_k_norm_g, CROSS_HEADS), jnp.ones((CROSS_WIDTH,), F32)])
    k_tiles = CROSS_WIDTH // 512
    memkv = _mm_call(memn, w_mem_kv, mem_gain.reshape(1, -1), ones128[:batch * N_MEM],
                     ones128[:batch * N_MEM], ((0, k_tiles, "head256"), (k_tiles, 2 * k_tiles, "plain")),
                     tm=batch * N_MEM, tn=512, name="mem_kv")

    h0 = _norm_call(xf, l0_attn_norm_g)
    qk_tiles = SELF_WIDTH // 512
    gain0 = jnp.concatenate([
        jnp.tile(l0_q_norm_g, 2 * DIFF_HEADS) * (DIFF_HEAD_DIM ** -0.5),
        jnp.tile(l0_k_norm_g, 2 * DIFF_HEADS),
        jnp.ones((SELF_WIDTH,), F32),
        jnp.tile(l0_cross_q_norm_g, CROSS_HEADS) * cross_scale]).reshape(1, -1)
    proj0 = _mm_call(h0, l0_w_in, gain0, cos_full, sin_full,
                     ((0, 2 * qk_tiles, "head128_rope"), (2 * qk_tiles, 3 * qk_tiles, "plain"),
                      (3 * qk_tiles, 3 * qk_tiles + CROSS_WIDTH // 512, "head256")),
                     tm=1024, tn=512, name="l0_in_proj")
    lam_init = 0.8 - 0.6 * math.exp(-0.3 * 0)
    row = lambda v: v.reshape(1, -1)
    o_self = _attn_call(proj0, proj0, proj0, posq, posk,
                        (row(l0_lambda_q1), row(l0_lambda_k1), row(l0_lambda_q2),
                         row(l0_lambda_k2), row(l0_subln_g)),
                        batch=batch, seq=seq, heads=DIFF_HEADS, n_maps=2, dk=DIFF_HEAD_DIM,
                        dv=DIFF_V_DIM, qcol=0, kcol=DIFF_HEADS, vcol=2 * DIFF_HEADS,
                        lam_init=lam_init, name="diff_attn")
    o_cross = _cross_call(proj0, memkv, batch=batch, seq=seq, qcol=3 * SELF_WIDTH // CROSS_WIDTH)
    x1 = _outproj_call(o_self, o_cross, l0_w_out, xf)
    x2, h1 = _moe_layer(x1, l0_ffn_norm_g, l0_w_group, l0_b_group, l0_w_router, l0_b_router,
                        l0_w_gate, l0_w_up, l0_w_down, l1_attn_norm_g)

    a = MLA_Q_RANK
    b = a + MLA_KV_RANK
    c = b + MLA_ROPE_DIM
    wcat = jnp.concatenate([l1_w_in[:, :b], l1_w_in[:, c:], l1_w_in[:, b:c],
                            jnp.zeros((d, LANES - MLA_ROPE_DIM), F32)], axis=1).astype(BF16)
    cq, ckv, qm, kpe = _mla_in_call(
        h1, wcat, row(l1_cq_norm_g), row(l1_ckv_norm_g),
        row(jnp.tile(l1_cross_q_norm_g, CROSS_HEADS) * cross_scale))
    w_uq3 = l1_w_uq.reshape(MLA_Q_RANK, MLA_HEADS, MLA_QK_DIM)
    w_q_nope = w_uq3[:, :, :MLA_NOPE_DIM].reshape(MLA_Q_RANK, -1).astype(BF16)
    w_q_pe = w_uq3[:, :, MLA_NOPE_DIM:].reshape(MLA_Q_RANK, -1).astype(BF16)
    q_scale = MLA_QK_DIM ** -0.5
    zeros64 = jnp.zeros((MLA_ROPE_DIM,), F32)
    q_pad = _mla_q_call(cq, w_q_nope, w_q_pe,
                        row(l1_q_norm_g[:MLA_NOPE_DIM] * q_scale),
                        row(jnp.tile(l1_q_norm_g[MLA_NOPE_DIM:], 2) * q_scale), cos_q, sin_q)
    k_pad, v1 = _mla_kv_call(ckv, l1_w_ukv.astype(BF16), kpe,
                             row(l1_k_norm_g[:MLA_NOPE_DIM]),
                             row(jnp.concatenate([l1_k_norm_g[MLA_NOPE_DIM:], zeros64])),
                             cos_k, sin_k)
    o_self1 = _attn_call(q_pad, k_pad, v1, posq, posk, (),
                         batch=batch, seq=seq, heads=MLA_HEADS, n_maps=1, dk=MLA_PAD_DIM,
                         dv=MLA_V_DIM, qcol=0, kcol=0, vcol=0, lam_init=0.0, name="mla_attn")
    o_cross1 = _cross_call(qm, memkv, batch=batch, seq=seq, qcol=0)
    x3 = _outproj_call(o_self1, o_cross1, l1_w_out, x2)
    (x4,) = _moe_layer(x3, l1_ffn_norm_g, l1_w_group, l1_b_group, l1_w_router, l1_b_router,
                       l1_w_gate, l1_w_up, l1_w_down, None)
    return x4.reshape(batch, seq, d)
```

```python
import functools
import math

import jax
import jax.numpy as jnp
from jax import lax
from jax.experimental import pallas as pl
from jax.experimental.pallas import tpu as pltpu

F32 = jnp.float32
BF16 = jnp.bfloat16
I32 = jnp.int32

D_MODEL = 4096
N_MEM = 256
ROPE_THETA = 10000.0
NORM_EPS = 1e-6

SELF_WIDTH = 3 * D_MODEL // 4
CROSS_HEADS = 4
CROSS_HEAD_DIM = (D_MODEL // 4) // CROSS_HEADS
CROSS_WIDTH = CROSS_HEADS * CROSS_HEAD_DIM

DIFF_HEAD_DIM = 128
DIFF_V_DIM = 2 * DIFF_HEAD_DIM
DIFF_HEADS = SELF_WIDTH // DIFF_V_DIM

MLA_NOPE_DIM = 128
MLA_ROPE_DIM = 64
MLA_V_DIM = 128
MLA_HEADS = SELF_WIDTH // MLA_V_DIM
MLA_QK_DIM = MLA_NOPE_DIM + MLA_ROPE_DIM
MLA_Q_RANK = 3 * D_MODEL // 16
MLA_KV_RANK = D_MODEL // 8
MLA_PAD_DIM = 256

N_GROUPS = 8
EXPERTS_PER_GROUP = 8
N_EXPERTS = N_GROUPS * EXPERTS_PER_GROUP
D_EXPERT = 3 * D_MODEL // 32
TOP_K = 2

LANES = 128
MOE_TM = 320
MOE_TK = 1024
DMA_GROUP = 8
ATTN_TQ = 256
VMEM_LIMIT = 56 * 1024 * 1024


def _cparams(n_axes, vmem=None):
    return pltpu.CompilerParams(dimension_semantics=("arbitrary",) * n_axes,
                                vmem_limit_bytes=vmem)


def _rms(x):
    return x * lax.rsqrt(jnp.mean(x * x, axis=-1, keepdims=True) + NORM_EPS)


def _dot_nt(a, b):
    return lax.dot_general(a, b, (((1,), (1,)), ((), ())), preferred_element_type=F32)


def _norm_kernel(x_ref, g_ref, o_ref):
    o_ref[...] = (_rms(x_ref[...]) * g_ref[...]).astype(o_ref.dtype)


def _norm_call(x, g, tm=256):
    m, d = x.shape
    return pl.pallas_call(
        _norm_kernel,
        grid=(m // tm,),
        in_specs=[pl.BlockSpec((tm, d), lambda i: (i, 0)),
                  pl.BlockSpec((1, d), lambda i: (0, 0))],
        out_specs=pl.BlockSpec((tm, d), lambda i: (i, 0)),
        out_shape=jax.ShapeDtypeStruct((m, d), BF16),
        compiler_params=_cparams(1),
        name="rmsnorm",
    )(x, g.reshape(1, d))


def _epilogue(kind, acc, gain_ref, cos_ref, sin_ref, o_ref):
    tn = acc.shape[1]
    if kind == "plain":
        o_ref[...] = acc.astype(o_ref.dtype)
    elif kind == "head128_rope":
        c = cos_ref[...]
        s = sin_ref[...]
        for j in range(tn // 128):
            sl = slice(j * 128, (j + 1) * 128)
            y = _rms(acc[:, sl]) * gain_ref[:, sl]
            y = y * c + pltpu.roll(y, 64, axis=1) * s
            o_ref[:, sl] = y.astype(o_ref.dtype)
    elif kind == "head256":
        for j in range(tn // 256):
            sl = slice(j * 256, (j + 1) * 256)
            o_ref[:, sl] = (_rms(acc[:, sl]) * gain_ref[:, sl]).astype(o_ref.dtype)
    else:
        raise ValueError(kind)


def _mm_kernel(x_ref, w_ref, gain_ref, cos_ref, sin_ref, o_ref, wbf_ref, *, kinds):
    n = pl.program_id(0)

    @pl.when(pl.program_id(1) == 0)
    def _():
        wbf_ref[...] = w_ref[...].astype(BF16)

    acc = jnp.dot(x_ref[...], wbf_ref[...], preferred_element_type=F32)
    if len(kinds) == 1:
        _epilogue(kinds[0][2], acc, gain_ref, cos_ref, sin_ref, o_ref)
    else:
        for lo, hi, kind in kinds:
            pl.when((n >= lo) & (n < hi))(
                functools.partial(_epilogue, kind, acc, gain_ref, cos_ref, sin_ref, o_ref))


def _mm_call(x, w, gain, cos, sin, kinds, *, tm, tn, name):
    m, k = x.shape
    n = w.shape[1]
    return pl.pallas_call(
        functools.partial(_mm_kernel, kinds=kinds),
        grid=(n // tn, m // tm),
        in_specs=[pl.BlockSpec((tm, k), lambda j, i: (i, 0)),
                  pl.BlockSpec((k, tn), lambda j, i: (0, j)),
                  pl.BlockSpec((1, tn), lambda j, i: (0, j)),
                  pl.BlockSpec((tm, LANES), lambda j, i: (i, 0)),
                  pl.BlockSpec((tm, LANES), lambda j, i: (i, 0))],
        out_specs=pl.BlockSpec((tm, tn), lambda j, i: (i, j)),
        out_shape=jax.ShapeDtypeStruct((m, n), BF16),
        scratch_shapes=[pltpu.VMEM((k, tn), BF16)],
        compiler_params=_cparams(2, VMEM_LIMIT),
        name=name,
    )(x, w, gain, cos, sin)


def _attn_kernel(*refs, n_maps, dk, tq, nq, lam_init):
    if n_maps == 2:
        (q_ref, k_ref, v_ref, posq_ref, posk_ref, lq1_ref, lk1_ref, lq2_ref, lk2_ref, subg_ref,
         o_ref) = refs
        lam = (jnp.exp(jnp.sum(lq1_ref[...] * lk1_ref[...], axis=-1, keepdims=True))
               - jnp.exp(jnp.sum(lq2_ref[...] * lk2_ref[...], axis=-1, keepdims=True))
               + lam_init)
    else:
        q_ref, k_ref, v_ref, posq_ref, posk_ref, o_ref = refs

    for i in range(nq):
        rows = slice(i * tq, (i + 1) * tq)
        n_past = i * tq
        mask = posk_ref[:, rows] <= posq_ref[rows, :]
        probs = []
        for mi in range(n_maps):
            cols = slice(mi * dk, (mi + 1) * dk)
            qm = q_ref[rows, cols]
            s_diag = jnp.where(mask, _dot_nt(qm, k_ref[rows, cols]), -jnp.inf)
            mx = jnp.max(s_diag, axis=-1, keepdims=True)
            if n_past:
                s_past = _dot_nt(qm, k_ref[0:n_past, cols])
                mx = jnp.maximum(mx, jnp.max(s_past, axis=-1, keepdims=True))
            p_diag = jnp.exp(s_diag - mx)
            den = jnp.sum(p_diag, axis=-1, keepdims=True)
            p_past = None
            if n_past:
                p_past = jnp.exp(s_past - mx)
                den = den + jnp.sum(p_past, axis=-1, keepdims=True)
            probs.append((p_diag, p_past, 1.0 / den))

        def pv(w_diag, w_past):
            o = jnp.dot(w_diag.astype(BF16), v_ref[rows, :], preferred_element_type=F32)
            if n_past:
                o = o + jnp.dot(w_past.astype(BF16), v_ref[0:n_past, :],
                                preferred_element_type=F32)
            return o

        if n_maps == 2:
            (d1, p1, r1), (d2, p2, r2) = probs
            c2 = lam * r2
            o = pv(d1 * r1 - d2 * c2, (p1 * r1 - p2 * c2) if n_past else None)
            o = _rms(o) * subg_ref[...] * (1.0 - lam_init)
        else:
            (d1, p1, r1), = probs
            o = pv(d1, p1) * r1
        o_ref[rows, :] = o.astype(o_ref.dtype)


def _attn_call(q, k, v, posq, posk, extra, *, batch, seq, heads, n_maps, dk, dv,
               qcol, kcol, vcol, lam_init, name):
    tq = ATTN_TQ
    in_specs = [
        pl.BlockSpec((seq, n_maps * dk), lambda b, h: (b, qcol + h)),
        pl.BlockSpec((seq, n_maps * dk), lambda b, h: (b, kcol + h)),
        pl.BlockSpec((seq, dv), lambda b, h: (b, vcol + h)),
        pl.BlockSpec((seq, 1), lambda b, h: (b, 0)),
        pl.BlockSpec((1, seq), lambda b, h: (0, b)),
    ] + [pl.BlockSpec(e.shape, lambda b, h: (0, 0)) for e in extra]
    return pl.pallas_call(
        functools.partial(_attn_kernel, n_maps=n_maps, dk=dk, tq=tq, nq=seq // tq,
                          lam_init=lam_init),
        grid=(batch, heads),
        in_specs=in_specs,
        out_specs=pl.BlockSpec((seq, dv), lambda b, h: (b, h)),
        out_shape=jax.ShapeDtypeStruct((batch * seq, heads * dv), BF16),
        compiler_params=_cparams(2, VMEM_LIMIT),
        name=name,
    )(q, k, v, posq, posk, *extra)


def _cross_kernel(q_ref, k_ref, v_ref, o_ref):
    for h in range(CROSS_HEADS):
        sl = slice(h * CROSS_HEAD_DIM, (h + 1) * CROSS_HEAD_DIM)
        s = _dot_nt(q_ref[:, sl], k_ref[:, sl])
        p = jnp.exp(s - jnp.max(s, axis=-1, keepdims=True))
        inv = 1.0 / jnp.sum(p, axis=-1, keepdims=True)
        o = jnp.dot(p.astype(BF16), v_ref[:, sl], preferred_element_type=F32)
        o_ref[:, sl] = (o * inv).astype(o_ref.dtype)


def _cross_call(q, memkv, *, batch, seq, qcol, tq=512):
    nq = seq // tq
    return pl.pallas_call(
        _cross_kernel,
        grid=(batch, nq),
        in_specs=[pl.BlockSpec((tq, CROSS_WIDTH), lambda b, i: (b * nq + i, qcol)),
                  pl.BlockSpec((N_MEM, CROSS_WIDTH), lambda b, i: (b, 0)),
                  pl.BlockSpec((N_MEM, CROSS_WIDTH), lambda b, i: (b, 1))],
        out_specs=pl.BlockSpec((tq, CROSS_WIDTH), lambda b, i: (b * nq + i, 0)),
        out_shape=jax.ShapeDtypeStruct((batch * seq, CROSS_WIDTH), BF16),
        compiler_params=_cparams(2),
        name="cross_attn",
    )(q, memkv, memkv)


def _outproj_kernel(os_ref, oc_ref, w_ref, x_ref, o_ref, wbf_ref):
    @pl.when(pl.program_id(1) == 0)
    def _():
        wbf_ref[...] = w_ref[...].astype(BF16)

    acc = jnp.dot(os_ref[...], wbf_ref[:SELF_WIDTH, :], preferred_element_type=F32)
    acc = acc + jnp.dot(oc_ref[...], wbf_ref[SELF_WIDTH:, :], preferred_element_type=F32)
    o_ref[...] = x_ref[...] + acc


def _outproj_call(o_self, o_cross, w_out, x, *, tm=1024, tn=512):
    m = x.shape[0]
    return pl.pallas_call(
        _outproj_kernel,
        grid=(D_MODEL // tn, m // tm),
        in_specs=[pl.BlockSpec((tm, SELF_WIDTH), lambda j, i: (i, 0)),
                  pl.BlockSpec((tm, CROSS_WIDTH), lambda j, i: (i, 0)),
                  pl.BlockSpec((D_MODEL, tn), lambda j, i: (0, j)),
                  pl.BlockSpec((tm, tn), lambda j, i: (i, j))],
        out_specs=pl.BlockSpec((tm, tn), lambda j, i: (i, j)),
        out_shape=jax.ShapeDtypeStruct((m, D_MODEL), F32),
        scratch_shapes=[pltpu.VMEM((D_MODEL, tn), BF16)],
        compiler_params=_cparams(2, VMEM_LIMIT),
        name="out_proj",
    )(o_self, o_cross, w_out, x)


def _mla_in_kernel(h_ref, w_ref, gcq_ref, gckv_ref, gqm_ref, cq_ref, ckv_ref, qm_ref, kpe_ref):
    acc = jnp.dot(h_ref[...], w_ref[...], preferred_element_type=F32)
    a = MLA_Q_RANK
    b = a + MLA_KV_RANK
    c = b + CROSS_WIDTH
    cq_ref[...] = (_rms(acc[:, :a]) * gcq_ref[...]).astype(cq_ref.dtype)
    ckv_ref[...] = (_rms(acc[:, a:b]) * gckv_ref[...]).astype(ckv_ref.dtype)
    for j in range(CROSS_HEADS):
        sl = slice(j * CROSS_HEAD_DIM, (j + 1) * CROSS_HEAD_DIM)
        qm_ref[:, sl] = (_rms(acc[:, b + j * CROSS_HEAD_DIM:b + (j + 1) * CROSS_HEAD_DIM])
                         * gqm_ref[:, sl]).astype(qm_ref.dtype)
    kpe_ref[...] = acc[:, c:]


def _mla_in_call(h, wcat, gcq, gckv, gqm, *, tm=256):
    m = h.shape[0]
    ncat = wcat.shape[1]
    row = lambda w: pl.BlockSpec((tm, w), lambda i: (i, 0))
    full = lambda r, w: pl.BlockSpec((r, w), lambda i: (0, 0))
    return pl.pallas_call(
        _mla_in_kernel,
        grid=(m // tm,),
        in_specs=[row(D_MODEL), full(D_MODEL, ncat), full(1, MLA_Q_RANK), full(1, MLA_KV_RANK),
                  full(1, CROSS_WIDTH)],
        out_specs=[row(MLA_Q_RANK), row(MLA_KV_RANK), row(CROSS_WIDTH), row(LANES)],
        out_shape=[jax.ShapeDtypeStruct((m, MLA_Q_RANK), BF16),
                   jax.ShapeDtypeStruct((m, MLA_KV_RANK), BF16),
                   jax.ShapeDtypeStruct((m, CROSS_WIDTH), BF16),
                   jax.ShapeDtypeStruct((m, LANES), F32)],
        compiler_params=_cparams(1, VMEM_LIMIT),
        name="mla_in_proj",
    )(h, wcat, gcq, gckv, gqm)


MLA_GROUP = 4


def _rope_pe(x, c, s):
    lane = lax.broadcasted_iota(I32, x.shape, 1)
    partner = jnp.where((lane & 63) < 32, pltpu.roll(x, 96, axis=1), pltpu.roll(x, 32, axis=1))
    return x * c + partner * s


def _mla_q_kernel(cq_ref, wn_ref, wp_ref, wr_ref, gn_ref, gp_ref, gr_ref, cos_ref, sin_ref, o_ref):
    x = cq_ref[...]
    an = jnp.dot(x, wn_ref[...], preferred_element_type=F32)
    ap = jnp.dot(x, wp_ref[...], preferred_element_type=F32)
    ar = jnp.dot(x, wr_ref[...], preferred_element_type=F32)
    gc = cos_ref[...] * gp_ref[...]
    gs = sin_ref[...] * gr_ref[...]
    for j in range(MLA_GROUP):
        sl = slice(j * 128, (j + 1) * 128)
        nj = an[:, sl]
        pj = ap[:, sl]
        ss = jnp.sum(nj * nj + pj * pj, axis=-1, keepdims=True)
        rs = lax.rsqrt(ss * (1.0 / MLA_QK_DIM) + NORM_EPS)
        o_ref[:, j * 256:j * 256 + 128] = (nj * rs * gn_ref[...]).astype(o_ref.dtype)
        o_ref[:, j * 256 + 128:(j + 1) * 256] = ((pj * gc + ar[:, sl] * gs) * rs).astype(o_ref.dtype)


def _mla_q_call(cq, w_nope, w_pe, w_rot, gn, gp, gr, cos, sin, *, tm=512):
    m = cq.shape[0]
    g = MLA_HEADS // MLA_GROUP
    wspec = pl.BlockSpec((MLA_Q_RANK, MLA_GROUP * 128), lambda i, j: (0, j))
    gspec = pl.BlockSpec((1, LANES), lambda i, j: (0, 0))
    tspec = pl.BlockSpec((tm, LANES), lambda i, j: (i, 0))
    return pl.pallas_call(
        _mla_q_kernel,
        grid=(m // tm, g),
        in_specs=[pl.BlockSpec((tm, MLA_Q_RANK), lambda i, j: (i, 0)),
                  wspec, wspec, wspec, gspec, gspec, gspec, tspec, tspec],
        out_specs=pl.BlockSpec((tm, MLA_GROUP * MLA_PAD_DIM), lambda i, j: (i, j)),
        out_shape=jax.ShapeDtypeStruct((m, MLA_HEADS * MLA_PAD_DIM), BF16),
        compiler_params=_cparams(2),
        name="mla_q_up",
    )(cq, w_nope, w_pe, w_rot, gn, gp, gr, cos, sin)


def _mla_kv_kernel(ckv_ref, w_ref, kpe_ref, gn_ref, gp_ref, cos_ref, sin_ref, k_ref, v_ref):
    acc = jnp.dot(ckv_ref[...], w_ref[...], preferred_element_type=F32)
    kpe = kpe_ref[...]
    ss_pe = jnp.sum(kpe * kpe, axis=-1, keepdims=True)
    pe = _rope_pe(kpe * gp_ref[...], cos_ref[...], sin_ref[...])
    for j in range(MLA_GROUP):
        kn = acc[:, j * 256:j * 256 + 128]
        ss = jnp.sum(kn * kn, axis=-1, keepdims=True) + ss_pe
        rs = lax.rsqrt(ss * (1.0 / MLA_QK_DIM) + NORM_EPS)
        k_ref[:, j * 256:j * 256 + 128] = (kn * rs * gn_ref[...]).astype(k_ref.dtype)
        k_ref[:, j * 256 + 128:(j + 1) * 256] = (pe * rs).astype(k_ref.dtype)
        v_ref[:, j * 128:(j + 1) * 128] = acc[:, j * 256 + 128:(j + 1) * 256].astype(v_ref.dtype)


def _mla_kv_call(ckv, w_ukv, kpe, gn, gp, cos, sin, *, tm=512):
    m = ckv.shape[0]
    g = MLA_HEADS // MLA_GROUP
    return pl.pallas_call(
        _mla_kv_kernel,
        grid=(m // tm, g),
        in_specs=[pl.BlockSpec((tm, MLA_KV_RANK), lambda i, j: (i, 0)),
                  pl.BlockSpec((MLA_KV_RANK, MLA_GROUP * 256), lambda i, j: (0, j)),
                  pl.BlockSpec((tm, LANES), lambda i, j: (i, 0)),
                  pl.BlockSpec((1, LANES), lambda i, j: (0, 0)),
                  pl.BlockSpec((1, LANES), lambda i, j: (0, 0)),
                  pl.BlockSpec((tm, LANES), lambda i, j: (i, 0)),
                  pl.BlockSpec((tm, LANES), lambda i, j: (i, 0))],
        out_specs=[pl.BlockSpec((tm, MLA_GROUP * MLA_PAD_DIM), lambda i, j: (i, j)),
                   pl.BlockSpec((tm, MLA_GROUP * MLA_V_DIM), lambda i, j: (i, j))],
        out_shape=[jax.ShapeDtypeStruct((m, MLA_HEADS * MLA_PAD_DIM), BF16),
                   jax.ShapeDtypeStruct((m, MLA_HEADS * MLA_V_DIM), BF16)],
        compiler_params=_cparams(2),
        name="mla_kv_up",
    )(ckv, w_ukv, kpe, gn, gp, cos, sin)


def _route_kernel(x_ref, g_ref, w_ref, b_ref, xn_ref, ids_ref, wts_ref):
    xn = _rms(x_ref[...]) * g_ref[...]
    xn_ref[...] = xn
    x_hi = xn.astype(BF16)
    x_lo = (xn - x_hi.astype(F32)).astype(BF16)
    w = w_ref[...]
    w_hi = w.astype(BF16)
    w_lo = (w - w_hi.astype(F32)).astype(BF16)
    lg = _dot_nt(w_hi, x_hi) + _dot_nt(w_hi, x_lo) + _dot_nt(w_lo, x_hi) + b_ref[...]
    tm = lg.shape[1]
    iota = lax.broadcasted_iota(I32, (EXPERTS_PER_GROUP, tm), 0)

    def first_argmax(v):
        mx = jnp.max(v, axis=0, keepdims=True)
        idx = jnp.min(jnp.where(v == mx, iota, EXPERTS_PER_GROUP), axis=0, keepdims=True)
        return mx, idx

    gl = lg[0:N_GROUPS, :]
    gmax, gsel = first_argmax(gl)
    g_gate = 1.0 / jnp.sum(jnp.exp(gl - gmax), axis=0, keepdims=True)
    el = jnp.zeros((EXPERTS_PER_GROUP, tm), F32)
    for g in range(N_GROUPS):
        lo = N_GROUPS + g * EXPERTS_PER_GROUP
        el = jnp.where(gsel == g, lg[lo:lo + EXPERTS_PER_GROUP, :], el)
    v1, i1 = first_argmax(el)
    v2, i2 = first_argmax(jnp.where(iota == i1, -jnp.inf, el))
    e = jnp.exp(v2 - v1)
    w1 = g_gate / (1.0 + e)
    w2 = g_gate * e / (1.0 + e)
    ids_ref[...] = jnp.concatenate([gsel * EXPERTS_PER_GROUP + i1,
                                    gsel * EXPERTS_PER_GROUP + i2], axis=0)
    wts_ref[...] = jnp.concatenate([w1, w2], axis=0)


def _route_call(x, g, w_t, b_col, *, tm=256):
    m = x.shape[0]
    return pl.pallas_call(
        _route_kernel,
        grid=(m // tm,),
        in_specs=[pl.BlockSpec((tm, D_MODEL), lambda i: (i, 0)),
                  pl.BlockSpec((1, D_MODEL), lambda i: (0, 0)),
                  pl.BlockSpec((LANES, D_MODEL), lambda i: (0, 0)),
                  pl.BlockSpec((LANES, 1), lambda i: (0, 0))],
        out_specs=[pl.BlockSpec((tm, D_MODEL), lambda i: (i, 0)),
                   pl.BlockSpec((TOP_K, tm), lambda i: (0, i)),
                   pl.BlockSpec((TOP_K, tm), lambda i: (0, i))],
        out_shape=[jax.ShapeDtypeStruct((m, D_MODEL), F32),
                   jax.ShapeDtypeStruct((TOP_K, m), I32),
                   jax.ShapeDtypeStruct((TOP_K, m), F32)],
        compiler_params=_cparams(1, VMEM_LIMIT),
        name="moe_route",
    )(x, g, w_t, b_col)


def _sort_kernel(ids_ref, pos_ref, te_ref, nu_ref, tc_ref, *, ts):
    n_tok = ids_ref.shape[1]
    nck = n_tok // ts
    iota_e = lax.broadcasted_iota(I32, (N_EXPERTS, ts), 0)

    def chunk(c):
        return pl.ds(pl.multiple_of(c * ts, ts), ts)

    def count_body(c, acc):
        for k in range(TOP_K):
            oh = jnp.where(iota_e == ids_ref[pl.ds(k, 1), chunk(c)], 1.0, 0.0)
            acc = acc + jnp.sum(oh, axis=1, keepdims=True)
        return acc

    counts = lax.fori_loop(0, nck, count_body, jnp.zeros((N_EXPERTS, 1), F32))
    tiles = jnp.floor((counts + (MOE_TM - 0.5)) * (1.0 / MOE_TM))
    er = lax.broadcasted_iota(I32, (N_EXPERTS, N_EXPERTS), 0)
    ec = lax.broadcasted_iota(I32, (N_EXPERTS, N_EXPERTS), 1)
    strict_lower = jnp.where(ec < er, 1.0, 0.0).astype(BF16)
    tiles_b = jnp.broadcast_to(tiles, (N_EXPERTS, LANES))
    first = jnp.dot(strict_lower, tiles_b.astype(BF16), preferred_element_type=F32)
    ends = first + tiles_b
    total = jnp.max(ends, axis=0, keepdims=True)
    tile_i = lax.broadcasted_iota(I32, (N_EXPERTS, LANES), 1).astype(F32)
    tile_c = jnp.minimum(tile_i, total - 1.0)
    te = jnp.sum(jnp.where(ends <= tile_c, 1.0, 0.0), axis=0, keepdims=True)
    te_ref[...] = te.astype(I32)
    nu_ref[...] = total.astype(I32)
    mine = lax.broadcasted_iota(I32, (N_EXPERTS, LANES), 0).astype(F32) == te
    left = jnp.sum(jnp.where(mine, counts - (tile_i - first) * MOE_TM, 0.0), axis=0, keepdims=True)
    left = jnp.where(tile_i[0:1, :] < total, jnp.clip(left, 0.0, float(MOE_TM)), 0.0)
    tc_ref[...] = left.astype(I32)

    row_base = first[:, 0:1] * MOE_TM
    ur = lax.broadcasted_iota(I32, (ts, ts), 0)
    uc = lax.broadcasted_iota(I32, (ts, ts), 1)
    upper = jnp.where(ur <= uc, 1.0, 0.0).astype(BF16)

    def pos_body(k):
        def body(c, carry):
            hit = iota_e == ids_ref[pl.ds(k, 1), chunk(c)]
            incl = jnp.dot(jnp.where(hit, 1.0, 0.0).astype(BF16), upper,
                           preferred_element_type=F32)
            val = row_base + carry + incl - 1.0
            p = jnp.sum(jnp.where(hit, val, 0.0), axis=0, keepdims=True)
            pos_ref[pl.ds(k, 1), chunk(c)] = p.astype(I32)
            return carry + incl[:, ts - 1:ts]
        return body

    carry = jnp.zeros((N_EXPERTS, 1), F32)
    for k in range(TOP_K):
        carry = lax.fori_loop(0, nck, pos_body(k), carry)


def _sort_call(ids, *, ts=512):
    n_tok = ids.shape[1]
    ts = min(ts, n_tok)
    return pl.pallas_call(
        functools.partial(_sort_kernel, ts=ts),
        out_shape=[jax.ShapeDtypeStruct((TOP_K, n_tok), I32),
                   jax.ShapeDtypeStruct((1, LANES), I32),
                   jax.ShapeDtypeStruct((1, LANES), I32),
                   jax.ShapeDtypeStruct((1, LANES), I32)],
        name="moe_sort",
    )(ids)


def _moe_tiles(n_tok):
    return -(-(TOP_K * n_tok + N_EXPERTS * (MOE_TM - 1)) // MOE_TM)


def _invert_kernel(pos_ref, inv_ref):
    def init(i, c):
        inv_ref[i] = jnp.int32(-1)
        return c

    lax.fori_loop(0, inv_ref.shape[0], init, 0, unroll=8)

    def put(s, c):
        inv_ref[pos_ref[s]] = s
        return c

    lax.fori_loop(0, pos_ref.shape[0], put, 0, unroll=8)


def _invert_call(pos_flat, n_rows):
    return pl.pallas_call(
        _invert_kernel,
        in_specs=[pl.BlockSpec(memory_space=pltpu.SMEM)],
        out_specs=pl.BlockSpec(memory_space=pltpu.SMEM),
        out_shape=jax.ShapeDtypeStruct((n_rows,), I32),
        name="moe_invert",
    )(pos_flat)


def _moe_kernel(te_ref, nu_ref, tc_ref, inv_ref, xn_hbm, wg_ref, wu_ref, wd_ref, y_hbm,
                xbuf, xb, acc, gu, wgu, gsem, ssem, *, n_tok):
    i = pl.program_id(0)
    k = pl.program_id(1)
    nk = pl.num_programs(1)
    n_used = nu_ref[0]
    valid = i < n_used
    slot = i % 2

    def for_rows(n, fn):
        def group(g, c):
            for u in range(DMA_GROUP):
                fn(g * DMA_GROUP + u)
            return c
        n_groups = n // DMA_GROUP
        lax.fori_loop(0, n_groups, group, 0)

        def rest(r, c):
            fn(r)
            return c
        lax.fori_loop(n_groups * DMA_GROUP, n, rest, 0)

    def wait_rows(n, copy_of):
        n8 = pl.multiple_of((n >> 3) << 3, 8)

        @pl.when(n8 > 0)
        def _():
            copy_of(n8).wait()

        def body(r, c):
            copy_of(1).wait()
            return c
        lax.fori_loop(0, n - n8, body, 0)

    def issue_gather(tile, sl):
        def one(r):
            s = inv_ref[tile * MOE_TM + r]
            tok = jnp.where(s >= n_tok, s - n_tok, s)
            pltpu.make_async_copy(xn_hbm.at[pl.ds(tok, 1)], xbuf.at[sl, pl.ds(r, 1)],
                                  gsem.at[sl]).start()
        for_rows(tc_ref[tile], one)

    def wait_gather(tile, sl):
        wait_rows(tc_ref[tile], lambda n: pltpu.make_async_copy(
            xn_hbm.at[pl.ds(0, n)], xbuf.at[sl, pl.ds(0, n)], gsem.at[sl]))

    def issue_scatter(tile):
        def one(r):
            s = inv_ref[tile * MOE_TM + r]
            pltpu.make_async_copy(acc.at[pl.ds(r, 1)], y_hbm.at[pl.ds(s, 1)], ssem.at[0]).start()
        for_rows(tc_ref[tile], one)

    def wait_scatter(tile):
        wait_rows(tc_ref[tile], lambda n: pltpu.make_async_copy(
            acc.at[pl.ds(0, n)], y_hbm.at[pl.ds(0, n)], ssem.at[0]))

    @pl.when((i == 0) & (k == 0))
    def _():
        xbuf[...] = jnp.zeros(xbuf.shape, xbuf.dtype)
        issue_gather(0, 0)

    @pl.when(valid & (k == 0))
    def _():
        wait_gather(i, slot)

        @pl.when(i + 1 < n_used)
        def _():
            issue_gather(i + 1, 1 - slot)

        xb[...] = xbuf[slot].astype(BF16)

    @pl.when(valid)
    def _():
        wgu[:, :D_EXPERT] = wg_ref[0].astype(BF16)
        wgu[:, D_EXPERT:] = wu_ref[0].astype(BF16)
        xs = xb[:, pl.ds(pl.multiple_of(k * MOE_TK, MOE_TK), MOE_TK)]
        part = jnp.dot(xs, wgu[...], preferred_element_type=F32)

        @pl.when(k == 0)
        def _():
            gu[...] = part

        @pl.when(k > 0)
        def _():
            gu[...] += part

        @pl.when(k == nk - 1)
        def _():
            a = gu[:, :D_EXPERT]
            hid = (a * jax.nn.sigmoid(a)) * gu[:, D_EXPERT:]
            out = jnp.dot(hid.astype(BF16), wd_ref[0].astype(BF16), preferred_element_type=F32)

            @pl.when(i > 0)
            def _():
                wait_scatter(i - 1)
            acc[...] = out
            issue_scatter(i)

            @pl.when(i == n_used - 1)
            def _():
                wait_scatter(i)


def _moe_call(te, nu, tc, inv, xn, w_gate, w_up, w_down):
    n_tok = xn.shape[0]
    n_tiles = inv.shape[0] // MOE_TM
    nk = D_MODEL // MOE_TK
    last_k = nk - 1

    def w_in_map(i, k, te_r, nu_r, tc_r, inv_r):
        return te_r[i], jnp.where(i < nu_r[0], k, last_k), 0

    def w_out_map(i, k, te_r, nu_r, tc_r, inv_r):
        return te_r[i], 0, 0

    grid_spec = pltpu.PrefetchScalarGridSpec(
        num_scalar_prefetch=4,
        grid=(n_tiles, nk),
        in_specs=[
            pl.BlockSpec(memory_space=pl.ANY),
            pl.BlockSpec((1, MOE_TK, D_EXPERT), w_in_map),
            pl.BlockSpec((1, MOE_TK, D_EXPERT), w_in_map),
            pl.BlockSpec((1, D_EXPERT, D_MODEL), w_out_map),
        ],
        out_specs=pl.BlockSpec(memory_space=pl.ANY),
        scratch_shapes=[pltpu.VMEM((2, MOE_TM, D_MODEL), F32),
                        pltpu.VMEM((MOE_TM, D_MODEL), BF16),
                        pltpu.VMEM((MOE_TM, D_MODEL), F32),
                        pltpu.VMEM((MOE_TM, 2 * D_EXPERT), F32),
                        pltpu.VMEM((MOE_TK, 2 * D_EXPERT), BF16),
                        pltpu.SemaphoreType.DMA((2,)),
                        pltpu.SemaphoreType.DMA((1,))],
    )
    return pl.pallas_call(
        functools.partial(_moe_kernel, n_tok=n_tok),
        grid_spec=grid_spec,
        out_shape=jax.ShapeDtypeStruct((TOP_K * n_tok, D_MODEL), F32),
        compiler_params=_cparams(2, VMEM_LIMIT),
        name="moe_experts",
    )(te, nu, tc, inv, xn, w_gate, w_up, w_down)


def _combine_kernel(x_ref, y0_ref, y1_ref, w_ref, g_ref, xo_ref, *h_ref):
    w = w_ref[...]
    x = x_ref[...] + w[:, 0:1] * y0_ref[...] + w[:, 1:2] * y1_ref[...]
    xo_ref[...] = x
    if h_ref:
        h_ref[0][...] = (_rms(x) * g_ref[...]).astype(BF16)


def _combine_call(x, y, wts_t, g_next, *, tm=256):
    m, d = x.shape
    nb = m // tm
    with_norm = g_next is not None
    g = g_next.reshape(1, d) if with_norm else jnp.ones((1, d), F32)
    row = pl.BlockSpec((tm, d), lambda i: (i, 0))
    out_specs = [row, row] if with_norm else [row]
    out_shape = [jax.ShapeDtypeStruct((m, d), F32)]
    if with_norm:
        out_shape.append(jax.ShapeDtypeStruct((m, d), BF16))
    return pl.pallas_call(
        _combine_kernel,
        grid=(nb,),
        in_specs=[row, row,
                  pl.BlockSpec((tm, d), lambda i: (i + nb, 0)),
                  pl.BlockSpec((tm, TOP_K), lambda i: (i, 0)),
                  pl.BlockSpec((1, d), lambda i: (0, 0))],
        out_specs=out_specs,
        out_shape=out_shape,
        compiler_params=_cparams(1, VMEM_LIMIT),
        name="moe_combine",
    )(x, y, y, wts_t, g)


def _moe_layer(x, norm_g, w_group, b_group, w_router, b_router, w_gate, w_up, w_down, g_next):
    n_tok = x.shape[0]
    pad_rows = LANES - N_GROUPS - N_EXPERTS
    w_t = jnp.concatenate([w_group.T, w_router.T, jnp.zeros((pad_rows, D_MODEL), F32)], axis=0)
    b_col = jnp.concatenate([b_group, b_router, jnp.zeros((pad_rows,), F32)]).reshape(LANES, 1)
    xn, ids, wts = _route_call(x, norm_g.reshape(1, D_MODEL), w_t, b_col)
    pos, te, nu, tc = _sort_call(ids)
    inv = _invert_call(pos.reshape(-1), _moe_tiles(n_tok) * MOE_TM)
    y = _moe_call(te.reshape(-1), nu.reshape(-1)[:1], tc.reshape(-1), inv,
                  xn, w_gate.reshape(N_EXPERTS, D_MODEL, D_EXPERT),
                  w_up.reshape(N_EXPERTS, D_MODEL, D_EXPERT),
                  w_down.reshape(N_EXPERTS, D_EXPERT, D_MODEL))
    return _combine_call(x, y, wts.T, g_next)


def _rope_tables(pos_flat, dim):
    inv_freq = ROPE_THETA ** (-jnp.arange(0, dim, 2, dtype=F32) / dim)
    ang = pos_flat.astype(F32)[:, None] * inv_freq
    return jnp.cos(ang), jnp.sin(ang)


def kernel(x, mem, positions, mem_norm_g, w_mem_kv, mem_k_norm_g, l0_attn_norm_g, l0_w_in, l0_q_norm_g, l0_k_norm_g, l0_lambda_q1, l0_lambda_k1, l0_lambda_q2, l0_lambda_k2, l0_subln_g, l0_cross_q_norm_g, l0_w_out, l0_ffn_norm_g, l0_w_group, l0_b_group, l0_w_router, l0_b_router, l0_w_gate, l0_w_up, l0_w_down, l1_attn_norm_g, l1_w_in, l1_cq_norm_g, l1_ckv_norm_g, l1_w_uq, l1_w_ukv, l1_q_norm_g, l1_k_norm_g, l1_cross_q_norm_g, l1_w_out, l1_ffn_norm_g, l1_w_group, l1_b_group, l1_w_router, l1_b_router, l1_w_gate, l1_w_up, l1_w_down):
    batch, seq, d = x.shape
    n_tok = batch * seq
    xf = x.reshape(n_tok, d)
    pos_flat = positions.reshape(n_tok)
    posq = pos_flat.reshape(n_tok, 1)
    posk = pos_flat.reshape(1, n_tok)
    ones128 = jnp.ones((n_tok, LANES), F32)

    c64, s64 = _rope_tables(pos_flat, DIFF_HEAD_DIM)
    cos_full = jnp.concatenate([c64, c64], axis=1)
    sin_full = jnp.concatenate([-s64, s64], axis=1)
    c32, s32 = _rope_tables(pos_flat, MLA_ROPE_DIM)
    z32 = jnp.zeros_like(c32)
    cos_k = jnp.concatenate([c32, c32, z32, z32], axis=1)
    sin_k = jnp.concatenate([-s32, s32, z32, z32], axis=1)

    cross_scale = CROSS_HEAD_DIM ** -0.5

    memn = _norm_call(mem.reshape(batch * N_MEM, d), mem_norm_g)
    mem_gain = jnp.concatenate([jnp.tile(mem_k_norm_g, CROSS_HEADS), jnp.ones((CROSS_WIDTH,), F32)])
    k_tiles = CROSS_WIDTH // 512
    memkv = _mm_call(memn, w_mem_kv, mem_gain.reshape(1, -1), ones128[:batch * N_MEM],
                     ones128[:batch * N_MEM], ((0, k_tiles, "head256"), (k_tiles, 2 * k_tiles, "plain")),
                     tm=batch * N_MEM, tn=512, name="mem_kv")

    h0 = _norm_call(xf, l0_attn_norm_g)
    qk_tiles = SELF_WIDTH // 512
    gain0 = jnp.concatenate([
        jnp.tile(l0_q_norm_g, 2 * DIFF_HEADS) * (DIFF_HEAD_DIM ** -0.5),
        jnp.tile(l0_k_norm_g, 2 * DIFF_HEADS),
        jnp.ones((SELF_WIDTH,), F32),
        jnp.tile(l0_cross_q_norm_g, CROSS_HEADS) * cross_scale]).reshape(1, -1)
    proj0 = _mm_call(h0, l0_w_in, gain0, cos_full, sin_full,
                     ((0, 2 * qk_tiles, "head128_rope"), (2 * qk_tiles, 3 * qk_tiles, "plain"),
                      (3 * qk_tiles, 3 * qk_tiles + CROSS_WIDTH // 512, "head256")),
                     tm=1024, tn=512, name="l0_in_proj")
    lam_init = 0.8 - 0.6 * math.exp(-0.3 * 0)
    row = lambda v: v.reshape(1, -1)
    o_self = _attn_call(proj0, proj0, proj0, posq, posk,
                        (row(l0_lambda_q1), row(l0_lambda_k1), row(l0_lambda_q2),
                         row(l0_lambda_k2), row(l0_subln_g)),
                        batch=batch, seq=seq, heads=DIFF_HEADS, n_maps=2, dk=DIFF_HEAD_DIM,
                        dv=DIFF_V_DIM, qcol=0, kcol=DIFF_HEADS, vcol=2 * DIFF_HEADS,
                        lam_init=lam_init, name="diff_attn")
    o_cross = _cross_call(proj0, memkv, batch=batch, seq=seq, qcol=3 * SELF_WIDTH // CROSS_WIDTH)
    x1 = _outproj_call(o_self, o_cross, l0_w_out, xf)
    x2, h1 = _moe_layer(x1, l0_ffn_norm_g, l0_w_group, l0_b_group, l0_w_router, l0_b_router,
                        l0_w_gate, l0_w_up, l0_w_down, l1_attn_norm_g)

    a = MLA_Q_RANK
    b = a + MLA_KV_RANK
    c = b + MLA_ROPE_DIM
    wcat = jnp.concatenate([l1_w_in[:, :b], l1_w_in[:, c:], l1_w_in[:, b:c],
                            jnp.zeros((d, LANES - MLA_ROPE_DIM), F32)], axis=1).astype(BF16)
    cq, ckv, qm, kpe = _mla_in_call(
        h1, wcat, row(l1_cq_norm_g), row(l1_ckv_norm_g),
        row(jnp.tile(l1_cross_q_norm_g, CROSS_HEADS) * cross_scale))
    w_uq3 = l1_w_uq.reshape(MLA_Q_RANK, MLA_HEADS, MLA_QK_DIM)
    w_q_nope = w_uq3[:, :, :MLA_NOPE_DIM].reshape(MLA_Q_RANK, -1).astype(BF16)
    partner = (jnp.arange(MLA_ROPE_DIM) + MLA_ROPE_DIM // 2) % MLA_ROPE_DIM
    w_pe3 = w_uq3[:, :, MLA_NOPE_DIM:]
    pad3 = jnp.zeros_like(w_pe3)
    w_q_pe = jnp.concatenate([w_pe3, pad3], axis=2).reshape(MLA_Q_RANK, -1).astype(BF16)
    w_q_rot = jnp.concatenate([w_pe3[:, :, partner], pad3], axis=2).reshape(MLA_Q_RANK, -1).astype(BF16)
    q_scale = MLA_QK_DIM ** -0.5
    zeros64 = jnp.zeros((MLA_ROPE_DIM,), F32)
    g_q_pe = l1_q_norm_g[MLA_NOPE_DIM:] * q_scale
    q_pad = _mla_q_call(cq, w_q_nope, w_q_pe, w_q_rot,
                        row(l1_q_norm_g[:MLA_NOPE_DIM] * q_scale),
                        row(jnp.concatenate([g_q_pe, zeros64])),
                        row(jnp.concatenate([g_q_pe[partner], zeros64])), cos_k, sin_k)
    k_pad, v1 = _mla_kv_call(ckv, l1_w_ukv.astype(BF16), kpe,
                             row(l1_k_norm_g[:MLA_NOPE_DIM]),
                             row(jnp.concatenate([l1_k_norm_g[MLA_NOPE_DIM:], zeros64])),
                             cos_k, sin_k)
    o_self1 = _attn_call(q_pad, k_pad, v1, posq, posk, (),
                         batch=batch, seq=seq, heads=MLA_HEADS, n_maps=1, dk=MLA_PAD_DIM,
                         dv=MLA_V_DIM, qcol=0, kcol=0, vcol=0, lam_init=0.0, name="mla_attn")
    o_cross1 = _cross_call(qm, memkv, batch=batch, seq=seq, qcol=0)
    x3 = _outproj_call(o_self1, o_cross1, l1_w_out, x2)
    (x4,) = _moe_layer(x3, l1_ffn_norm_g, l1_w_group, l1_b_group, l1_w_router, l1_b_router,
                       l1_w_gate, l1_w_up, l1_w_down, None)
    return x4.reshape(batch, seq, d)
```

```python
import functools
import math

import jax
import jax.numpy as jnp
from jax import lax
from jax.experimental import pallas as pl
from jax.experimental.pallas import tpu as pltpu

F32 = jnp.float32
BF16 = jnp.bfloat16
I32 = jnp.int32

D_MODEL = 4096
N_MEM = 256
ROPE_THETA = 10000.0
NORM_EPS = 1e-6

SELF_WIDTH = 3 * D_MODEL // 4
CROSS_HEADS = 4
CROSS_HEAD_DIM = (D_MODEL // 4) // CROSS_HEADS
CROSS_WIDTH = CROSS_HEADS * CROSS_HEAD_DIM

DIFF_HEAD_DIM = 128
DIFF_V_DIM = 2 * DIFF_HEAD_DIM
DIFF_HEADS = SELF_WIDTH // DIFF_V_DIM

MLA_NOPE_DIM = 128
MLA_ROPE_DIM = 64
MLA_V_DIM = 128
MLA_HEADS = SELF_WIDTH // MLA_V_DIM
MLA_QK_DIM = MLA_NOPE_DIM + MLA_ROPE_DIM
MLA_Q_RANK = 3 * D_MODEL // 16
MLA_KV_RANK = D_MODEL // 8
MLA_PAD_DIM = 256

N_GROUPS = 8
EXPERTS_PER_GROUP = 8
N_EXPERTS = N_GROUPS * EXPERTS_PER_GROUP
D_EXPERT = 3 * D_MODEL // 32
TOP_K = 2

LANES = 128
MOE_TM = 320
MOE_TK = 1024
DMA_GROUP = 8
ROW_ALIGN = 8
DISPATCH_ZROWS = 64
DISPATCH_BATCH = 512
ATTN_TQ = 256
VMEM_LIMIT = 56 * 1024 * 1024


def _cparams(n_axes, vmem=None):
    return pltpu.CompilerParams(dimension_semantics=("arbitrary",) * n_axes,
                                vmem_limit_bytes=vmem)


def _rms(x):
    return x * lax.rsqrt(jnp.mean(x * x, axis=-1, keepdims=True) + NORM_EPS)


def _dot_nt(a, b):
    return lax.dot_general(a, b, (((1,), (1,)), ((), ())), preferred_element_type=F32)


def _norm_kernel(x_ref, g_ref, o_ref):
    o_ref[...] = (_rms(x_ref[...]) * g_ref[...]).astype(o_ref.dtype)


def _norm_call(x, g, tm=256):
    m, d = x.shape
    return pl.pallas_call(
        _norm_kernel,
        grid=(m // tm,),
        in_specs=[pl.BlockSpec((tm, d), lambda i: (i, 0)),
                  pl.BlockSpec((1, d), lambda i: (0, 0))],
        out_specs=pl.BlockSpec((tm, d), lambda i: (i, 0)),
        out_shape=jax.ShapeDtypeStruct((m, d), BF16),
        compiler_params=_cparams(1),
        name="rmsnorm",
    )(x, g.reshape(1, d))


def _epilogue(kind, acc, rows, gain_ref, cos_ref, sin_ref, o_ref):
    tn = acc.shape[1]
    if kind == "plain":
        o_ref[rows, :] = acc.astype(o_ref.dtype)
    elif kind == "head128_rope":
        c = cos_ref[rows, :]
        s = sin_ref[rows, :]
        for j in range(tn // 128):
            sl = slice(j * 128, (j + 1) * 128)
            y = _rms(acc[:, sl]) * gain_ref[:, sl]
            y = y * c + pltpu.roll(y, 64, axis=1) * s
            o_ref[rows, sl] = y.astype(o_ref.dtype)
    elif kind == "head256":
        for j in range(tn // 256):
            sl = slice(j * 256, (j + 1) * 256)
            o_ref[rows, sl] = (_rms(acc[:, sl]) * gain_ref[:, sl]).astype(o_ref.dtype)
    else:
        raise ValueError(kind)


MM_SUB = 256


def _mm_kernel(x_ref, w_ref, gain_ref, cos_ref, sin_ref, o_ref, wbf_ref, *, kinds):
    n = pl.program_id(0)

    @pl.when(pl.program_id(1) == 0)
    def _():
        wbf_ref[...] = w_ref[...].astype(BF16)

    def run(kind):
        for r in range(x_ref.shape[0] // MM_SUB):
            rows = slice(r * MM_SUB, (r + 1) * MM_SUB)
            acc = jnp.dot(x_ref[rows, :], wbf_ref[...], preferred_element_type=F32)
            _epilogue(kind, acc, rows, gain_ref, cos_ref, sin_ref, o_ref)

    if len(kinds) == 1:
        run(kinds[0][2])
    else:
        for lo, hi, kind in kinds:
            pl.when((n >= lo) & (n < hi))(functools.partial(run, kind))


def _mm_call(x, w, gain, cos, sin, kinds, *, tm, tn, name):
    m, k = x.shape
    n = w.shape[1]
    return pl.pallas_call(
        functools.partial(_mm_kernel, kinds=kinds),
        grid=(n // tn, m // tm),
        in_specs=[pl.BlockSpec((tm, k), lambda j, i: (i, 0)),
                  pl.BlockSpec((k, tn), lambda j, i: (0, j)),
                  pl.BlockSpec((1, tn), lambda j, i: (0, j)),
                  pl.BlockSpec((tm, LANES), lambda j, i: (i, 0)),
                  pl.BlockSpec((tm, LANES), lambda j, i: (i, 0))],
        out_specs=pl.BlockSpec((tm, tn), lambda j, i: (i, j)),
        out_shape=jax.ShapeDtypeStruct((m, n), BF16),
        scratch_shapes=[pltpu.VMEM((k, tn), BF16)],
        compiler_params=_cparams(2, VMEM_LIMIT),
        name=name,
    )(x, w, gain, cos, sin)


def _attn_kernel(*refs, n_maps, dk, tq, nq, lam_init):
    if n_maps == 2:
        (q_ref, k_ref, v_ref, posq_ref, posk_ref, lq1_ref, lk1_ref, lq2_ref, lk2_ref, subg_ref,
         o_ref) = refs
        lam = (jnp.exp(jnp.sum(lq1_ref[...] * lk1_ref[...], axis=-1, keepdims=True))
               - jnp.exp(jnp.sum(lq2_ref[...] * lk2_ref[...], axis=-1, keepdims=True))
               + lam_init)
    else:
        q_ref, k_ref, v_ref, posq_ref, posk_ref, o_ref = refs

    for i in range(nq):
        rows = slice(i * tq, (i + 1) * tq)
        n_past = i * tq
        mask = posk_ref[:, rows] <= posq_ref[rows, :]
        probs = []
        for mi in range(n_maps):
            cols = slice(mi * dk, (mi + 1) * dk)
            qm = q_ref[rows, cols]
            s_diag = jnp.where(mask, _dot_nt(qm, k_ref[rows, cols]), -jnp.inf)
            mx = jnp.max(s_diag, axis=-1, keepdims=True)
            if n_past:
                s_past = _dot_nt(qm, k_ref[0:n_past, cols])
                mx = jnp.maximum(mx, jnp.max(s_past, axis=-1, keepdims=True))
            p_diag = jnp.exp(s_diag - mx)
            den = jnp.sum(p_diag, axis=-1, keepdims=True)
            p_past = None
            if n_past:
                p_past = jnp.exp(s_past - mx)
                den = den + jnp.sum(p_past, axis=-1, keepdims=True)
            probs.append((p_diag, p_past, 1.0 / den))

        def pv(w_diag, w_past):
            o = jnp.dot(w_diag.astype(BF16), v_ref[rows, :], preferred_element_type=F32)
            if n_past:
                o = o + jnp.dot(w_past.astype(BF16), v_ref[0:n_past, :],
                                preferred_element_type=F32)
            return o

        if n_maps == 2:
            (d1, p1, r1), (d2, p2, r2) = probs
            c2 = lam * r2
            o = pv(d1 * r1 - d2 * c2, (p1 * r1 - p2 * c2) if n_past else None)
            o = _rms(o) * subg_ref[...] * (1.0 - lam_init)
        else:
            (d1, p1, r1), = probs
            o = pv(d1, p1) * r1
        o_ref[rows, :] = o.astype(o_ref.dtype)


def _attn_call(q, k, v, posq, posk, extra, *, batch, seq, heads, n_maps, dk, dv,
               qcol, kcol, vcol, lam_init, name):
    tq = ATTN_TQ
    in_specs = [
        pl.BlockSpec((seq, n_maps * dk), lambda b, h: (b, qcol + h)),
        pl.BlockSpec((seq, n_maps * dk), lambda b, h: (b, kcol + h)),
        pl.BlockSpec((seq, dv), lambda b, h: (b, vcol + h)),
        pl.BlockSpec((seq, 1), lambda b, h: (b, 0)),
        pl.BlockSpec((1, seq), lambda b, h: (0, b)),
    ] + [pl.BlockSpec(e.shape, lambda b, h: (0, 0)) for e in extra]
    return pl.pallas_call(
        functools.partial(_attn_kernel, n_maps=n_maps, dk=dk, tq=tq, nq=seq // tq,
                          lam_init=lam_init),
        grid=(batch, heads),
        in_specs=in_specs,
        out_specs=pl.BlockSpec((seq, dv), lambda b, h: (b, h)),
        out_shape=jax.ShapeDtypeStruct((batch * seq, heads * dv), BF16),
        compiler_params=_cparams(2, VMEM_LIMIT),
        name=name,
    )(q, k, v, posq, posk, *extra)


def _cross_kernel(q_ref, k_ref, v_ref, o_ref):
    for h in range(CROSS_HEADS):
        sl = slice(h * CROSS_HEAD_DIM, (h + 1) * CROSS_HEAD_DIM)
        s = _dot_nt(q_ref[:, sl], k_ref[:, sl])
        p = jnp.exp(s - jnp.max(s, axis=-1, keepdims=True))
        inv = 1.0 / jnp.sum(p, axis=-1, keepdims=True)
        o = jnp.dot(p.astype(BF16), v_ref[:, sl], preferred_element_type=F32)
        o_ref[:, sl] = (o * inv).astype(o_ref.dtype)


def _cross_call(q, memkv, *, batch, seq, qcol, tq=512):
    nq = seq // tq
    return pl.pallas_call(
        _cross_kernel,
        grid=(batch, nq),
        in_specs=[pl.BlockSpec((tq, CROSS_WIDTH), lambda b, i: (b * nq + i, qcol)),
                  pl.BlockSpec((N_MEM, CROSS_WIDTH), lambda b, i: (b, 0)),
                  pl.BlockSpec((N_MEM, CROSS_WIDTH), lambda b, i: (b, 1))],
        out_specs=pl.BlockSpec((tq, CROSS_WIDTH), lambda b, i: (b * nq + i, 0)),
        out_shape=jax.ShapeDtypeStruct((batch * seq, CROSS_WIDTH), BF16),
        compiler_params=_cparams(2),
        name="cross_attn",
    )(q, memkv, memkv)


def _outproj_kernel(os_ref, oc_ref, w_ref, x_ref, o_ref, wbf_ref):
    @pl.when(pl.program_id(1) == 0)
    def _():
        wbf_ref[...] = w_ref[...].astype(BF16)

    acc = jnp.dot(os_ref[...], wbf_ref[:SELF_WIDTH, :], preferred_element_type=F32)
    acc = acc + jnp.dot(oc_ref[...], wbf_ref[SELF_WIDTH:, :], preferred_element_type=F32)
    o_ref[...] = x_ref[...] + acc


def _outproj_call(o_self, o_cross, w_out, x, *, tm=1024, tn=512):
    m = x.shape[0]
    return pl.pallas_call(
        _outproj_kernel,
        grid=(D_MODEL // tn, m // tm),
        in_specs=[pl.BlockSpec((tm, SELF_WIDTH), lambda j, i: (i, 0)),
                  pl.BlockSpec((tm, CROSS_WIDTH), lambda j, i: (i, 0)),
                  pl.BlockSpec((D_MODEL, tn), lambda j, i: (0, j)),
                  pl.BlockSpec((tm, tn), lambda j, i: (i, j))],
        out_specs=pl.BlockSpec((tm, tn), lambda j, i: (i, j)),
        out_shape=jax.ShapeDtypeStruct((m, D_MODEL), F32),
        scratch_shapes=[pltpu.VMEM((D_MODEL, tn), BF16)],
        compiler_params=_cparams(2, VMEM_LIMIT),
        name="out_proj",
    )(o_self, o_cross, w_out, x)


def _mla_in_kernel(h_ref, w_ref, gcq_ref, gckv_ref, gqm_ref, cq_ref, ckv_ref, qm_ref, kpe_ref):
    acc = jnp.dot(h_ref[...], w_ref[...], preferred_element_type=F32)
    a = MLA_Q_RANK
    b = a + MLA_KV_RANK
    c = b + CROSS_WIDTH
    cq_ref[...] = (_rms(acc[:, :a]) * gcq_ref[...]).astype(cq_ref.dtype)
    ckv_ref[...] = (_rms(acc[:, a:b]) * gckv_ref[...]).astype(ckv_ref.dtype)
    for j in range(CROSS_HEADS):
        sl = slice(j * CROSS_HEAD_DIM, (j + 1) * CROSS_HEAD_DIM)
        qm_ref[:, sl] = (_rms(acc[:, b + j * CROSS_HEAD_DIM:b + (j + 1) * CROSS_HEAD_DIM])
                         * gqm_ref[:, sl]).astype(qm_ref.dtype)
    kpe_ref[...] = acc[:, c:]


def _mla_in_call(h, wcat, gcq, gckv, gqm, *, tm=256):
    m = h.shape[0]
    ncat = wcat.shape[1]
    row = lambda w: pl.BlockSpec((tm, w), lambda i: (i, 0))
    full = lambda r, w: pl.BlockSpec((r, w), lambda i: (0, 0))
    return pl.pallas_call(
        _mla_in_kernel,
        grid=(m // tm,),
        in_specs=[row(D_MODEL), full(D_MODEL, ncat), full(1, MLA_Q_RANK), full(1, MLA_KV_RANK),
                  full(1, CROSS_WIDTH)],
        out_specs=[row(MLA_Q_RANK), row(MLA_KV_RANK), row(CROSS_WIDTH), row(LANES)],
        out_shape=[jax.ShapeDtypeStruct((m, MLA_Q_RANK), BF16),
                   jax.ShapeDtypeStruct((m, MLA_KV_RANK), BF16),
                   jax.ShapeDtypeStruct((m, CROSS_WIDTH), BF16),
                   jax.ShapeDtypeStruct((m, LANES), F32)],
        compiler_params=_cparams(1, VMEM_LIMIT),
        name="mla_in_proj",
    )(h, wcat, gcq, gckv, gqm)


MLA_GROUP = 4


def _rope_pe(x, c, s):
    lane = lax.broadcasted_iota(I32, x.shape, 1)
    partner = jnp.where((lane & 63) < 32, pltpu.roll(x, 96, axis=1), pltpu.roll(x, 32, axis=1))
    return x * c + partner * s


def _mla_q_kernel(cq_ref, wn_ref, wp_ref, wr_ref, gn_ref, gp_ref, gr_ref, cos_ref, sin_ref, o_ref):
    x = cq_ref[...]
    an = jnp.dot(x, wn_ref[...], preferred_element_type=F32)
    ap = jnp.dot(x, wp_ref[...], preferred_element_type=F32)
    ar = jnp.dot(x, wr_ref[...], preferred_element_type=F32)
    gc = cos_ref[...] * gp_ref[...]
    gs = sin_ref[...] * gr_ref[...]
    for j in range(MLA_GROUP):
        sl = slice(j * 128, (j + 1) * 128)
        nj = an[:, sl]
        pj = ap[:, sl]
        ss = jnp.sum(nj * nj + pj * pj, axis=-1, keepdims=True)
        rs = lax.rsqrt(ss * (1.0 / MLA_QK_DIM) + NORM_EPS)
        o_ref[:, j * 256:j * 256 + 128] = (nj * rs * gn_ref[...]).astype(o_ref.dtype)
        o_ref[:, j * 256 + 128:(j + 1) * 256] = ((pj * gc + ar[:, sl] * gs) * rs).astype(o_ref.dtype)


def _mla_q_call(cq, w_nope, w_pe, w_rot, gn, gp, gr, cos, sin, *, tm=512):
    m = cq.shape[0]
    g = MLA_HEADS // MLA_GROUP
    wspec = pl.BlockSpec((MLA_Q_RANK, MLA_GROUP * 128), lambda i, j: (0, j))
    gspec = pl.BlockSpec((1, LANES), lambda i, j: (0, 0))
    tspec = pl.BlockSpec((tm, LANES), lambda i, j: (i, 0))
    return pl.pallas_call(
        _mla_q_kernel,
        grid=(m // tm, g),
        in_specs=[pl.BlockSpec((tm, MLA_Q_RANK), lambda i, j: (i, 0)),
                  wspec, wspec, wspec, gspec, gspec, gspec, tspec, tspec],
        out_specs=pl.BlockSpec((tm, MLA_GROUP * MLA_PAD_DIM), lambda i, j: (i, j)),
        out_shape=jax.ShapeDtypeStruct((m, MLA_HEADS * MLA_PAD_DIM), BF16),
        compiler_params=_cparams(2),
        name="mla_q_up",
    )(cq, w_nope, w_pe, w_rot, gn, gp, gr, cos, sin)


def _mla_kv_kernel(ckv_ref, w_ref, kpe_ref, gn_ref, gp_ref, cos_ref, sin_ref, k_ref, v_ref):
    acc = jnp.dot(ckv_ref[...], w_ref[...], preferred_element_type=F32)
    kpe = kpe_ref[...]
    ss_pe = jnp.sum(kpe * kpe, axis=-1, keepdims=True)
    pe = _rope_pe(kpe * gp_ref[...], cos_ref[...], sin_ref[...])
    for j in range(MLA_GROUP):
        kn = acc[:, j * 256:j * 256 + 128]
        ss = jnp.sum(kn * kn, axis=-1, keepdims=True) + ss_pe
        rs = lax.rsqrt(ss * (1.0 / MLA_QK_DIM) + NORM_EPS)
        k_ref[:, j * 256:j * 256 + 128] = (kn * rs * gn_ref[...]).astype(k_ref.dtype)
        k_ref[:, j * 256 + 128:(j + 1) * 256] = (pe * rs).astype(k_ref.dtype)
        v_ref[:, j * 128:(j + 1) * 128] = acc[:, j * 256 + 128:(j + 1) * 256].astype(v_ref.dtype)


def _mla_kv_call(ckv, w_ukv, kpe, gn, gp, cos, sin, *, tm=512):
    m = ckv.shape[0]
    g = MLA_HEADS // MLA_GROUP
    return pl.pallas_call(
        _mla_kv_kernel,
        grid=(m // tm, g),
        in_specs=[pl.BlockSpec((tm, MLA_KV_RANK), lambda i, j: (i, 0)),
                  pl.BlockSpec((MLA_KV_RANK, MLA_GROUP * 256), lambda i, j: (0, j)),
                  pl.BlockSpec((tm, LANES), lambda i, j: (i, 0)),
                  pl.BlockSpec((1, LANES), lambda i, j: (0, 0)),
                  pl.BlockSpec((1, LANES), lambda i, j: (0, 0)),
                  pl.BlockSpec((tm, LANES), lambda i, j: (i, 0)),
                  pl.BlockSpec((tm, LANES), lambda i, j: (i, 0))],
        out_specs=[pl.BlockSpec((tm, MLA_GROUP * MLA_PAD_DIM), lambda i, j: (i, j)),
                   pl.BlockSpec((tm, MLA_GROUP * MLA_V_DIM), lambda i, j: (i, j))],
        out_shape=[jax.ShapeDtypeStruct((m, MLA_HEADS * MLA_PAD_DIM), BF16),
                   jax.ShapeDtypeStruct((m, MLA_HEADS * MLA_V_DIM), BF16)],
        compiler_params=_cparams(2),
        name="mla_kv_up",
    )(ckv, w_ukv, kpe, gn, gp, cos, sin)


def _route_kernel(x_ref, g_ref, w_ref, b_ref, xn_ref, ids_ref, wts_ref):
    xn = _rms(x_ref[...]) * g_ref[...]
    xn_ref[...] = xn
    x_hi = xn.astype(BF16)
    x_lo = (xn - x_hi.astype(F32)).astype(BF16)
    w = w_ref[...]
    w_hi = w.astype(BF16)
    w_lo = (w - w_hi.astype(F32)).astype(BF16)
    lg = _dot_nt(w_hi, x_hi) + _dot_nt(w_hi, x_lo) + _dot_nt(w_lo, x_hi) + b_ref[...]
    tm = lg.shape[1]
    iota = lax.broadcasted_iota(I32, (EXPERTS_PER_GROUP, tm), 0)

    def first_argmax(v):
        mx = jnp.max(v, axis=0, keepdims=True)
        idx = jnp.min(jnp.where(v == mx, iota, EXPERTS_PER_GROUP), axis=0, keepdims=True)
        return mx, idx

    gl = lg[0:N_GROUPS, :]
    gmax, gsel = first_argmax(gl)
    g_gate = 1.0 / jnp.sum(jnp.exp(gl - gmax), axis=0, keepdims=True)
    el = jnp.zeros((EXPERTS_PER_GROUP, tm), F32)
    for g in range(N_GROUPS):
        lo = N_GROUPS + g * EXPERTS_PER_GROUP
        el = jnp.where(gsel == g, lg[lo:lo + EXPERTS_PER_GROUP, :], el)
    v1, i1 = first_argmax(el)
    v2, i2 = first_argmax(jnp.where(iota == i1, -jnp.inf, el))
    e = jnp.exp(v2 - v1)
    w1 = g_gate / (1.0 + e)
    w2 = g_gate * e / (1.0 + e)
    ids_ref[...] = jnp.concatenate([gsel * EXPERTS_PER_GROUP + i1,
                                    gsel * EXPERTS_PER_GROUP + i2], axis=0)
    wts_ref[...] = jnp.concatenate([w1, w2], axis=0)


def _route_call(x, g, w_t, b_col, *, tm=256):
    m = x.shape[0]
    return pl.pallas_call(
        _route_kernel,
        grid=(m // tm,),
        in_specs=[pl.BlockSpec((tm, D_MODEL), lambda i: (i, 0)),
                  pl.BlockSpec((1, D_MODEL), lambda i: (0, 0)),
                  pl.BlockSpec((LANES, D_MODEL), lambda i: (0, 0)),
                  pl.BlockSpec((LANES, 1), lambda i: (0, 0))],
        out_specs=[pl.BlockSpec((tm, D_MODEL), lambda i: (i, 0)),
                   pl.BlockSpec((TOP_K, tm), lambda i: (0, i)),
                   pl.BlockSpec((TOP_K, tm), lambda i: (0, i))],
        out_shape=[jax.ShapeDtypeStruct((m, D_MODEL), F32),
                   jax.ShapeDtypeStruct((TOP_K, m), I32),
                   jax.ShapeDtypeStruct((TOP_K, m), F32)],
        compiler_params=_cparams(1, VMEM_LIMIT),
        name="moe_route",
    )(x, g, w_t, b_col)


def _sort_kernel(ids_ref, pos_ref, te_ref, nu_ref, rs_ref, cnt_ref, start_ref, *, ts):
    n_tok = ids_ref.shape[1]
    nck = n_tok // ts
    iota_e = lax.broadcasted_iota(I32, (N_EXPERTS, ts), 0)

    def chunk(c):
        return pl.ds(pl.multiple_of(c * ts, ts), ts)

    def count_body(c, acc):
        for k in range(TOP_K):
            oh = jnp.where(iota_e == ids_ref[pl.ds(k, 1), chunk(c)], 1.0, 0.0)
            acc = acc + jnp.sum(oh, axis=1, keepdims=True)
        return acc

    counts = lax.fori_loop(0, nck, count_body, jnp.zeros((N_EXPERTS, 1), F32))
    er = lax.broadcasted_iota(I32, (N_EXPERTS, N_EXPERTS), 0)
    ec = lax.broadcasted_iota(I32, (N_EXPERTS, N_EXPERTS), 1)
    strict_lower = jnp.where(ec < er, 1.0, 0.0).astype(BF16)

    def excl_cumsum(v):
        vb = jnp.broadcast_to(v, (N_EXPERTS, LANES)).astype(BF16)
        return jnp.dot(strict_lower, vb, preferred_element_type=F32)

    padded = jnp.floor((counts + (ROW_ALIGN - 1)) * (1.0 / ROW_ALIGN)) * ROW_ALIGN
    hi = jnp.floor(padded * (1.0 / LANES))
    start = excl_cumsum(hi) * LANES + excl_cumsum(padded - hi * LANES)
    chunks = jnp.floor((counts + (MOE_TM - 0.5)) * (1.0 / MOE_TM))
    first = excl_cumsum(chunks)
    ends = first + chunks
    total = jnp.max(ends, axis=0, keepdims=True)
    chunk_i = lax.broadcasted_iota(I32, (N_EXPERTS, LANES), 1).astype(F32)
    chunk_c = jnp.minimum(chunk_i, total - 1.0)
    te = jnp.sum(jnp.where(ends <= chunk_c, 1.0, 0.0), axis=0, keepdims=True)
    mine = lax.broadcasted_iota(I32, (N_EXPERTS, LANES), 0).astype(F32) == te
    rs = jnp.sum(jnp.where(mine, start + (chunk_c - first) * MOE_TM, 0.0), axis=0, keepdims=True)
    te_ref[...] = te.astype(I32)
    nu_ref[...] = total.astype(I32)
    rs_ref[...] = rs.astype(I32)
    cnt_ref[...] = counts.astype(I32)
    start_ref[...] = start[:, 0:1].astype(I32)

    row_base = start[:, 0:1]
    ur = lax.broadcasted_iota(I32, (ts, ts), 0)
    uc = lax.broadcasted_iota(I32, (ts, ts), 1)
    upper = jnp.where(ur <= uc, 1.0, 0.0).astype(BF16)

    def pos_body(k):
        def body(c, carry):
            hit = iota_e == ids_ref[pl.ds(k, 1), chunk(c)]
            incl = jnp.dot(jnp.where(hit, 1.0, 0.0).astype(BF16), upper,
                           preferred_element_type=F32)
            val = row_base + carry + incl - 1.0
            p = jnp.sum(jnp.where(hit, val, 0.0), axis=0, keepdims=True)
            pos_ref[pl.ds(k, 1), chunk(c)] = p.astype(I32)
            return carry + incl[:, ts - 1:ts]
        return body

    carry = jnp.zeros((N_EXPERTS, 1), F32)
    for k in range(TOP_K):
        carry = lax.fori_loop(0, nck, pos_body(k), carry)


def _sort_call(ids, *, ts=512):
    n_tok = ids.shape[1]
    ts = min(ts, n_tok)
    return pl.pallas_call(
        functools.partial(_sort_kernel, ts=ts),
        out_shape=[jax.ShapeDtypeStruct((TOP_K, n_tok), I32),
                   jax.ShapeDtypeStruct((1, LANES), I32),
                   jax.ShapeDtypeStruct((1, LANES), I32),
                   jax.ShapeDtypeStruct((1, LANES), I32),
                   jax.ShapeDtypeStruct((N_EXPERTS, 1), I32),
                   jax.ShapeDtypeStruct((N_EXPERTS, 1), I32)],
        name="moe_sort",
    )(ids)


def _sorted_rows(n_tok):
    return TOP_K * n_tok + N_EXPERTS * ROW_ALIGN + MOE_TM


def _moe_chunks(n_tok):
    return -(-(TOP_K * n_tok) // MOE_TM) + N_EXPERTS


def _dispatch_kernel(pos_ref, cnt_ref, start_ref, xn_hbm, xs_hbm, zbuf, zsem, rsem, *, n_tok):
    n_slots = TOP_K * n_tok
    n_rows = xs_hbm.shape[0]
    zbuf[...] = jnp.zeros(zbuf.shape, zbuf.dtype)

    def tail_copy(j):
        return pltpu.make_async_copy(
            zbuf, xs_hbm.at[pl.ds(n_slots + j * DISPATCH_ZROWS, DISPATCH_ZROWS)], zsem.at[0])

    n_tail = (n_rows - n_slots) // DISPATCH_ZROWS
    for j in range(n_tail):
        tail_copy(j).start()
    for j in range(n_tail):
        tail_copy(j).wait()

    def zero_row(dst):
        return pltpu.make_async_copy(zbuf.at[pl.ds(0, 1)], xs_hbm.at[pl.ds(dst, 1)], zsem.at[0])

    def per_expert(e, c):
        n = cnt_ref[e]
        first = start_ref[e] + n
        n_pad = (-n) & (ROW_ALIGN - 1)

        def put(r, c2):
            zero_row(first + r).start()
            return c2
        lax.fori_loop(0, n_pad, put, 0)

        def done(r, c2):
            zero_row(0).wait()
            return c2
        lax.fori_loop(0, n_pad, done, 0)
        return c

    lax.fori_loop(0, N_EXPERTS, per_expert, 0)

    def batch_wait(sl):
        pltpu.make_async_copy(xn_hbm.at[pl.ds(0, DISPATCH_BATCH)],
                              xs_hbm.at[pl.ds(0, DISPATCH_BATCH)], rsem.at[sl]).wait()

    def batch(b, c):
        sl = b % 2

        @pl.when(b >= 2)
        def _():
            batch_wait(sl)

        def group(g, c2):
            for u in range(DMA_GROUP):
                s = b * DISPATCH_BATCH + g * DMA_GROUP + u
                tok = jnp.where(s >= n_tok, s - n_tok, s)
                pltpu.make_async_copy(xn_hbm.at[pl.ds(tok, 1)], xs_hbm.at[pl.ds(pos_ref[s], 1)],
                                      rsem.at[sl]).start()
            return c2
        lax.fori_loop(0, DISPATCH_BATCH // DMA_GROUP, group, 0)
        return c

    n_batches = n_slots // DISPATCH_BATCH
    lax.fori_loop(0, n_batches, batch, 0)
    batch_wait(n_batches % 2)
    batch_wait((n_batches + 1) % 2)


def _dispatch_call(pos_flat, cnt, start, xn):
    n_tok = xn.shape[0]
    assert (TOP_K * n_tok) % DISPATCH_BATCH == 0 and TOP_K * n_tok >= 2 * DISPATCH_BATCH
    assert (N_EXPERTS * ROW_ALIGN + MOE_TM) % DISPATCH_ZROWS == 0
    smem = pl.BlockSpec(memory_space=pltpu.SMEM)
    return pl.pallas_call(
        functools.partial(_dispatch_kernel, n_tok=n_tok),
        in_specs=[smem, smem, smem, pl.BlockSpec(memory_space=pl.ANY)],
        out_specs=pl.BlockSpec(memory_space=pl.ANY),
        out_shape=jax.ShapeDtypeStruct((_sorted_rows(n_tok), D_MODEL), F32),
        scratch_shapes=[pltpu.VMEM((DISPATCH_ZROWS, D_MODEL), F32),
                        pltpu.SemaphoreType.DMA((1,)),
                        pltpu.SemaphoreType.DMA((2,))],
        name="moe_dispatch",
    )(pos_flat, cnt, start, xn)


def _moe_kernel(te_ref, nu_ref, rs_ref, xs_hbm, wg_ref, wu_ref, wd_ref, y_hbm,
                xbuf, xb, acc, gu, wgu, gsem, ssem, *, n_slots):
    i = pl.program_id(0)
    k = pl.program_id(1)
    nk = pl.num_programs(1)
    n_used = nu_ref[0]
    valid = i < n_used
    slot = i % 2

    def x_copy(c, sl):
        rows = pl.ds(pl.multiple_of(rs_ref[c], ROW_ALIGN), MOE_TM)
        return pltpu.make_async_copy(xs_hbm.at[rows], xbuf.at[sl], gsem.at[sl])

    def y_copy(c):
        rows = pl.ds(pl.multiple_of(rs_ref[c], ROW_ALIGN), MOE_TM)
        return pltpu.make_async_copy(acc, y_hbm.at[rows], ssem.at[0])

    @pl.when((i == 0) & (k == 0))
    def _():
        x_copy(0, 0).start()
        acc[...] = jnp.zeros(acc.shape, acc.dtype)
        n_rows = y_hbm.shape[0]
        tails = [(off, min(MOE_TM, n_rows - off)) for off in range(n_slots, n_rows, MOE_TM)]
        for off, size in tails:
            pltpu.make_async_copy(acc.at[pl.ds(0, size)], y_hbm.at[pl.ds(off, size)],
                                  ssem.at[0]).start()
        for off, size in tails:
            pltpu.make_async_copy(acc.at[pl.ds(0, size)], y_hbm.at[pl.ds(off, size)],
                                  ssem.at[0]).wait()

    @pl.when(valid & (k == 0))
    def _():
        x_copy(i, slot).wait()

        @pl.when(i + 1 < n_used)
        def _():
            x_copy(i + 1, 1 - slot).start()

        xb[...] = xbuf[slot].astype(BF16)

    @pl.when(valid)
    def _():
        wgu[:, :D_EXPERT] = wg_ref[0].astype(BF16)
        wgu[:, D_EXPERT:] = wu_ref[0].astype(BF16)
        xs = xb[:, pl.ds(pl.multiple_of(k * MOE_TK, MOE_TK), MOE_TK)]
        part = jnp.dot(xs, wgu[...], preferred_element_type=F32)

        @pl.when(k == 0)
        def _():
            gu[...] = part

        @pl.when(k > 0)
        def _():
            gu[...] += part

        @pl.when(k == nk - 1)
        def _():
            a = gu[:, :D_EXPERT]
            hid = (a * jax.nn.sigmoid(a)) * gu[:, D_EXPERT:]
            out = jnp.dot(hid.astype(BF16), wd_ref[0].astype(BF16), preferred_element_type=F32)

            @pl.when(i > 0)
            def _():
                y_copy(i - 1).wait()
            acc[...] = out
            y_copy(i).start()

            @pl.when(i == n_used - 1)
            def _():
                y_copy(i).wait()


def _moe_call(te, nu, rs, xs, w_gate, w_up, w_down, *, n_tok):
    nk = D_MODEL // MOE_TK
    last_k = nk - 1

    def w_in_map(i, k, te_r, nu_r, rs_r):
        return te_r[i], jnp.where(i < nu_r[0], k, last_k), 0

    def w_out_map(i, k, te_r, nu_r, rs_r):
        return te_r[i], 0, 0

    grid_spec = pltpu.PrefetchScalarGridSpec(
        num_scalar_prefetch=3,
        grid=(_moe_chunks(n_tok), nk),
        in_specs=[
            pl.BlockSpec(memory_space=pl.ANY),
            pl.BlockSpec((1, MOE_TK, D_EXPERT), w_in_map),
            pl.BlockSpec((1, MOE_TK, D_EXPERT), w_in_map),
            pl.BlockSpec((1, D_EXPERT, D_MODEL), w_out_map),
        ],
        out_specs=pl.BlockSpec(memory_space=pl.ANY),
        scratch_shapes=[pltpu.VMEM((2, MOE_TM, D_MODEL), F32),
                        pltpu.VMEM((MOE_TM, D_MODEL), BF16),
                        pltpu.VMEM((MOE_TM, D_MODEL), F32),
                        pltpu.VMEM((MOE_TM, 2 * D_EXPERT), F32),
                        pltpu.VMEM((MOE_TK, 2 * D_EXPERT), BF16),
                        pltpu.SemaphoreType.DMA((2,)),
                        pltpu.SemaphoreType.DMA((1,))],
    )
    return pl.pallas_call(
        functools.partial(_moe_kernel, n_slots=TOP_K * n_tok),
        grid_spec=grid_spec,
        out_shape=jax.ShapeDtypeStruct(xs.shape, F32),
        compiler_params=_cparams(2, VMEM_LIMIT),
        name="moe_experts",
    )(te, nu, rs, xs, w_gate, w_up, w_down)


def _combine_kernel(pos_ref, x_ref, w_ref, g_ref, y_hbm, xo_ref, *rest, n_tok):
    *h_ref, ybuf, sem = rest
    i = pl.program_id(0)
    nb = pl.num_programs(0)
    tm = x_ref.shape[0]
    slot = i % 2

    def fetch(tile, sl):
        def group(g, c):
            for u in range(DMA_GROUP):
                r = g * DMA_GROUP + u
                for k in range(TOP_K):
                    src = pos_ref[k * n_tok + tile * tm + r]
                    pltpu.make_async_copy(y_hbm.at[pl.ds(src, 1)], ybuf.at[sl, k, pl.ds(r, 1)],
                                          sem.at[sl]).start()
            return c
        lax.fori_loop(0, tm // DMA_GROUP, group, 0)

    @pl.when(i == 0)
    def _():
        fetch(0, 0)

    @pl.when(i + 1 < nb)
    def _():
        fetch(i + 1, 1 - slot)

    for k in range(TOP_K):
        pltpu.make_async_copy(y_hbm.at[pl.ds(0, tm)], ybuf.at[slot, k], sem.at[slot]).wait()
    w = w_ref[...]
    x = x_ref[...] + w[:, 0:1] * ybuf[slot, 0] + w[:, 1:2] * ybuf[slot, 1]
    xo_ref[...] = x
    if h_ref:
        h_ref[0][...] = (_rms(x) * g_ref[...]).astype(BF16)


def _combine_call(pos_flat, x, y, wts_t, g_next, *, tm=256):
    m, d = x.shape
    with_norm = g_next is not None
    g = g_next.reshape(1, d) if with_norm else jnp.ones((1, d), F32)
    row = pl.BlockSpec((tm, d), lambda i, pos_r: (i, 0))
    out_specs = [row, row] if with_norm else [row]
    out_shape = [jax.ShapeDtypeStruct((m, d), F32)]
    if with_norm:
        out_shape.append(jax.ShapeDtypeStruct((m, d), BF16))
    grid_spec = pltpu.PrefetchScalarGridSpec(
        num_scalar_prefetch=1,
        grid=(m // tm,),
        in_specs=[row,
                  pl.BlockSpec((tm, TOP_K), lambda i, pos_r: (i, 0)),
                  pl.BlockSpec((1, d), lambda i, pos_r: (0, 0)),
                  pl.BlockSpec(memory_space=pl.ANY)],
        out_specs=out_specs,
        scratch_shapes=[pltpu.VMEM((2, TOP_K, tm, d), F32),
                        pltpu.SemaphoreType.DMA((2,))],
    )
    return pl.pallas_call(
        functools.partial(_combine_kernel, n_tok=m),
        grid_spec=grid_spec,
        out_shape=out_shape,
        compiler_params=_cparams(1, VMEM_LIMIT),
        name="moe_combine",
    )(pos_flat, x, wts_t, g, y)


def _moe_layer(x, norm_g, w_group, b_group, w_router, b_router, w_gate, w_up, w_down, g_next):
    n_tok = x.shape[0]
    pad_rows = LANES - N_GROUPS - N_EXPERTS
    w_t = jnp.concatenate([w_group.T, w_router.T, jnp.zeros((pad_rows, D_MODEL), F32)], axis=0)
    b_col = jnp.concatenate([b_group, b_router, jnp.zeros((pad_rows,), F32)]).reshape(LANES, 1)
    xn, ids, wts = _route_call(x, norm_g.reshape(1, D_MODEL), w_t, b_col)
    pos, te, nu, rs, cnt, start = _sort_call(ids)
    pos_flat = pos.reshape(-1)
    xs = _dispatch_call(pos_flat, cnt.reshape(-1), start.reshape(-1), xn)
    y = _moe_call(te.reshape(-1), nu.reshape(-1)[:1], rs.reshape(-1), xs,
                  w_gate.reshape(N_EXPERTS, D_MODEL, D_EXPERT),
                  w_up.reshape(N_EXPERTS, D_MODEL, D_EXPERT),
                  w_down.reshape(N_EXPERTS, D_EXPERT, D_MODEL), n_tok=n_tok)
    return _combine_call(pos_flat, x, y, wts.T, g_next)


def _rope_tables(pos_flat, dim):
    inv_freq = ROPE_THETA ** (-jnp.arange(0, dim, 2, dtype=F32) / dim)
    ang = pos_flat.astype(F32)[:, None] * inv_freq
    return jnp.cos(ang), jnp.sin(ang)


def kernel(x, mem, positions, mem_norm_g, w_mem_kv, mem_k_norm_g, l0_attn_norm_g, l0_w_in, l0_q_norm_g, l0_k_norm_g, l0_lambda_q1, l0_lambda_k1, l0_lambda_q2, l0_lambda_k2, l0_subln_g, l0_cross_q_norm_g, l0_w_out, l0_ffn_norm_g, l0_w_group, l0_b_group, l0_w_router, l0_b_router, l0_w_gate, l0_w_up, l0_w_down, l1_attn_norm_g, l1_w_in, l1_cq_norm_g, l1_ckv_norm_g, l1_w_uq, l1_w_ukv, l1_q_norm_g, l1_k_norm_g, l1_cross_q_norm_g, l1_w_out, l1_ffn_norm_g, l1_w_group, l1_b_group, l1_w_router, l1_b_router, l1_w_gate, l1_w_up, l1_w_down):
    batch, seq, d = x.shape
    n_tok = batch * seq
    xf = x.reshape(n_tok, d)
    pos_flat = positions.reshape(n_tok)
    posq = pos_flat.reshape(n_tok, 1)
    posk = pos_flat.reshape(1, n_tok)
    ones128 = jnp.ones((n_tok, LANES), F32)

    c64, s64 = _rope_tables(pos_flat, DIFF_HEAD_DIM)
    cos_full = jnp.concatenate([c64, c64], axis=1)
    sin_full = jnp.concatenate([-s64, s64], axis=1)
    c32, s32 = _rope_tables(pos_flat, MLA_ROPE_DIM)
    z32 = jnp.zeros_like(c32)
    cos_k = jnp.concatenate([c32, c32, z32, z32], axis=1)
    sin_k = jnp.concatenate([-s32, s32, z32, z32], axis=1)

    cross_scale = CROSS_HEAD_DIM ** -0.5

    memn = _norm_call(mem.reshape(batch * N_MEM, d), mem_norm_g)
    mem_gain = jnp.concatenate([jnp.tile(mem_k_norm_g, CROSS_HEADS), jnp.ones((CROSS_WIDTH,), F32)])
    k_tiles = CROSS_WIDTH // 512
    memkv = _mm_call(memn, w_mem_kv, mem_gain.reshape(1, -1), ones128[:batch * N_MEM],
                     ones128[:batch * N_MEM], ((0, k_tiles, "head256"), (k_tiles, 2 * k_tiles, "plain")),
                     tm=batch * N_MEM, tn=512, name="mem_kv")

    h0 = _norm_call(xf, l0_attn_norm_g)
    qk_tiles = SELF_WIDTH // 512
    gain0 = jnp.concatenate([
        jnp.tile(l0_q_norm_g, 2 * DIFF_HEADS) * (DIFF_HEAD_DIM ** -0.5),
        jnp.tile(l0_k_norm_g, 2 * DIFF_HEADS),
        jnp.ones((SELF_WIDTH,), F32),
        jnp.tile(l0_cross_q_norm_g, CROSS_HEADS) * cross_scale]).reshape(1, -1)
    proj0 = _mm_call(h0, l0_w_in, gain0, cos_full, sin_full,
                     ((0, 2 * qk_tiles, "head128_rope"), (2 * qk_tiles, 3 * qk_tiles, "plain"),
                      (3 * qk_tiles, 3 * qk_tiles + CROSS_WIDTH // 512, "head256")),
                     tm=1024, tn=512, name="l0_in_proj")
    lam_init = 0.8 - 0.6 * math.exp(-0.3 * 0)
    row = lambda v: v.reshape(1, -1)
    o_self = _attn_call(proj0, proj0, proj0, posq, posk,
                        (row(l0_lambda_q1), row(l0_lambda_k1), row(l0_lambda_q2),
                         row(l0_lambda_k2), row(l0_subln_g)),
                        batch=batch, seq=seq, heads=DIFF_HEADS, n_maps=2, dk=DIFF_HEAD_DIM,
                        dv=DIFF_V_DIM, qcol=0, kcol=DIFF_HEADS, vcol=2 * DIFF_HEADS,
                        lam_init=lam_init, name="diff_attn")
    o_cross = _cross_call(proj0, memkv, batch=batch, seq=seq, qcol=3 * SELF_WIDTH // CROSS_WIDTH)
    x1 = _outproj_call(o_self, o_cross, l0_w_out, xf)
    x2, h1 = _moe_layer(x1, l0_ffn_norm_g, l0_w_group, l0_b_group, l0_w_router, l0_b_router,
                        l0_w_gate, l0_w_up, l0_w_down, l1_attn_norm_g)

    a = MLA_Q_RANK
    b = a + MLA_KV_RANK
    c = b + MLA_ROPE_DIM
    wcat = jnp.concatenate([l1_w_in[:, :b], l1_w_in[:, c:], l1_w_in[:, b:c],
                            jnp.zeros((d, LANES - MLA_ROPE_DIM), F32)], axis=1).astype(BF16)
    cq, ckv, qm, kpe = _mla_in_call(
        h1, wcat, row(l1_cq_norm_g), row(l1_ckv_norm_g),
        row(jnp.tile(l1_cross_q_norm_g, CROSS_HEADS) * cross_scale))
    w_uq3 = l1_w_uq.reshape(MLA_Q_RANK, MLA_HEADS, MLA_QK_DIM)
    w_q_nope = w_uq3[:, :, :MLA_NOPE_DIM].reshape(MLA_Q_RANK, -1).astype(BF16)
    partner = (jnp.arange(MLA_ROPE_DIM) + MLA_ROPE_DIM // 2) % MLA_ROPE_DIM
    w_pe3 = w_uq3[:, :, MLA_NOPE_DIM:]
    pad3 = jnp.zeros_like(w_pe3)
    w_q_pe = jnp.concatenate([w_pe3, pad3], axis=2).reshape(MLA_Q_RANK, -1).astype(BF16)
    w_q_rot = jnp.concatenate([w_pe3[:, :, partner], pad3], axis=2).reshape(MLA_Q_RANK, -1).astype(BF16)
    q_scale = MLA_QK_DIM ** -0.5
    zeros64 = jnp.zeros((MLA_ROPE_DIM,), F32)
    g_q_pe = l1_q_norm_g[MLA_NOPE_DIM:] * q_scale
    q_pad = _mla_q_call(cq, w_q_nope, w_q_pe, w_q_rot,
                        row(l1_q_norm_g[:MLA_NOPE_DIM] * q_scale),
                        row(jnp.concatenate([g_q_pe, zeros64])),
                        row(jnp.concatenate([g_q_pe[partner], zeros64])), cos_k, sin_k)
    k_pad, v1 = _mla_kv_call(ckv, l1_w_ukv.astype(BF16), kpe,
                             row(l1_k_norm_g[:MLA_NOPE_DIM]),
                             row(jnp.concatenate([l1_k_norm_g[MLA_NOPE_DIM:], zeros64])),
                             cos_k, sin_k)
    o_self1 = _attn_call(q_pad, k_pad, v1, posq, posk, (),
                         batch=batch, seq=seq, heads=MLA_HEADS, n_maps=1, dk=MLA_PAD_DIM,
                         dv=MLA_V_DIM, qcol=0, kcol=0, vcol=0, lam_init=0.0, name="mla_attn")
    o_cross1 = _cross_call(qm, memkv, batch=batch, seq=seq, qcol=0)
    x3 = _outproj_call(o_self1, o_cross1, l1_w_out, x2)
    (x4,) = _moe_layer(x3, l1_ffn_norm_g, l1_w_group, l1_b_group, l1_w_router, l1_b_router,
                       l1_w_gate, l1_w_up, l1_w_down, None)
    return x4.reshape(batch, seq, d)
```

```python
import functools
import math

import jax
import jax.numpy as jnp
from jax import lax
from jax.experimental import pallas as pl
from jax.experimental.pallas import tpu as pltpu

F32 = jnp.float32
BF16 = jnp.bfloat16
I32 = jnp.int32

D_MODEL = 4096
N_MEM = 256
ROPE_THETA = 10000.0
NORM_EPS = 1e-6

SELF_WIDTH = 3 * D_MODEL // 4
CROSS_HEADS = 4
CROSS_HEAD_DIM = (D_MODEL // 4) // CROSS_HEADS
CROSS_WIDTH = CROSS_HEADS * CROSS_HEAD_DIM

DIFF_HEAD_DIM = 128
DIFF_V_DIM = 2 * DIFF_HEAD_DIM
DIFF_HEADS = SELF_WIDTH // DIFF_V_DIM

MLA_NOPE_DIM = 128
MLA_ROPE_DIM = 64
MLA_V_DIM = 128
MLA_HEADS = SELF_WIDTH // MLA_V_DIM
MLA_QK_DIM = MLA_NOPE_DIM + MLA_ROPE_DIM
MLA_Q_RANK = 3 * D_MODEL // 16
MLA_KV_RANK = D_MODEL // 8
MLA_PAD_DIM = 256

N_GROUPS = 8
EXPERTS_PER_GROUP = 8
N_EXPERTS = N_GROUPS * EXPERTS_PER_GROUP
D_EXPERT = 3 * D_MODEL // 32
TOP_K = 2

LANES = 128
MOE_TM = 320
MOE_TK = 2048
DMA_GROUP = 8
ROW_ALIGN = 8
DISPATCH_ZROWS = 64
ATTN_TQ = 256
VMEM_LIMIT = 56 * 1024 * 1024


def _cparams(n_axes, vmem=None):
    return pltpu.CompilerParams(dimension_semantics=("arbitrary",) * n_axes,
                                vmem_limit_bytes=vmem)


def _rms(x):
    return x * lax.rsqrt(jnp.mean(x * x, axis=-1, keepdims=True) + NORM_EPS)


def _dot_nt(a, b):
    return lax.dot_general(a, b, (((1,), (1,)), ((), ())), preferred_element_type=F32)


def _norm_kernel(x_ref, g_ref, o_ref):
    o_ref[...] = (_rms(x_ref[...]) * g_ref[...]).astype(o_ref.dtype)


def _norm_call(x, g, tm=256):
    m, d = x.shape
    return pl.pallas_call(
        _norm_kernel,
        grid=(m // tm,),
        in_specs=[pl.BlockSpec((tm, d), lambda i: (i, 0)),
                  pl.BlockSpec((1, d), lambda i: (0, 0))],
        out_specs=pl.BlockSpec((tm, d), lambda i: (i, 0)),
        out_shape=jax.ShapeDtypeStruct((m, d), BF16),
        compiler_params=_cparams(1),
        name="rmsnorm",
    )(x, g.reshape(1, d))


def _epilogue(kind, acc, rows, gain_ref, cos_ref, sin_ref, o_ref):
    tn = acc.shape[1]
    if kind == "plain":
        o_ref[rows, :] = acc.astype(o_ref.dtype)
    elif kind == "head128_rope":
        c = cos_ref[rows, :]
        s = sin_ref[rows, :]
        for j in range(tn // 128):
            sl = slice(j * 128, (j + 1) * 128)
            y = _rms(acc[:, sl]) * gain_ref[:, sl]
            y = y * c + pltpu.roll(y, 64, axis=1) * s
            o_ref[rows, sl] = y.astype(o_ref.dtype)
    elif kind == "head256":
        for j in range(tn // 256):
            sl = slice(j * 256, (j + 1) * 256)
            o_ref[rows, sl] = (_rms(acc[:, sl]) * gain_ref[:, sl]).astype(o_ref.dtype)
    else:
        raise ValueError(kind)


MM_SUB = 256


def _mm_kernel(x_ref, w_ref, gain_ref, cos_ref, sin_ref, o_ref, wbf_ref, *, kinds):
    n = pl.program_id(0)

    @pl.when(pl.program_id(1) == 0)
    def _():
        wbf_ref[...] = w_ref[...].astype(BF16)

    def run(kind):
        for r in range(x_ref.shape[0] // MM_SUB):
            rows = slice(r * MM_SUB, (r + 1) * MM_SUB)
            acc = jnp.dot(x_ref[rows, :], wbf_ref[...], preferred_element_type=F32)
            _epilogue(kind, acc, rows, gain_ref, cos_ref, sin_ref, o_ref)

    if len(kinds) == 1:
        run(kinds[0][2])
    else:
        for lo, hi, kind in kinds:
            pl.when((n >= lo) & (n < hi))(functools.partial(run, kind))


def _mm_call(x, w, gain, cos, sin, kinds, *, tm, tn, name):
    m, k = x.shape
    n = w.shape[1]
    return pl.pallas_call(
        functools.partial(_mm_kernel, kinds=kinds),
        grid=(n // tn, m // tm),
        in_specs=[pl.BlockSpec((tm, k), lambda j, i: (i, 0)),
                  pl.BlockSpec((k, tn), lambda j, i: (0, j)),
                  pl.BlockSpec((1, tn), lambda j, i: (0, j)),
                  pl.BlockSpec((tm, LANES), lambda j, i: (i, 0)),
                  pl.BlockSpec((tm, LANES), lambda j, i: (i, 0))],
        out_specs=pl.BlockSpec((tm, tn), lambda j, i: (i, j)),
        out_shape=jax.ShapeDtypeStruct((m, n), BF16),
        scratch_shapes=[pltpu.VMEM((k, tn), BF16)],
        compiler_params=_cparams(2, VMEM_LIMIT),
        name=name,
    )(x, w, gain, cos, sin)


def _attn_kernel(*refs, n_maps, dk, tq, nq, lam_init):
    if n_maps == 2:
        (q_ref, k_ref, v_ref, posq_ref, posk_ref, lq1_ref, lk1_ref, lq2_ref, lk2_ref, subg_ref,
         o_ref) = refs
        lam = (jnp.exp(jnp.sum(lq1_ref[...] * lk1_ref[...], axis=-1, keepdims=True))
               - jnp.exp(jnp.sum(lq2_ref[...] * lk2_ref[...], axis=-1, keepdims=True))
               + lam_init)
    else:
        q_ref, k_ref, v_ref, posq_ref, posk_ref, o_ref = refs

    for i in range(nq):
        rows = slice(i * tq, (i + 1) * tq)
        n_past = i * tq
        mask = posk_ref[:, rows] <= posq_ref[rows, :]
        probs = []
        for mi in range(n_maps):
            cols = slice(mi * dk, (mi + 1) * dk)
            qm = q_ref[rows, cols]
            s_diag = jnp.where(mask, _dot_nt(qm, k_ref[rows, cols]), -jnp.inf)
            mx = jnp.max(s_diag, axis=-1, keepdims=True)
            if n_past:
                s_past = _dot_nt(qm, k_ref[0:n_past, cols])
                mx = jnp.maximum(mx, jnp.max(s_past, axis=-1, keepdims=True))
            p_diag = jnp.exp(s_diag - mx)
            den = jnp.sum(p_diag, axis=-1, keepdims=True)
            p_past = None
            if n_past:
                p_past = jnp.exp(s_past - mx)
                den = den + jnp.sum(p_past, axis=-1, keepdims=True)
            probs.append((p_diag, p_past, 1.0 / den))

        def pv(w_diag, w_past):
            o = jnp.dot(w_diag.astype(BF16), v_ref[rows, :], preferred_element_type=F32)
            if n_past:
                o = o + jnp.dot(w_past.astype(BF16), v_ref[0:n_past, :],
                                preferred_element_type=F32)
            return o

        if n_maps == 2:
            (d1, p1, r1), (d2, p2, r2) = probs
            c2 = lam * r2
            o = pv(d1 * r1 - d2 * c2, (p1 * r1 - p2 * c2) if n_past else None)
            o = _rms(o) * subg_ref[...] * (1.0 - lam_init)
        else:
            (d1, p1, r1), = probs
            o = pv(d1, p1) * r1
        o_ref[rows, :] = o.astype(o_ref.dtype)


def _attn_call(q, k, v, posq, posk, extra, *, batch, seq, heads, n_maps, dk, dv,
               qcol, kcol, vcol, lam_init, name):
    tq = ATTN_TQ
    in_specs = [
        pl.BlockSpec((seq, n_maps * dk), lambda b, h: (b, qcol + h)),
        pl.BlockSpec((seq, n_maps * dk), lambda b, h: (b, kcol + h)),
        pl.BlockSpec((seq, dv), lambda b, h: (b, vcol + h)),
        pl.BlockSpec((seq, 1), lambda b, h: (b, 0)),
        pl.BlockSpec((1, seq), lambda b, h: (0, b)),
    ] + [pl.BlockSpec(e.shape, lambda b, h: (0, 0)) for e in extra]
    return pl.pallas_call(
        functools.partial(_attn_kernel, n_maps=n_maps, dk=dk, tq=tq, nq=seq // tq,
                          lam_init=lam_init),
        grid=(batch, heads),
        in_specs=in_specs,
        out_specs=pl.BlockSpec((seq, dv), lambda b, h: (b, h)),
        out_shape=jax.ShapeDtypeStruct((batch * seq, heads * dv), BF16),
        compiler_params=_cparams(2, VMEM_LIMIT),
        name=name,
    )(q, k, v, posq, posk, *extra)


def _cross_kernel(q_ref, k_ref, v_ref, o_ref):
    for h in range(CROSS_HEADS):
        sl = slice(h * CROSS_HEAD_DIM, (h + 1) * CROSS_HEAD_DIM)
        s = _dot_nt(q_ref[:, sl], k_ref[:, sl])
        p = jnp.exp(s - jnp.max(s, axis=-1, keepdims=True))
        inv = 1.0 / jnp.sum(p, axis=-1, keepdims=True)
        o = jnp.dot(p.astype(BF16), v_ref[:, sl], preferred_element_type=F32)
        o_ref[:, sl] = (o * inv).astype(o_ref.dtype)


def _cross_call(q, memkv, *, batch, seq, qcol, tq=512):
    nq = seq // tq
    return pl.pallas_call(
        _cross_kernel,
        grid=(batch, nq),
        in_specs=[pl.BlockSpec((tq, CROSS_WIDTH), lambda b, i: (b * nq + i, qcol)),
                  pl.BlockSpec((N_MEM, CROSS_WIDTH), lambda b, i: (b, 0)),
                  pl.BlockSpec((N_MEM, CROSS_WIDTH), lambda b, i: (b, 1))],
        out_specs=pl.BlockSpec((tq, CROSS_WIDTH), lambda b, i: (b * nq + i, 0)),
        out_shape=jax.ShapeDtypeStruct((batch * seq, CROSS_WIDTH), BF16),
        compiler_params=_cparams(2),
        name="cross_attn",
    )(q, memkv, memkv)


def _outproj_kernel(os_ref, oc_ref, w_ref, x_ref, o_ref, wbf_ref):
    @pl.when(pl.program_id(1) == 0)
    def _():
        wbf_ref[...] = w_ref[...].astype(BF16)

    acc = jnp.dot(os_ref[...], wbf_ref[:SELF_WIDTH, :], preferred_element_type=F32)
    acc = acc + jnp.dot(oc_ref[...], wbf_ref[SELF_WIDTH:, :], preferred_element_type=F32)
    o_ref[...] = x_ref[...] + acc


def _outproj_call(o_self, o_cross, w_out, x, *, tm=1024, tn=512):
    m = x.shape[0]
    return pl.pallas_call(
        _outproj_kernel,
        grid=(D_MODEL // tn, m // tm),
        in_specs=[pl.BlockSpec((tm, SELF_WIDTH), lambda j, i: (i, 0)),
                  pl.BlockSpec((tm, CROSS_WIDTH), lambda j, i: (i, 0)),
                  pl.BlockSpec((D_MODEL, tn), lambda j, i: (0, j)),
                  pl.BlockSpec((tm, tn), lambda j, i: (i, j))],
        out_specs=pl.BlockSpec((tm, tn), lambda j, i: (i, j)),
        out_shape=jax.ShapeDtypeStruct((m, D_MODEL), F32),
        scratch_shapes=[pltpu.VMEM((D_MODEL, tn), BF16)],
        compiler_params=_cparams(2, VMEM_LIMIT),
        name="out_proj",
    )(o_self, o_cross, w_out, x)


def _mla_in_kernel(h_ref, w_ref, gcq_ref, gckv_ref, gqm_ref, cq_ref, ckv_ref, qm_ref, kpe_ref):
    a = MLA_Q_RANK
    b = a + MLA_KV_RANK
    c = b + CROSS_WIDTH
    for r in range(h_ref.shape[0] // MM_SUB):
        rows = slice(r * MM_SUB, (r + 1) * MM_SUB)
        acc = jnp.dot(h_ref[rows, :], w_ref[...], preferred_element_type=F32)
        cq_ref[rows, :] = (_rms(acc[:, :a]) * gcq_ref[...]).astype(cq_ref.dtype)
        ckv_ref[rows, :] = (_rms(acc[:, a:b]) * gckv_ref[...]).astype(ckv_ref.dtype)
        for j in range(CROSS_HEADS):
            sl = slice(j * CROSS_HEAD_DIM, (j + 1) * CROSS_HEAD_DIM)
            qm_ref[rows, sl] = (_rms(acc[:, b + j * CROSS_HEAD_DIM:b + (j + 1) * CROSS_HEAD_DIM])
                                * gqm_ref[:, sl]).astype(qm_ref.dtype)
        kpe_ref[rows, :] = acc[:, c:]


def _mla_in_call(h, wcat, gcq, gckv, gqm, *, tm=512):
    m = h.shape[0]
    ncat = wcat.shape[1]
    row = lambda w: pl.BlockSpec((tm, w), lambda i: (i, 0))
    full = lambda r, w: pl.BlockSpec((r, w), lambda i: (0, 0))
    w_spec = pl.BlockSpec((D_MODEL, ncat), lambda i: (0, 0), pipeline_mode=pl.Buffered(1))
    return pl.pallas_call(
        _mla_in_kernel,
        grid=(m // tm,),
        in_specs=[row(D_MODEL), w_spec, full(1, MLA_Q_RANK), full(1, MLA_KV_RANK),
                  full(1, CROSS_WIDTH)],
        out_specs=[row(MLA_Q_RANK), row(MLA_KV_RANK), row(CROSS_WIDTH), row(LANES)],
        out_shape=[jax.ShapeDtypeStruct((m, MLA_Q_RANK), BF16),
                   jax.ShapeDtypeStruct((m, MLA_KV_RANK), BF16),
                   jax.ShapeDtypeStruct((m, CROSS_WIDTH), BF16),
                   jax.ShapeDtypeStruct((m, LANES), F32)],
        compiler_params=_cparams(1, VMEM_LIMIT),
        name="mla_in_proj",
    )(h, wcat, gcq, gckv, gqm)


MLA_GROUP = 4


def _rope_pe(x, c, s):
    lane = lax.broadcasted_iota(I32, x.shape, 1)
    partner = jnp.where((lane & 63) < 32, pltpu.roll(x, 96, axis=1), pltpu.roll(x, 32, axis=1))
    return x * c + partner * s


def _mla_q_kernel(cq_ref, wn_ref, wp_ref, wr_ref, gn_ref, gp_ref, gr_ref, cos_ref, sin_ref, o_ref):
    x = cq_ref[...]
    an = jnp.dot(x, wn_ref[...], preferred_element_type=F32)
    ap = jnp.dot(x, wp_ref[...], preferred_element_type=F32)
    ar = jnp.dot(x, wr_ref[...], preferred_element_type=F32)
    gc = cos_ref[...] * gp_ref[...]
    gs = sin_ref[...] * gr_ref[...]
    for j in range(MLA_GROUP):
        sl = slice(j * 128, (j + 1) * 128)
        nj = an[:, sl]
        pj = ap[:, sl]
        ss = jnp.sum(nj * nj + pj * pj, axis=-1, keepdims=True)
        rs = lax.rsqrt(ss * (1.0 / MLA_QK_DIM) + NORM_EPS)
        o_ref[:, j * 256:j * 256 + 128] = (nj * rs * gn_ref[...]).astype(o_ref.dtype)
        o_ref[:, j * 256 + 128:(j + 1) * 256] = ((pj * gc + ar[:, sl] * gs) * rs).astype(o_ref.dtype)


def _mla_q_call(cq, w_nope, w_pe, w_rot, gn, gp, gr, cos, sin, *, tm=1024):
    m = cq.shape[0]
    g = MLA_HEADS // MLA_GROUP
    tm = min(tm, m)
    wspec = pl.BlockSpec((MLA_Q_RANK, MLA_GROUP * 128), lambda j, i: (0, j))
    gspec = pl.BlockSpec((1, LANES), lambda j, i: (0, 0))
    tspec = pl.BlockSpec((tm, LANES), lambda j, i: (i, 0))
    return pl.pallas_call(
        _mla_q_kernel,
        grid=(g, m // tm),
        in_specs=[pl.BlockSpec((tm, MLA_Q_RANK), lambda j, i: (i, 0)),
                  wspec, wspec, wspec, gspec, gspec, gspec, tspec, tspec],
        out_specs=pl.BlockSpec((tm, MLA_GROUP * MLA_PAD_DIM), lambda j, i: (i, j)),
        out_shape=jax.ShapeDtypeStruct((m, MLA_HEADS * MLA_PAD_DIM), BF16),
        compiler_params=_cparams(2),
        name="mla_q_up",
    )(cq, w_nope, w_pe, w_rot, gn, gp, gr, cos, sin)


def _mla_kv_kernel(ckv_ref, w_ref, kpe_ref, gn_ref, gp_ref, cos_ref, sin_ref, k_ref, v_ref):
    acc = jnp.dot(ckv_ref[...], w_ref[...], preferred_element_type=F32)
    kpe = kpe_ref[...]
    ss_pe = jnp.sum(kpe * kpe, axis=-1, keepdims=True)
    pe = _rope_pe(kpe * gp_ref[...], cos_ref[...], sin_ref[...])
    for j in range(MLA_GROUP):
        kn = acc[:, j * 256:j * 256 + 128]
        ss = jnp.sum(kn * kn, axis=-1, keepdims=True) + ss_pe
        rs = lax.rsqrt(ss * (1.0 / MLA_QK_DIM) + NORM_EPS)
        k_ref[:, j * 256:j * 256 + 128] = (kn * rs * gn_ref[...]).astype(k_ref.dtype)
        k_ref[:, j * 256 + 128:(j + 1) * 256] = (pe * rs).astype(k_ref.dtype)
        v_ref[:, j * 128:(j + 1) * 128] = acc[:, j * 256 + 128:(j + 1) * 256].astype(v_ref.dtype)


def _mla_kv_call(ckv, w_ukv, kpe, gn, gp, cos, sin, *, tm=1024):
    m = ckv.shape[0]
    g = MLA_HEADS // MLA_GROUP
    tm = min(tm, m)
    return pl.pallas_call(
        _mla_kv_kernel,
        grid=(g, m // tm),
        in_specs=[pl.BlockSpec((tm, MLA_KV_RANK), lambda j, i: (i, 0)),
                  pl.BlockSpec((MLA_KV_RANK, MLA_GROUP * 256), lambda j, i: (0, j)),
                  pl.BlockSpec((tm, LANES), lambda j, i: (i, 0)),
                  pl.BlockSpec((1, LANES), lambda j, i: (0, 0)),
                  pl.BlockSpec((1, LANES), lambda j, i: (0, 0)),
                  pl.BlockSpec((tm, LANES), lambda j, i: (i, 0)),
                  pl.BlockSpec((tm, LANES), lambda j, i: (i, 0))],
        out_specs=[pl.BlockSpec((tm, MLA_GROUP * MLA_PAD_DIM), lambda j, i: (i, j)),
                   pl.BlockSpec((tm, MLA_GROUP * MLA_V_DIM), lambda j, i: (i, j))],
        out_shape=[jax.ShapeDtypeStruct((m, MLA_HEADS * MLA_PAD_DIM), BF16),
                   jax.ShapeDtypeStruct((m, MLA_HEADS * MLA_V_DIM), BF16)],
        compiler_params=_cparams(2),
        name="mla_kv_up",
    )(ckv, w_ukv, kpe, gn, gp, cos, sin)


HALF = D_MODEL // 2
U32 = jnp.uint32
HI_MASK = 0xFFFF0000


def _pack_bf16_pairs(x):
    bits = lax.bitcast_convert_type(x.astype(BF16).astype(F32), U32)
    return (bits[:, :HALF] >> 16) | bits[:, HALF:]


def _unpack_bf16_pairs(w):
    lo = lax.bitcast_convert_type(w << 16, F32)
    hi = lax.bitcast_convert_type(w & U32(HI_MASK), F32)
    return lo, hi


def _route_kernel(x_ref, g_ref, w_ref, b_ref, xp_ref, ids_ref, wts_ref):
    xn = _rms(x_ref[...]) * g_ref[...]
    xp_ref[...] = _pack_bf16_pairs(xn)
    x_hi = xn.astype(BF16)
    x_lo = (xn - x_hi.astype(F32)).astype(BF16)
    w = w_ref[...]
    w_hi = w.astype(BF16)
    w_lo = (w - w_hi.astype(F32)).astype(BF16)
    lg = _dot_nt(w_hi, x_hi) + _dot_nt(w_hi, x_lo) + _dot_nt(w_lo, x_hi) + b_ref[...]
    tm = lg.shape[1]
    iota = lax.broadcasted_iota(I32, (EXPERTS_PER_GROUP, tm), 0)

    def first_argmax(v):
        mx = jnp.max(v, axis=0, keepdims=True)
        idx = jnp.min(jnp.where(v == mx, iota, EXPERTS_PER_GROUP), axis=0, keepdims=True)
        return mx, idx

    gl = lg[0:N_GROUPS, :]
    gmax, gsel = first_argmax(gl)
    g_gate = 1.0 / jnp.sum(jnp.exp(gl - gmax), axis=0, keepdims=True)
    el = jnp.zeros((EXPERTS_PER_GROUP, tm), F32)
    for g in range(N_GROUPS):
        lo = N_GROUPS + g * EXPERTS_PER_GROUP
        el = jnp.where(gsel == g, lg[lo:lo + EXPERTS_PER_GROUP, :], el)
    v1, i1 = first_argmax(el)
    v2, i2 = first_argmax(jnp.where(iota == i1, -jnp.inf, el))
    e = jnp.exp(v2 - v1)
    w1 = g_gate / (1.0 + e)
    w2 = g_gate * e / (1.0 + e)
    ids_ref[...] = jnp.concatenate([gsel * EXPERTS_PER_GROUP + i1,
                                    gsel * EXPERTS_PER_GROUP + i2], axis=0)
    wts_ref[...] = jnp.concatenate([w1, w2], axis=0)


def _route_call(x, g, w_t, b_col, *, tm=256):
    m = x.shape[0]
    return pl.pallas_call(
        _route_kernel,
        grid=(m // tm,),
        in_specs=[pl.BlockSpec((tm, D_MODEL), lambda i: (i, 0)),
                  pl.BlockSpec((1, D_MODEL), lambda i: (0, 0)),
                  pl.BlockSpec((LANES, D_MODEL), lambda i: (0, 0)),
                  pl.BlockSpec((LANES, 1), lambda i: (0, 0))],
        out_specs=[pl.BlockSpec((tm, HALF), lambda i: (i, 0)),
                   pl.BlockSpec((TOP_K, tm), lambda i: (0, i)),
                   pl.BlockSpec((TOP_K, tm), lambda i: (0, i))],
        out_shape=[jax.ShapeDtypeStruct((m, HALF), U32),
                   jax.ShapeDtypeStruct((TOP_K, m), I32),
                   jax.ShapeDtypeStruct((TOP_K, m), F32)],
        compiler_params=_cparams(1, VMEM_LIMIT),
        name="moe_route",
    )(x, g, w_t, b_col)


def _sort_kernel(ids_ref, pos_ref, te_ref, nu_ref, rs_ref, cnt_ref, start_ref, *, ts):
    n_tok = ids_ref.shape[1]
    nck = n_tok // ts
    iota_e = lax.broadcasted_iota(I32, (N_EXPERTS, ts), 0)

    def chunk(c):
        return pl.ds(pl.multiple_of(c * ts, ts), ts)

    def count_body(c, acc):
        for k in range(TOP_K):
            oh = jnp.where(iota_e == ids_ref[pl.ds(k, 1), chunk(c)], 1.0, 0.0)
            acc = acc + jnp.sum(oh, axis=1, keepdims=True)
        return acc

    counts = lax.fori_loop(0, nck, count_body, jnp.zeros((N_EXPERTS, 1), F32))
    er = lax.broadcasted_iota(I32, (N_EXPERTS, N_EXPERTS), 0)
    ec = lax.broadcasted_iota(I32, (N_EXPERTS, N_EXPERTS), 1)
    strict_lower = jnp.where(ec < er, 1.0, 0.0).astype(BF16)

    def excl_cumsum(v):
        vb = jnp.broadcast_to(v, (N_EXPERTS, LANES)).astype(BF16)
        return jnp.dot(strict_lower, vb, preferred_element_type=F32)

    padded = jnp.floor((counts + (ROW_ALIGN - 1)) * (1.0 / ROW_ALIGN)) * ROW_ALIGN
    hi = jnp.floor(padded * (1.0 / LANES))
    start = excl_cumsum(hi) * LANES + excl_cumsum(padded - hi * LANES)
    chunks = jnp.floor((counts + (MOE_TM - 0.5)) * (1.0 / MOE_TM))
    first = excl_cumsum(chunks)
    ends = first + chunks
    total = jnp.max(ends, axis=0, keepdims=True)
    chunk_i = lax.broadcasted_iota(I32, (N_EXPERTS, LANES), 1).astype(F32)
    chunk_c = jnp.minimum(chunk_i, total - 1.0)
    te = jnp.sum(jnp.where(ends <= chunk_c, 1.0, 0.0), axis=0, keepdims=True)
    mine = lax.broadcasted_iota(I32, (N_EXPERTS, LANES), 0).astype(F32) == te
    rs = jnp.sum(jnp.where(mine, start + (chunk_c - first) * MOE_TM, 0.0), axis=0, keepdims=True)
    te_ref[...] = te.astype(I32)
    nu_ref[...] = total.astype(I32)
    rs_ref[...] = rs.astype(I32)
    cnt_ref[...] = counts.astype(I32)
    start_ref[...] = start[:, 0:1].astype(I32)

    row_base = start[:, 0:1]
    ur = lax.broadcasted_iota(I32, (ts, ts), 0)
    uc = lax.broadcasted_iota(I32, (ts, ts), 1)
    upper = jnp.where(ur <= uc, 1.0, 0.0).astype(BF16)

    def pos_body(k):
        def body(c, carry):
            hit = iota_e == ids_ref[pl.ds(k, 1), chunk(c)]
            incl = jnp.dot(jnp.where(hit, 1.0, 0.0).astype(BF16), upper,
                           preferred_element_type=F32)
            val = row_base + carry + incl - 1.0
            p = jnp.sum(jnp.where(hit, val, 0.0), axis=0, keepdims=True)
            pos_ref[pl.ds(k, 1), chunk(c)] = p.astype(I32)
            return carry + incl[:, ts - 1:ts]
        return body

    carry = jnp.zeros((N_EXPERTS, 1), F32)
    for k in range(TOP_K):
        carry = lax.fori_loop(0, nck, pos_body(k), carry)


def _sort_call(ids, *, ts=512):
    n_tok = ids.shape[1]
    ts = min(ts, n_tok)
    return pl.pallas_call(
        functools.partial(_sort_kernel, ts=ts),
        out_shape=[jax.ShapeDtypeStruct((TOP_K, n_tok), I32),
                   jax.ShapeDtypeStruct((1, LANES), I32),
                   jax.ShapeDtypeStruct((1, LANES), I32),
                   jax.ShapeDtypeStruct((1, LANES), I32),
                   jax.ShapeDtypeStruct((N_EXPERTS, 1), I32),
                   jax.ShapeDtypeStruct((N_EXPERTS, 1), I32)],
        name="moe_sort",
    )(ids)


def _sorted_rows(n_tok):
    return TOP_K * n_tok + N_EXPERTS * ROW_ALIGN + MOE_TM


def _moe_chunks(n_tok):
    return -(-(TOP_K * n_tok) // MOE_TM) + N_EXPERTS


def _dispatch_kernel(pos_ref, cnt_ref, start_ref, xp_ref, xs_hbm, stage, zbuf, zsem, rsem, *, n_tok):
    i = pl.program_id(0)
    nb = pl.num_programs(0)
    tm = xp_ref.shape[0]
    slot = i % 2
    n_slots = TOP_K * n_tok
    n_rows = xs_hbm.shape[0]

    @pl.when(i == 0)
    def _():
        zbuf[...] = jnp.zeros(zbuf.shape, zbuf.dtype)

        def tail_copy(j):
            return pltpu.make_async_copy(
                zbuf, xs_hbm.at[pl.ds(n_slots + j * DISPATCH_ZROWS, DISPATCH_ZROWS)], zsem.at[0])

        n_tail = (n_rows - n_slots) // DISPATCH_ZROWS
        for j in range(n_tail):
            tail_copy(j).start()
        for j in range(n_tail):
            tail_copy(j).wait()

        def zero_row(dst):
            return pltpu.make_async_copy(zbuf.at[pl.ds(0, 1)], xs_hbm.at[pl.ds(dst, 1)], zsem.at[0])

        def per_expert(e, c):
            n = cnt_ref[e]
            first = start_ref[e] + n
            n_pad = (-n) & (ROW_ALIGN - 1)

            def put(r, c2):
                zero_row(first + r).start()
                return c2
            lax.fori_loop(0, n_pad, put, 0)

            def done(r, c2):
                zero_row(0).wait()
                return c2
            lax.fori_loop(0, n_pad, done, 0)
            return c

        lax.fori_loop(0, N_EXPERTS, per_expert, 0)

    def wait_stage(sl):
        for k in range(TOP_K):
            pltpu.make_async_copy(stage.at[sl], xs_hbm.at[pl.ds(0, tm)], rsem.at[sl]).wait()

    @pl.when(i >= 2)
    def _():
        wait_stage(slot)

    stage[slot] = xp_ref[...]

    def group(g, c):
        for u in range(DMA_GROUP):
            r = g * DMA_GROUP + u
            for k in range(TOP_K):
                dst = pos_ref[k * n_tok + i * tm + r]
                pltpu.make_async_copy(stage.at[slot, pl.ds(r, 1)], xs_hbm.at[pl.ds(dst, 1)],
                                      rsem.at[slot]).start()
        return c
    lax.fori_loop(0, tm // DMA_GROUP, group, 0)

    @pl.when(i == nb - 1)
    def _():
        wait_stage(1 - slot)
        wait_stage(slot)


def _dispatch_call(pos_flat, cnt, start, xp, *, tm=256):
    n_tok = xp.shape[0]
    assert n_tok // tm >= 2
    assert (N_EXPERTS * ROW_ALIGN + MOE_TM) % DISPATCH_ZROWS == 0
    grid_spec = pltpu.PrefetchScalarGridSpec(
        num_scalar_prefetch=3,
        grid=(n_tok // tm,),
        in_specs=[pl.BlockSpec((tm, HALF), lambda i, p, c, s: (i, 0))],
        out_specs=pl.BlockSpec(memory_space=pl.ANY),
        scratch_shapes=[pltpu.VMEM((2, tm, HALF), U32),
                        pltpu.VMEM((DISPATCH_ZROWS, HALF), U32),
                        pltpu.SemaphoreType.DMA((1,)),
                        pltpu.SemaphoreType.DMA((2,))],
    )
    return pl.pallas_call(
        functools.partial(_dispatch_kernel, n_tok=n_tok),
        grid_spec=grid_spec,
        out_shape=jax.ShapeDtypeStruct((_sorted_rows(n_tok), HALF), U32),
        compiler_params=_cparams(1),
        name="moe_dispatch",
    )(pos_flat, cnt, start, xp)


def _moe_kernel(te_ref, nu_ref, rs_ref, xs_hbm, wg_ref, wu_ref, wd_ref, y_hbm,
                xbuf, xb, acc, gu, wgu, gsem, ssem, *, n_slots):
    i = pl.program_id(0)
    k = pl.program_id(1)
    nk = pl.num_programs(1)
    n_used = nu_ref[0]
    valid = i < n_used
    slot = i % 2

    def x_copy(c, sl):
        rows = pl.ds(pl.multiple_of(rs_ref[c], ROW_ALIGN), MOE_TM)
        return pltpu.make_async_copy(xs_hbm.at[rows], xbuf.at[sl], gsem.at[sl])

    def y_copy(c):
        rows = pl.ds(pl.multiple_of(rs_ref[c], ROW_ALIGN), MOE_TM)
        return pltpu.make_async_copy(acc, y_hbm.at[rows], ssem.at[0])

    @pl.when((i == 0) & (k == 0))
    def _():
        x_copy(0, 0).start()
        acc[...] = jnp.zeros(acc.shape, acc.dtype)
        n_rows = y_hbm.shape[0]
        tails = [(off, min(MOE_TM, n_rows - off)) for off in range(n_slots, n_rows, MOE_TM)]
        for off, size in tails:
            pltpu.make_async_copy(acc.at[pl.ds(0, size)], y_hbm.at[pl.ds(off, size)],
                                  ssem.at[0]).start()
        for off, size in tails:
            pltpu.make_async_copy(acc.at[pl.ds(0, size)], y_hbm.at[pl.ds(off, size)],
                                  ssem.at[0]).wait()

    @pl.when(valid & (k == 0))
    def _():
        x_copy(i, slot).wait()

        @pl.when(i + 1 < n_used)
        def _():
            x_copy(i + 1, 1 - slot).start()

        lo, hi = _unpack_bf16_pairs(xbuf[slot])
        xb[:, :HALF] = lo.astype(BF16)
        xb[:, HALF:] = hi.astype(BF16)

    @pl.when(valid)
    def _():
        wgu[:, :D_EXPERT] = wg_ref[0].astype(BF16)
        wgu[:, D_EXPERT:] = wu_ref[0].astype(BF16)
        xs = xb[:, pl.ds(pl.multiple_of(k * MOE_TK, MOE_TK), MOE_TK)]
        part = jnp.dot(xs, wgu[...], preferred_element_type=F32)

        @pl.when(k == 0)
        def _():
            gu[...] = part

        @pl.when(k > 0)
        def _():
            gu[...] += part

        @pl.when(k == nk - 1)
        def _():
            a = gu[:, :D_EXPERT]
            hid = (a * jax.nn.sigmoid(a)) * gu[:, D_EXPERT:]
            out = jnp.dot(hid.astype(BF16), wd_ref[0].astype(BF16), preferred_element_type=F32)

            @pl.when(i > 0)
            def _():
                y_copy(i - 1).wait()
            acc[...] = _pack_bf16_pairs(out)
            y_copy(i).start()

            @pl.when(i == n_used - 1)
            def _():
                y_copy(i).wait()


def _moe_call(te, nu, rs, xs, w_gate, w_up, w_down, *, n_tok):
    nk = D_MODEL // MOE_TK
    last_k = nk - 1

    def w_in_map(i, k, te_r, nu_r, rs_r):
        return te_r[i], jnp.where(i < nu_r[0], k, last_k), 0

    def w_out_map(i, k, te_r, nu_r, rs_r):
        return te_r[i], 0, 0

    grid_spec = pltpu.PrefetchScalarGridSpec(
        num_scalar_prefetch=3,
        grid=(_moe_chunks(n_tok), nk),
        in_specs=[
            pl.BlockSpec(memory_space=pl.ANY),
            pl.BlockSpec((1, MOE_TK, D_EXPERT), w_in_map),
            pl.BlockSpec((1, MOE_TK, D_EXPERT), w_in_map),
            pl.BlockSpec((1, D_EXPERT, D_MODEL), w_out_map),
        ],
        out_specs=pl.BlockSpec(memory_space=pl.ANY),
        scratch_shapes=[pltpu.VMEM((2, MOE_TM, HALF), U32),
                        pltpu.VMEM((MOE_TM, D_MODEL), BF16),
                        pltpu.VMEM((MOE_TM, HALF), U32),
                        pltpu.VMEM((MOE_TM, 2 * D_EXPERT), F32),
                        pltpu.VMEM((MOE_TK, 2 * D_EXPERT), BF16),
                        pltpu.SemaphoreType.DMA((2,)),
                        pltpu.SemaphoreType.DMA((1,))],
    )
    return pl.pallas_call(
        functools.partial(_moe_kernel, n_slots=TOP_K * n_tok),
        grid_spec=grid_spec,
        out_shape=jax.ShapeDtypeStruct(xs.shape, U32),
        compiler_params=_cparams(2, VMEM_LIMIT),
        name="moe_experts",
    )(te, nu, rs, xs, w_gate, w_up, w_down)


def _combine_kernel(pos_ref, x_ref, w_ref, g_ref, y_hbm, xo_ref, *rest, n_tok):
    *h_ref, ybuf, sem = rest
    i = pl.program_id(0)
    nb = pl.num_programs(0)
    tm = x_ref.shape[0]
    slot = i % 2

    def fetch(tile, sl):
        def group(g, c):
            for u in range(DMA_GROUP):
                r = g * DMA_GROUP + u
                for k in range(TOP_K):
                    src = pos_ref[k * n_tok + tile * tm + r]
                    pltpu.make_async_copy(y_hbm.at[pl.ds(src, 1)], ybuf.at[sl, k, pl.ds(r, 1)],
                                          sem.at[sl]).start()
            return c
        lax.fori_loop(0, tm // DMA_GROUP, group, 0)

    @pl.when(i == 0)
    def _():
        fetch(0, 0)

    @pl.when(i + 1 < nb)
    def _():
        fetch(i + 1, 1 - slot)

    for k in range(TOP_K):
        pltpu.make_async_copy(y_hbm.at[pl.ds(0, tm)], ybuf.at[slot, k], sem.at[slot]).wait()
    w = w_ref[...]
    lo0, hi0 = _unpack_bf16_pairs(ybuf[slot, 0])
    lo1, hi1 = _unpack_bf16_pairs(ybuf[slot, 1])
    x_lo = x_ref[:, :HALF] + w[:, 0:1] * lo0 + w[:, 1:2] * lo1
    x_hi = x_ref[:, HALF:] + w[:, 0:1] * hi0 + w[:, 1:2] * hi1
    xo_ref[:, :HALF] = x_lo
    xo_ref[:, HALF:] = x_hi
    if h_ref:
        ss = (jnp.sum(x_lo * x_lo, axis=-1, keepdims=True)
              + jnp.sum(x_hi * x_hi, axis=-1, keepdims=True))
        rs = lax.rsqrt(ss * (1.0 / D_MODEL) + NORM_EPS)
        h_ref[0][:, :HALF] = (x_lo * rs * g_ref[:, :HALF]).astype(BF16)
        h_ref[0][:, HALF:] = (x_hi * rs * g_ref[:, HALF:]).astype(BF16)


def _combine_call(pos_flat, x, y, wts_t, g_next, *, tm=256):
    m, d = x.shape
    with_norm = g_next is not None
    g = g_next.reshape(1, d) if with_norm else jnp.ones((1, d), F32)
    row = pl.BlockSpec((tm, d), lambda i, pos_r: (i, 0))
    out_specs = [row, row] if with_norm else [row]
    out_shape = [jax.ShapeDtypeStruct((m, d), F32)]
    if with_norm:
        out_shape.append(jax.ShapeDtypeStruct((m, d), BF16))
    grid_spec = pltpu.PrefetchScalarGridSpec(
        num_scalar_prefetch=1,
        grid=(m // tm,),
        in_specs=[row,
                  pl.BlockSpec((tm, TOP_K), lambda i, pos_r: (i, 0)),
                  pl.BlockSpec((1, d), lambda i, pos_r: (0, 0)),
                  pl.BlockSpec(memory_space=pl.ANY)],
        out_specs=out_specs,
        scratch_shapes=[pltpu.VMEM((2, TOP_K, tm, HALF), U32),
                        pltpu.SemaphoreType.DMA((2,))],
    )
    return pl.pallas_call(
        functools.partial(_combine_kernel, n_tok=m),
        grid_spec=grid_spec,
        out_shape=out_shape,
        compiler_params=_cparams(1, VMEM_LIMIT),
        name="moe_combine",
    )(pos_flat, x, wts_t, g, y)


def _moe_layer(x, norm_g, w_group, b_group, w_router, b_router, w_gate, w_up, w_down, g_next):
    n_tok = x.shape[0]
    pad_rows = LANES - N_GROUPS - N_EXPERTS
    w_t = jnp.concatenate([w_group.T, w_router.T, jnp.zeros((pad_rows, D_MODEL), F32)], axis=0)
    b_col = jnp.concatenate([b_group, b_router, jnp.zeros((pad_rows,), F32)]).reshape(LANES, 1)
    xp, ids, wts = _route_call(x, norm_g.reshape(1, D_MODEL), w_t, b_col)
    pos, te, nu, rs, cnt, start = _sort_call(ids)
    pos_flat = pos.reshape(-1)
    xs = _dispatch_call(pos_flat, cnt.reshape(-1), start.reshape(-1), xp)
    y = _moe_call(te.reshape(-1), nu.reshape(-1)[:1], rs.reshape(-1), xs,
                  w_gate.reshape(N_EXPERTS, D_MODEL, D_EXPERT),
                  w_up.reshape(N_EXPERTS, D_MODEL, D_EXPERT),
                  w_down.reshape(N_EXPERTS, D_EXPERT, D_MODEL), n_tok=n_tok)
    return _combine_call(pos_flat, x, y, wts.T, g_next)


def _rope_tables(pos_flat, dim):
    inv_freq = ROPE_THETA ** (-jnp.arange(0, dim, 2, dtype=F32) / dim)
    ang = pos_flat.astype(F32)[:, None] * inv_freq
    return jnp.cos(ang), jnp.sin(ang)


def kernel(x, mem, positions, mem_norm_g, w_mem_kv, mem_k_norm_g, l0_attn_norm_g, l0_w_in, l0_q_norm_g, l0_k_norm_g, l0_lambda_q1, l0_lambda_k1, l0_lambda_q2, l0_lambda_k2, l0_subln_g, l0_cross_q_norm_g, l0_w_out, l0_ffn_norm_g, l0_w_group, l0_b_group, l0_w_router, l0_b_router, l0_w_gate, l0_w_up, l0_w_down, l1_attn_norm_g, l1_w_in, l1_cq_norm_g, l1_ckv_norm_g, l1_w_uq, l1_w_ukv, l1_q_norm_g, l1_k_norm_g, l1_cross_q_norm_g, l1_w_out, l1_ffn_norm_g, l1_w_group, l1_b_group, l1_w_router, l1_b_router, l1_w_gate, l1_w_up, l1_w_down):
    batch, seq, d = x.shape
    n_tok = batch * seq
    xf = x.reshape(n_tok, d)
    pos_flat = positions.reshape(n_tok)
    posq = pos_flat.reshape(n_tok, 1)
    posk = pos_flat.reshape(1, n_tok)
    ones128 = jnp.ones((n_tok, LANES), F32)

    c64, s64 = _rope_tables(pos_flat, DIFF_HEAD_DIM)
    cos_full = jnp.concatenate([c64, c64], axis=1)
    sin_full = jnp.concatenate([-s64, s64], axis=1)
    c32, s32 = _rope_tables(pos_flat, MLA_ROPE_DIM)
    z32 = jnp.zeros_like(c32)
    cos_k = jnp.concatenate([c32, c32, z32, z32], axis=1)
    sin_k = jnp.concatenate([-s32, s32, z32, z32], axis=1)

    cross_scale = CROSS_HEAD_DIM ** -0.5

    memn = _norm_call(mem.reshape(batch * N_MEM, d), mem_norm_g)
    mem_gain = jnp.concatenate([jnp.tile(mem_k_norm_g, CROSS_HEADS), jnp.ones((CROSS_WIDTH,), F32)])
    k_tiles = CROSS_WIDTH // 512
    memkv = _mm_call(memn, w_mem_kv, mem_gain.reshape(1, -1), ones128[:batch * N_MEM],
                     ones128[:batch * N_MEM], ((0, k_tiles, "head256"), (k_tiles, 2 * k_tiles, "plain")),
                     tm=batch * N_MEM, tn=512, name="mem_kv")

    h0 = _norm_call(xf, l0_attn_norm_g)
    qk_tiles = SELF_WIDTH // 512
    gain0 = jnp.concatenate([
        jnp.tile(l0_q_norm_g, 2 * DIFF_HEADS) * (DIFF_HEAD_DIM ** -0.5),
        jnp.tile(l0_k_norm_g, 2 * DIFF_HEADS),
        jnp.ones((SELF_WIDTH,), F32),
        jnp.tile(l0_cross_q_norm_g, CROSS_HEADS) * cross_scale]).reshape(1, -1)
    proj0 = _mm_call(h0, l0_w_in, gain0, cos_full, sin_full,
                     ((0, 2 * qk_tiles, "head128_rope"), (2 * qk_tiles, 3 * qk_tiles, "plain"),
                      (3 * qk_tiles, 3 * qk_tiles + CROSS_WIDTH // 512, "head256")),
                     tm=1024, tn=512, name="l0_in_proj")
    lam_init = 0.8 - 0.6 * math.exp(-0.3 * 0)
    row = lambda v: v.reshape(1, -1)
    o_self = _attn_call(proj0, proj0, proj0, posq, posk,
                        (row(l0_lambda_q1), row(l0_lambda_k1), row(l0_lambda_q2),
                         row(l0_lambda_k2), row(l0_subln_g)),
                        batch=batch, seq=seq, heads=DIFF_HEADS, n_maps=2, dk=DIFF_HEAD_DIM,
                        dv=DIFF_V_DIM, qcol=0, kcol=DIFF_HEADS, vcol=2 * DIFF_HEADS,
                        lam_init=lam_init, name="diff_attn")
    o_cross = _cross_call(proj0, memkv, batch=batch, seq=seq, qcol=3 * SELF_WIDTH // CROSS_WIDTH)
    x1 = _outproj_call(o_self, o_cross, l0_w_out, xf)
    x2, h1 = _moe_layer(x1, l0_ffn_norm_g, l0_w_group, l0_b_group, l0_w_router, l0_b_router,
                        l0_w_gate, l0_w_up, l0_w_down, l1_attn_norm_g)

    a = MLA_Q_RANK
    b = a + MLA_KV_RANK
    c = b + MLA_ROPE_DIM
    wcat = jnp.concatenate([l1_w_in[:, :b], l1_w_in[:, c:], l1_w_in[:, b:c],
                            jnp.zeros((d, LANES - MLA_ROPE_DIM), F32)], axis=1).astype(BF16)
    cq, ckv, qm, kpe = _mla_in_call(
        h1, wcat, row(l1_cq_norm_g), row(l1_ckv_norm_g),
        row(jnp.tile(l1_cross_q_norm_g, CROSS_HEADS) * cross_scale))
    w_uq3 = l1_w_uq.reshape(MLA_Q_RANK, MLA_HEADS, MLA_QK_DIM)
    w_q_nope = w_uq3[:, :, :MLA_NOPE_DIM].reshape(MLA_Q_RANK, -1).astype(BF16)
    partner = (jnp.arange(MLA_ROPE_DIM) + MLA_ROPE_DIM // 2) % MLA_ROPE_DIM
    w_pe3 = w_uq3[:, :, MLA_NOPE_DIM:]
    pad3 = jnp.zeros_like(w_pe3)
    w_q_pe = jnp.concatenate([w_pe3, pad3], axis=2).reshape(MLA_Q_RANK, -1).astype(BF16)
    w_q_rot = jnp.concatenate([w_pe3[:, :, partner], pad3], axis=2).reshape(MLA_Q_RANK, -1).astype(BF16)
    q_scale = MLA_QK_DIM ** -0.5
    zeros64 = jnp.zeros((MLA_ROPE_DIM,), F32)
    g_q_pe = l1_q_norm_g[MLA_NOPE_DIM:] * q_scale
    q_pad = _mla_q_call(cq, w_q_nope, w_q_pe, w_q_rot,
                        row(l1_q_norm_g[:MLA_NOPE_DIM] * q_scale),
                        row(jnp.concatenate([g_q_pe, zeros64])),
                        row(jnp.concatenate([g_q_pe[partner], zeros64])), cos_k, sin_k)
    k_pad, v1 = _mla_kv_call(ckv, l1_w_ukv.astype(BF16), kpe,
                             row(l1_k_norm_g[:MLA_NOPE_DIM]),
                             row(jnp.concatenate([l1_k_norm_g[MLA_NOPE_DIM:], zeros64])),
                             cos_k, sin_k)
    o_self1 = _attn_call(q_pad, k_pad, v1, posq, posk, (),
                         batch=batch, seq=seq, heads=MLA_HEADS, n_maps=1, dk=MLA_PAD_DIM,
                         dv=MLA_V_DIM, qcol=0, kcol=0, vcol=0, lam_init=0.0, name="mla_attn")
    o_cross1 = _cross_call(qm, memkv, batch=batch, seq=seq, qcol=0)
    x3 = _outproj_call(o_self1, o_cross1, l1_w_out, x2)
    (x4,) = _moe_layer(x3, l1_ffn_norm_g, l1_w_group, l1_b_group, l1_w_router, l1_b_router,
                       l1_w_gate, l1_w_up, l1_w_down, None)
    return x4.reshape(batch, seq, d)
```

```python
import functools
import math

import jax
import jax.numpy as jnp
from jax import lax
from jax.experimental import pallas as pl
from jax.experimental.pallas import tpu as pltpu

F32 = jnp.float32
BF16 = jnp.bfloat16
I32 = jnp.int32

D_MODEL = 4096
N_MEM = 256
ROPE_THETA = 10000.0
NORM_EPS = 1e-6

SELF_WIDTH = 3 * D_MODEL // 4
CROSS_HEADS = 4
CROSS_HEAD_DIM = (D_MODEL // 4) // CROSS_HEADS
CROSS_WIDTH = CROSS_HEADS * CROSS_HEAD_DIM

DIFF_HEAD_DIM = 128
DIFF_V_DIM = 2 * DIFF_HEAD_DIM
DIFF_HEADS = SELF_WIDTH // DIFF_V_DIM

MLA_NOPE_DIM = 128
MLA_ROPE_DIM = 64
MLA_V_DIM = 128
MLA_HEADS = SELF_WIDTH // MLA_V_DIM
MLA_QK_DIM = MLA_NOPE_DIM + MLA_ROPE_DIM
MLA_Q_RANK = 3 * D_MODEL // 16
MLA_KV_RANK = D_MODEL // 8
MLA_PAD_DIM = 256

N_GROUPS = 8
EXPERTS_PER_GROUP = 8
N_EXPERTS = N_GROUPS * EXPERTS_PER_GROUP
D_EXPERT = 3 * D_MODEL // 32
TOP_K = 2

LANES = 128
MOE_TM = 320
MOE_TK = 2048
DMA_GROUP = 8
ROW_ALIGN = 8
DISPATCH_ZROWS = 64
ATTN_TQ = 256
VMEM_LIMIT = 56 * 1024 * 1024


def _cparams(n_axes, vmem=None):
    return pltpu.CompilerParams(dimension_semantics=("arbitrary",) * n_axes,
                                vmem_limit_bytes=vmem)


def _rms(x):
    return x * lax.rsqrt(jnp.mean(x * x, axis=-1, keepdims=True) + NORM_EPS)


def _dot_nt(a, b):
    return lax.dot_general(a, b, (((1,), (1,)), ((), ())), preferred_element_type=F32)


def _norm_kernel(x_ref, g_ref, o_ref):
    o_ref[...] = (_rms(x_ref[...]) * g_ref[...]).astype(o_ref.dtype)


def _norm_call(x, g, tm=256):
    m, d = x.shape
    return pl.pallas_call(
        _norm_kernel,
        grid=(m // tm,),
        in_specs=[pl.BlockSpec((tm, d), lambda i: (i, 0)),
                  pl.BlockSpec((1, d), lambda i: (0, 0))],
        out_specs=pl.BlockSpec((tm, d), lambda i: (i, 0)),
        out_shape=jax.ShapeDtypeStruct((m, d), BF16),
        compiler_params=_cparams(1),
        name="rmsnorm",
    )(x, g.reshape(1, d))


def _epilogue(kind, acc, rows, gain_ref, cos_ref, sin_ref, o_ref):
    tn = acc.shape[1]
    if kind == "plain":
        o_ref[rows, :] = acc.astype(o_ref.dtype)
    elif kind == "head128_rope":
        c = cos_ref[rows, :]
        s = sin_ref[rows, :]
        for j in range(tn // 128):
            sl = slice(j * 128, (j + 1) * 128)
            y = _rms(acc[:, sl]) * gain_ref[:, sl]
            y = y * c + pltpu.roll(y, 64, axis=1) * s
            o_ref[rows, sl] = y.astype(o_ref.dtype)
    elif kind == "head256":
        for j in range(tn // 256):
            sl = slice(j * 256, (j + 1) * 256)
            o_ref[rows, sl] = (_rms(acc[:, sl]) * gain_ref[:, sl]).astype(o_ref.dtype)
    else:
        raise ValueError(kind)


MM_SUB = 256


def _mm_kernel(x_ref, w_ref, gain_ref, cos_ref, sin_ref, o_ref, wbf_ref, *, kinds):
    n = pl.program_id(0)

    @pl.when(pl.program_id(1) == 0)
    def _():
        wbf_ref[...] = w_ref[...].astype(BF16)

    def run(kind):
        for r in range(x_ref.shape[0] // MM_SUB):
            rows = slice(r * MM_SUB, (r + 1) * MM_SUB)
            acc = jnp.dot(x_ref[rows, :], wbf_ref[...], preferred_element_type=F32)
            _epilogue(kind, acc, rows, gain_ref, cos_ref, sin_ref, o_ref)

    if len(kinds) == 1:
        run(kinds[0][2])
    else:
        for lo, hi, kind in kinds:
            pl.when((n >= lo) & (n < hi))(functools.partial(run, kind))


def _mm_call(x, w, gain, cos, sin, kinds, *, tm, tn, name):
    m, k = x.shape
    n = w.shape[1]
    return pl.pallas_call(
        functools.partial(_mm_kernel, kinds=kinds),
        grid=(n // tn, m // tm),
        in_specs=[pl.BlockSpec((tm, k), lambda j, i: (i, 0)),
                  pl.BlockSpec((k, tn), lambda j, i: (0, j)),
                  pl.BlockSpec((1, tn), lambda j, i: (0, j)),
                  pl.BlockSpec((tm, LANES), lambda j, i: (i, 0)),
                  pl.BlockSpec((tm, LANES), lambda j, i: (i, 0))],
        out_specs=pl.BlockSpec((tm, tn), lambda j, i: (i, j)),
        out_shape=jax.ShapeDtypeStruct((m, n), BF16),
        scratch_shapes=[pltpu.VMEM((k, tn), BF16)],
        compiler_params=_cparams(2, VMEM_LIMIT),
        name=name,
    )(x, w, gain, cos, sin)


def _attn_kernel(*refs, n_maps, dk, tq, nq, lam_init):
    if n_maps == 2:
        (q_ref, k_ref, v_ref, posq_ref, posk_ref, lq1_ref, lk1_ref, lq2_ref, lk2_ref, subg_ref,
         o_ref) = refs
        lam = (jnp.exp(jnp.sum(lq1_ref[...] * lk1_ref[...], axis=-1, keepdims=True))
               - jnp.exp(jnp.sum(lq2_ref[...] * lk2_ref[...], axis=-1, keepdims=True))
               + lam_init)
    else:
        q_ref, k_ref, v_ref, posq_ref, posk_ref, o_ref = refs

    for i in range(nq):
        rows = slice(i * tq, (i + 1) * tq)
        n_past = i * tq
        mask = posk_ref[:, rows] <= posq_ref[rows, :]
        probs = []
        for mi in range(n_maps):
            cols = slice(mi * dk, (mi + 1) * dk)
            qm = q_ref[rows, cols]
            s_diag = jnp.where(mask, _dot_nt(qm, k_ref[rows, cols]), -jnp.inf)
            mx = jnp.max(s_diag, axis=-1, keepdims=True)
            if n_past:
                s_past = _dot_nt(qm, k_ref[0:n_past, cols])
                mx = jnp.maximum(mx, jnp.max(s_past, axis=-1, keepdims=True))
            p_diag = jnp.exp(s_diag - mx)
            den = jnp.sum(p_diag, axis=-1, keepdims=True)
            p_past = None
            if n_past:
                p_past = jnp.exp(s_past - mx)
                den = den + jnp.sum(p_past, axis=-1, keepdims=True)
            probs.append((p_diag, p_past, 1.0 / den))

        def pv(w_diag, w_past):
            o = jnp.dot(w_diag.astype(BF16), v_ref[rows, :], preferred_element_type=F32)
            if n_past:
                o = o + jnp.dot(w_past.astype(BF16), v_ref[0:n_past, :],
                                preferred_element_type=F32)
            return o

        if n_maps == 2:
            (d1, p1, r1), (d2, p2, r2) = probs
            c2 = lam * r2
            o = pv(d1 * r1 - d2 * c2, (p1 * r1 - p2 * c2) if n_past else None)
            o = _rms(o) * subg_ref[...] * (1.0 - lam_init)
        else:
            (d1, p1, r1), = probs
            o = pv(d1, p1) * r1
        o_ref[rows, :] = o.astype(o_ref.dtype)


def _attn_call(q, k, v, posq, posk, extra, *, batch, seq, heads, n_maps, dk, dv,
               qcol, kcol, vcol, lam_init, name):
    tq = ATTN_TQ
    in_specs = [
        pl.BlockSpec((seq, n_maps * dk), lambda b, h: (b, qcol + h)),
        pl.BlockSpec((seq, n_maps * dk), lambda b, h: (b, kcol + h)),
        pl.BlockSpec((seq, dv), lambda b, h: (b, vcol + h)),
        pl.BlockSpec((seq, 1), lambda b, h: (b, 0)),
        pl.BlockSpec((1, seq), lambda b, h: (0, b)),
    ] + [pl.BlockSpec(e.shape, lambda b, h: (0, 0)) for e in extra]
    return pl.pallas_call(
        functools.partial(_attn_kernel, n_maps=n_maps, dk=dk, tq=tq, nq=seq // tq,
                          lam_init=lam_init),
        grid=(batch, heads),
        in_specs=in_specs,
        out_specs=pl.BlockSpec((seq, dv), lambda b, h: (b, h)),
        out_shape=jax.ShapeDtypeStruct((batch * seq, heads * dv), BF16),
        compiler_params=_cparams(2, VMEM_LIMIT),
        name=name,
    )(q, k, v, posq, posk, *extra)


def _cross_kernel(q_ref, k_ref, v_ref, o_ref):
    for h in range(CROSS_HEADS):
        sl = slice(h * CROSS_HEAD_DIM, (h + 1) * CROSS_HEAD_DIM)
        s = _dot_nt(q_ref[:, sl], k_ref[:, sl])
        p = jnp.exp(s - jnp.max(s, axis=-1, keepdims=True))
        inv = 1.0 / jnp.sum(p, axis=-1, keepdims=True)
        o = jnp.dot(p.astype(BF16), v_ref[:, sl], preferred_element_type=F32)
        o_ref[:, sl] = (o * inv).astype(o_ref.dtype)


def _cross_call(q, memkv, *, batch, seq, qcol, tq=512):
    nq = seq // tq
    return pl.pallas_call(
        _cross_kernel,
        grid=(batch, nq),
        in_specs=[pl.BlockSpec((tq, CROSS_WIDTH), lambda b, i: (b * nq + i, qcol)),
                  pl.BlockSpec((N_MEM, CROSS_WIDTH), lambda b, i: (b, 0)),
                  pl.BlockSpec((N_MEM, CROSS_WIDTH), lambda b, i: (b, 1))],
        out_specs=pl.BlockSpec((tq, CROSS_WIDTH), lambda b, i: (b * nq + i, 0)),
        out_shape=jax.ShapeDtypeStruct((batch * seq, CROSS_WIDTH), BF16),
        compiler_params=_cparams(2),
        name="cross_attn",
    )(q, memkv, memkv)


def _outproj_kernel(os_ref, oc_ref, w_ref, x_ref, o_ref, wbf_ref):
    @pl.when(pl.program_id(1) == 0)
    def _():
        wbf_ref[...] = w_ref[...].astype(BF16)

    acc = jnp.dot(os_ref[...], wbf_ref[:SELF_WIDTH, :], preferred_element_type=F32)
    acc = acc + jnp.dot(oc_ref[...], wbf_ref[SELF_WIDTH:, :], preferred_element_type=F32)
    o_ref[...] = x_ref[...] + acc


def _outproj_call(o_self, o_cross, w_out, x, *, tm=1024, tn=512):
    m = x.shape[0]
    return pl.pallas_call(
        _outproj_kernel,
        grid=(D_MODEL // tn, m // tm),
        in_specs=[pl.BlockSpec((tm, SELF_WIDTH), lambda j, i: (i, 0)),
                  pl.BlockSpec((tm, CROSS_WIDTH), lambda j, i: (i, 0)),
                  pl.BlockSpec((D_MODEL, tn), lambda j, i: (0, j)),
                  pl.BlockSpec((tm, tn), lambda j, i: (i, j))],
        out_specs=pl.BlockSpec((tm, tn), lambda j, i: (i, j)),
        out_shape=jax.ShapeDtypeStruct((m, D_MODEL), F32),
        scratch_shapes=[pltpu.VMEM((D_MODEL, tn), BF16)],
        compiler_params=_cparams(2, VMEM_LIMIT),
        name="out_proj",
    )(o_self, o_cross, w_out, x)


def _mla_in_kernel(h_ref, w_ref, gcq_ref, gckv_ref, gqm_ref, cq_ref, ckv_ref, qm_ref, kpe_ref):
    a = MLA_Q_RANK
    b = a + MLA_KV_RANK
    c = b + CROSS_WIDTH
    for r in range(h_ref.shape[0] // MM_SUB):
        rows = slice(r * MM_SUB, (r + 1) * MM_SUB)
        acc = jnp.dot(h_ref[rows, :], w_ref[...], preferred_element_type=F32)
        cq_ref[rows, :] = (_rms(acc[:, :a]) * gcq_ref[...]).astype(cq_ref.dtype)
        ckv_ref[rows, :] = (_rms(acc[:, a:b]) * gckv_ref[...]).astype(ckv_ref.dtype)
        for j in range(CROSS_HEADS):
            sl = slice(j * CROSS_HEAD_DIM, (j + 1) * CROSS_HEAD_DIM)
            qm_ref[rows, sl] = (_rms(acc[:, b + j * CROSS_HEAD_DIM:b + (j + 1) * CROSS_HEAD_DIM])
                                * gqm_ref[:, sl]).astype(qm_ref.dtype)
        kpe_ref[rows, :] = acc[:, c:]


def _mla_in_call(h, wcat, gcq, gckv, gqm, *, tm=512):
    m = h.shape[0]
    ncat = wcat.shape[1]
    row = lambda w: pl.BlockSpec((tm, w), lambda i: (i, 0))
    full = lambda r, w: pl.BlockSpec((r, w), lambda i: (0, 0))
    w_spec = pl.BlockSpec((D_MODEL, ncat), lambda i: (0, 0), pipeline_mode=pl.Buffered(1))
    return pl.pallas_call(
        _mla_in_kernel,
        grid=(m // tm,),
        in_specs=[row(D_MODEL), w_spec, full(1, MLA_Q_RANK), full(1, MLA_KV_RANK),
                  full(1, CROSS_WIDTH)],
        out_specs=[row(MLA_Q_RANK), row(MLA_KV_RANK), row(CROSS_WIDTH), row(LANES)],
        out_shape=[jax.ShapeDtypeStruct((m, MLA_Q_RANK), BF16),
                   jax.ShapeDtypeStruct((m, MLA_KV_RANK), BF16),
                   jax.ShapeDtypeStruct((m, CROSS_WIDTH), BF16),
                   jax.ShapeDtypeStruct((m, LANES), F32)],
        compiler_params=_cparams(1, VMEM_LIMIT),
        name="mla_in_proj",
    )(h, wcat, gcq, gckv, gqm)


MLA_GROUP = 4


def _rope_pe(x, c, s):
    lane = lax.broadcasted_iota(I32, x.shape, 1)
    partner = jnp.where((lane & 63) < 32, pltpu.roll(x, 96, axis=1), pltpu.roll(x, 32, axis=1))
    return x * c + partner * s


def _mla_q_kernel(cq_ref, wn_ref, wp_ref, wr_ref, gn_ref, gp_ref, gr_ref, cos_ref, sin_ref, o_ref):
    x = cq_ref[...]
    an = jnp.dot(x, wn_ref[...], preferred_element_type=F32)
    ap = jnp.dot(x, wp_ref[...], preferred_element_type=F32)
    ar = jnp.dot(x, wr_ref[...], preferred_element_type=F32)
    gc = cos_ref[...] * gp_ref[...]
    gs = sin_ref[...] * gr_ref[...]
    for j in range(MLA_GROUP):
        sl = slice(j * 128, (j + 1) * 128)
        nj = an[:, sl]
        pj = ap[:, sl]
        ss = jnp.sum(nj * nj + pj * pj, axis=-1, keepdims=True)
        rs = lax.rsqrt(ss * (1.0 / MLA_QK_DIM) + NORM_EPS)
        o_ref[:, j * 256:j * 256 + 128] = (nj * rs * gn_ref[...]).astype(o_ref.dtype)
        o_ref[:, j * 256 + 128:(j + 1) * 256] = ((pj * gc + ar[:, sl] * gs) * rs).astype(o_ref.dtype)


def _mla_q_call(cq, w_nope, w_pe, w_rot, gn, gp, gr, cos, sin, *, tm=1024):
    m = cq.shape[0]
    g = MLA_HEADS // MLA_GROUP
    tm = min(tm, m)
    wspec = pl.BlockSpec((MLA_Q_RANK, MLA_GROUP * 128), lambda j, i: (0, j))
    gspec = pl.BlockSpec((1, LANES), lambda j, i: (0, 0))
    tspec = pl.BlockSpec((tm, LANES), lambda j, i: (i, 0))
    return pl.pallas_call(
        _mla_q_kernel,
        grid=(g, m // tm),
        in_specs=[pl.BlockSpec((tm, MLA_Q_RANK), lambda j, i: (i, 0)),
                  wspec, wspec, wspec, gspec, gspec, gspec, tspec, tspec],
        out_specs=pl.BlockSpec((tm, MLA_GROUP * MLA_PAD_DIM), lambda j, i: (i, j)),
        out_shape=jax.ShapeDtypeStruct((m, MLA_HEADS * MLA_PAD_DIM), BF16),
        compiler_params=_cparams(2),
        name="mla_q_up",
    )(cq, w_nope, w_pe, w_rot, gn, gp, gr, cos, sin)


def _mla_kv_kernel(ckv_ref, w_ref, kpe_ref, gn_ref, gp_ref, cos_ref, sin_ref, k_ref, v_ref):
    acc = jnp.dot(ckv_ref[...], w_ref[...], preferred_element_type=F32)
    kpe = kpe_ref[...]
    ss_pe = jnp.sum(kpe * kpe, axis=-1, keepdims=True)
    pe = _rope_pe(kpe * gp_ref[...], cos_ref[...], sin_ref[...])
    for j in range(MLA_GROUP):
        kn = acc[:, j * 256:j * 256 + 128]
        ss = jnp.sum(kn * kn, axis=-1, keepdims=True) + ss_pe
        rs = lax.rsqrt(ss * (1.0 / MLA_QK_DIM) + NORM_EPS)
        k_ref[:, j * 256:j * 256 + 128] = (kn * rs * gn_ref[...]).astype(k_ref.dtype)
        k_ref[:, j * 256 + 128:(j + 1) * 256] = (pe * rs).astype(k_ref.dtype)
        v_ref[:, j * 128:(j + 1) * 128] = acc[:, j * 256 + 128:(j + 1) * 256].astype(v_ref.dtype)


def _mla_kv_call(ckv, w_ukv, kpe, gn, gp, cos, sin, *, tm=1024):
    m = ckv.shape[0]
    g = MLA_HEADS // MLA_GROUP
    tm = min(tm, m)
    return pl.pallas_call(
        _mla_kv_kernel,
        grid=(g, m // tm),
        in_specs=[pl.BlockSpec((tm, MLA_KV_RANK), lambda j, i: (i, 0)),
                  pl.BlockSpec((MLA_KV_RANK, MLA_GROUP * 256), lambda j, i: (0, j)),
                  pl.BlockSpec((tm, LANES), lambda j, i: (i, 0)),
                  pl.BlockSpec((1, LANES), lambda j, i: (0, 0)),
                  pl.BlockSpec((1, LANES), lambda j, i: (0, 0)),
                  pl.BlockSpec((tm, LANES), lambda j, i: (i, 0)),
                  pl.BlockSpec((tm, LANES), lambda j, i: (i, 0))],
        out_specs=[pl.BlockSpec((tm, MLA_GROUP * MLA_PAD_DIM), lambda j, i: (i, j)),
                   pl.BlockSpec((tm, MLA_GROUP * MLA_V_DIM), lambda j, i: (i, j))],
        out_shape=[jax.ShapeDtypeStruct((m, MLA_HEADS * MLA_PAD_DIM), BF16),
                   jax.ShapeDtypeStruct((m, MLA_HEADS * MLA_V_DIM), BF16)],
        compiler_params=_cparams(2),
        name="mla_kv_up",
    )(ckv, w_ukv, kpe, gn, gp, cos, sin)


def _route_kernel(x_ref, g_ref, w_ref, b_ref, xn_ref, ids_ref, wts_ref):
    xn = _rms(x_ref[...]) * g_ref[...]
    xn_ref[...] = xn
    x_hi = xn.astype(BF16)
    x_lo = (xn - x_hi.astype(F32)).astype(BF16)
    w = w_ref[...]
    w_hi = w.astype(BF16)
    w_lo = (w - w_hi.astype(F32)).astype(BF16)
    lg = _dot_nt(w_hi, x_hi) + _dot_nt(w_hi, x_lo) + _dot_nt(w_lo, x_hi) + b_ref[...]
    tm = lg.shape[1]
    iota = lax.broadcasted_iota(I32, (EXPERTS_PER_GROUP, tm), 0)

    def first_argmax(v):
        mx = jnp.max(v, axis=0, keepdims=True)
        idx = jnp.min(jnp.where(v == mx, iota, EXPERTS_PER_GROUP), axis=0, keepdims=True)
        return mx, idx

    gl = lg[0:N_GROUPS, :]
    gmax, gsel = first_argmax(gl)
    g_gate = 1.0 / jnp.sum(jnp.exp(gl - gmax), axis=0, keepdims=True)
    el = jnp.zeros((EXPERTS_PER_GROUP, tm), F32)
    for g in range(N_GROUPS):
        lo = N_GROUPS + g * EXPERTS_PER_GROUP
        el = jnp.where(gsel == g, lg[lo:lo + EXPERTS_PER_GROUP, :], el)
    v1, i1 = first_argmax(el)
    v2, i2 = first_argmax(jnp.where(iota == i1, -jnp.inf, el))
    e = jnp.exp(v2 - v1)
    w1 = g_gate / (1.0 + e)
    w2 = g_gate * e / (1.0 + e)
    ids_ref[...] = jnp.concatenate([gsel * EXPERTS_PER_GROUP + i1,
                                    gsel * EXPERTS_PER_GROUP + i2], axis=0)
    wts_ref[...] = jnp.concatenate([w1, w2], axis=0)


def _route_call(x, g, w_t, b_col, *, tm=256):
    m = x.shape[0]
    return pl.pallas_call(
        _route_kernel,
        grid=(m // tm,),
        in_specs=[pl.BlockSpec((tm, D_MODEL), lambda i: (i, 0)),
                  pl.BlockSpec((1, D_MODEL), lambda i: (0, 0)),
                  pl.BlockSpec((LANES, D_MODEL), lambda i: (0, 0)),
                  pl.BlockSpec((LANES, 1), lambda i: (0, 0))],
        out_specs=[pl.BlockSpec((tm, D_MODEL), lambda i: (i, 0)),
                   pl.BlockSpec((TOP_K, tm), lambda i: (0, i)),
                   pl.BlockSpec((TOP_K, tm), lambda i: (0, i))],
        out_shape=[jax.ShapeDtypeStruct((m, D_MODEL), F32),
                   jax.ShapeDtypeStruct((TOP_K, m), I32),
                   jax.ShapeDtypeStruct((TOP_K, m), F32)],
        compiler_params=_cparams(1, VMEM_LIMIT),
        name="moe_route",
    )(x, g, w_t, b_col)


def _sort_kernel(ids_ref, pos_ref, te_ref, nu_ref, rs_ref, rc_ref, cnt_ref, start_ref, *, ts):
    n_tok = ids_ref.shape[1]
    nck = n_tok // ts
    iota_e = lax.broadcasted_iota(I32, (N_EXPERTS, ts), 0)

    def chunk(c):
        return pl.ds(pl.multiple_of(c * ts, ts), ts)

    def count_body(c, acc):
        for k in range(TOP_K):
            oh = jnp.where(iota_e == ids_ref[pl.ds(k, 1), chunk(c)], 1.0, 0.0)
            acc = acc + jnp.sum(oh, axis=1, keepdims=True)
        return acc

    counts = lax.fori_loop(0, nck, count_body, jnp.zeros((N_EXPERTS, 1), F32))
    er = lax.broadcasted_iota(I32, (N_EXPERTS, N_EXPERTS), 0)
    ec = lax.broadcasted_iota(I32, (N_EXPERTS, N_EXPERTS), 1)
    strict_lower = jnp.where(ec < er, 1.0, 0.0).astype(BF16)

    def excl_cumsum(v):
        vb = jnp.broadcast_to(v, (N_EXPERTS, LANES)).astype(BF16)
        return jnp.dot(strict_lower, vb, preferred_element_type=F32)

    padded = jnp.floor((counts + (ROW_ALIGN - 1)) * (1.0 / ROW_ALIGN)) * ROW_ALIGN
    hi = jnp.floor(padded * (1.0 / LANES))
    start = excl_cumsum(hi) * LANES + excl_cumsum(padded - hi * LANES)
    chunks = jnp.floor((counts + (MOE_TM - 0.5)) * (1.0 / MOE_TM))
    first = excl_cumsum(chunks)
    ends = first + chunks
    total = jnp.max(ends, axis=0, keepdims=True)
    chunk_i = lax.broadcasted_iota(I32, (N_EXPERTS, LANES), 1).astype(F32)
    chunk_c = jnp.minimum(chunk_i, total - 1.0)
    te = jnp.sum(jnp.where(ends <= chunk_c, 1.0, 0.0), axis=0, keepdims=True)
    mine = lax.broadcasted_iota(I32, (N_EXPERTS, LANES), 0).astype(F32) == te
    rs = jnp.sum(jnp.where(mine, start + (chunk_c - first) * MOE_TM, 0.0), axis=0, keepdims=True)
    te_ref[...] = te.astype(I32)
    nu_ref[...] = total.astype(I32)
    rs_ref[...] = rs.astype(I32)
    left = jnp.sum(jnp.where(mine, padded - (chunk_c - first) * MOE_TM, 0.0), axis=0, keepdims=True)
    rc_ref[...] = jnp.clip(left, 0.0, float(MOE_TM)).astype(I32)
    cnt_ref[...] = counts.astype(I32)
    start_ref[...] = start[:, 0:1].astype(I32)

    row_base = start[:, 0:1]
    ur = lax.broadcasted_iota(I32, (ts, ts), 0)
    uc = lax.broadcasted_iota(I32, (ts, ts), 1)
    upper = jnp.where(ur <= uc, 1.0, 0.0).astype(BF16)

    def pos_body(k):
        def body(c, carry):
            hit = iota_e == ids_ref[pl.ds(k, 1), chunk(c)]
            incl = jnp.dot(jnp.where(hit, 1.0, 0.0).astype(BF16), upper,
                           preferred_element_type=F32)
            val = row_base + carry + incl - 1.0
            p = jnp.sum(jnp.where(hit, val, 0.0), axis=0, keepdims=True)
            pos_ref[pl.ds(k, 1), chunk(c)] = p.astype(I32)
            return carry + incl[:, ts - 1:ts]
        return body

    carry = jnp.zeros((N_EXPERTS, 1), F32)
    for k in range(TOP_K):
        carry = lax.fori_loop(0, nck, pos_body(k), carry)


def _sort_call(ids, *, ts=512):
    n_tok = ids.shape[1]
    ts = min(ts, n_tok)
    return pl.pallas_call(
        functools.partial(_sort_kernel, ts=ts),
        out_shape=[jax.ShapeDtypeStruct((TOP_K, n_tok), I32),
                   jax.ShapeDtypeStruct((1, LANES), I32),
                   jax.ShapeDtypeStruct((1, LANES), I32),
                   jax.ShapeDtypeStruct((1, LANES), I32),
                   jax.ShapeDtypeStruct((1, LANES), I32),
                   jax.ShapeDtypeStruct((N_EXPERTS, 1), I32),
                   jax.ShapeDtypeStruct((N_EXPERTS, 1), I32)],
        name="moe_sort",
    )(ids)


def _sorted_rows(n_tok):
    return TOP_K * n_tok + N_EXPERTS * ROW_ALIGN


def _moe_chunks(n_tok):
    return -(-(TOP_K * n_tok) // MOE_TM) + N_EXPERTS


def _dispatch_kernel(pos_ref, cnt_ref, start_ref, xp_ref, xs_hbm, stage, zbuf, zsem, rsem, *, n_tok):
    i = pl.program_id(0)
    nb = pl.num_programs(0)
    tm = xp_ref.shape[0]
    slot = i % 2
    n_slots = TOP_K * n_tok
    n_rows = xs_hbm.shape[0]

    @pl.when(i == 0)
    def _():
        zbuf[...] = jnp.zeros(zbuf.shape, zbuf.dtype)

        def tail_copy(j):
            return pltpu.make_async_copy(
                zbuf, xs_hbm.at[pl.ds(n_slots + j * DISPATCH_ZROWS, DISPATCH_ZROWS)], zsem.at[0])

        n_tail = (n_rows - n_slots) // DISPATCH_ZROWS
        for j in range(n_tail):
            tail_copy(j).start()
        for j in range(n_tail):
            tail_copy(j).wait()

        def zero_row(dst):
            return pltpu.make_async_copy(zbuf.at[pl.ds(0, 1)], xs_hbm.at[pl.ds(dst, 1)], zsem.at[0])

        def per_expert(e, c):
            n = cnt_ref[e]
            first = start_ref[e] + n
            n_pad = (-n) & (ROW_ALIGN - 1)

            def put(r, c2):
                zero_row(first + r).start()
                return c2
            lax.fori_loop(0, n_pad, put, 0)

            def done(r, c2):
                zero_row(0).wait()
                return c2
            lax.fori_loop(0, n_pad, done, 0)
            return c

        lax.fori_loop(0, N_EXPERTS, per_expert, 0)

    def wait_stage(sl):
        for k in range(TOP_K):
            pltpu.make_async_copy(stage.at[sl], xs_hbm.at[pl.ds(0, tm)], rsem.at[sl]).wait()

    @pl.when(i >= 2)
    def _():
        wait_stage(slot)

    stage[slot] = xp_ref[...]

    def group(g, c):
        for u in range(DMA_GROUP):
            r = g * DMA_GROUP + u
            for k in range(TOP_K):
                dst = pos_ref[k * n_tok + i * tm + r]
                pltpu.make_async_copy(stage.at[slot, pl.ds(r, 1)], xs_hbm.at[pl.ds(dst, 1)],
                                      rsem.at[slot]).start()
        return c
    lax.fori_loop(0, tm // DMA_GROUP, group, 0)

    @pl.when(i == nb - 1)
    def _():
        wait_stage(1 - slot)
        wait_stage(slot)


def _dispatch_call(pos_flat, cnt, start, xn, *, tm=256):
    n_tok = xn.shape[0]
    assert n_tok // tm >= 2
    assert (N_EXPERTS * ROW_ALIGN) % DISPATCH_ZROWS == 0
    grid_spec = pltpu.PrefetchScalarGridSpec(
        num_scalar_prefetch=3,
        grid=(n_tok // tm,),
        in_specs=[pl.BlockSpec((tm, D_MODEL), lambda i, p, c, s: (i, 0))],
        out_specs=pl.BlockSpec(memory_space=pl.ANY),
        scratch_shapes=[pltpu.VMEM((2, tm, D_MODEL), F32),
                        pltpu.VMEM((DISPATCH_ZROWS, D_MODEL), F32),
                        pltpu.SemaphoreType.DMA((1,)),
                        pltpu.SemaphoreType.DMA((2,))],
    )
    return pl.pallas_call(
        functools.partial(_dispatch_kernel, n_tok=n_tok),
        grid_spec=grid_spec,
        out_shape=jax.ShapeDtypeStruct((_sorted_rows(n_tok), D_MODEL), F32),
        compiler_params=_cparams(1, VMEM_LIMIT),
        name="moe_dispatch",
    )(pos_flat, cnt, start, xn)


def _moe_kernel(te_ref, nu_ref, rs_ref, rc_ref, xs_hbm, wg_ref, wu_ref, wd_ref, y_hbm,
                xbuf, acc, gu, wgu, gsem, ssem, *, n_slots):
    i = pl.program_id(0)
    k = pl.program_id(1)
    nk = pl.num_programs(1)
    n_used = nu_ref[0]
    valid = i < n_used
    slot = i % 2

    def chunk_rows(c):
        return (pl.ds(pl.multiple_of(rs_ref[c], ROW_ALIGN), pl.multiple_of(rc_ref[c], ROW_ALIGN)),
                pl.ds(0, pl.multiple_of(rc_ref[c], ROW_ALIGN)))

    def x_copy(c, sl):
        src, dst = chunk_rows(c)
        return pltpu.make_async_copy(xs_hbm.at[src], xbuf.at[sl, dst], gsem.at[sl])

    def y_copy(c):
        dst, src = chunk_rows(c)
        return pltpu.make_async_copy(acc.at[src], y_hbm.at[dst], ssem.at[0])

    @pl.when((i == 0) & (k == 0))
    def _():
        xbuf[...] = jnp.zeros(xbuf.shape, xbuf.dtype)
        x_copy(0, 0).start()
        acc[...] = jnp.zeros(acc.shape, acc.dtype)
        n_rows = y_hbm.shape[0]
        tails = [(off, min(MOE_TM, n_rows - off)) for off in range(n_slots, n_rows, MOE_TM)]
        for off, size in tails:
            pltpu.make_async_copy(acc.at[pl.ds(0, size)], y_hbm.at[pl.ds(off, size)],
                                  ssem.at[0]).start()
        for off, size in tails:
            pltpu.make_async_copy(acc.at[pl.ds(0, size)], y_hbm.at[pl.ds(off, size)],
                                  ssem.at[0]).wait()

    @pl.when(valid & (k == 0))
    def _():
        x_copy(i, slot).wait()

        @pl.when(i + 1 < n_used)
        def _():
            x_copy(i + 1, 1 - slot).start()

    @pl.when(valid)
    def _():
        wgu[:, :D_EXPERT] = wg_ref[0].astype(BF16)
        wgu[:, D_EXPERT:] = wu_ref[0].astype(BF16)
        xs = xbuf[slot, :, pl.ds(pl.multiple_of(k * MOE_TK, MOE_TK), MOE_TK)].astype(BF16)
        part = jnp.dot(xs, wgu[...], preferred_element_type=F32)

        @pl.when(k == 0)
        def _():
            gu[...] = part

        @pl.when(k > 0)
        def _():
            gu[...] += part

        @pl.when(k == nk - 1)
        def _():
            a = gu[:, :D_EXPERT]
            hid = (a * jax.nn.sigmoid(a)) * gu[:, D_EXPERT:]
            out = jnp.dot(hid.astype(BF16), wd_ref[0].astype(BF16), preferred_element_type=F32)

            @pl.when(i > 0)
            def _():
                y_copy(i - 1).wait()
            acc[...] = out
            y_copy(i).start()

            @pl.when(i == n_used - 1)
            def _():
                y_copy(i).wait()


def _moe_call(te, nu, rs, rc, xs, w_gate, w_up, w_down, *, n_tok):
    nk = D_MODEL // MOE_TK
    last_k = nk - 1

    def w_in_map(i, k, te_r, nu_r, rs_r, rc_r):
        return te_r[i], jnp.where(i < nu_r[0], k, last_k), 0

    def w_out_map(i, k, te_r, nu_r, rs_r, rc_r):
        return te_r[i], 0, 0

    grid_spec = pltpu.PrefetchScalarGridSpec(
        num_scalar_prefetch=4,
        grid=(_moe_chunks(n_tok), nk),
        in_specs=[
            pl.BlockSpec(memory_space=pl.ANY),
            pl.BlockSpec((1, MOE_TK, D_EXPERT), w_in_map),
            pl.BlockSpec((1, MOE_TK, D_EXPERT), w_in_map),
            pl.BlockSpec((1, D_EXPERT, D_MODEL), w_out_map),
        ],
        out_specs=pl.BlockSpec(memory_space=pl.ANY),
        scratch_shapes=[pltpu.VMEM((2, MOE_TM, D_MODEL), F32),
                        pltpu.VMEM((MOE_TM, D_MODEL), F32),
                        pltpu.VMEM((MOE_TM, 2 * D_EXPERT), F32),
                        pltpu.VMEM((MOE_TK, 2 * D_EXPERT), BF16),
                        pltpu.SemaphoreType.DMA((2,)),
                        pltpu.SemaphoreType.DMA((1,))],
    )
    return pl.pallas_call(
        functools.partial(_moe_kernel, n_slots=TOP_K * n_tok),
        grid_spec=grid_spec,
        out_shape=jax.ShapeDtypeStruct(xs.shape, F32),
        compiler_params=_cparams(2, VMEM_LIMIT),
        name="moe_experts",
    )(te, nu, rs, rc, xs, w_gate, w_up, w_down)


def _combine_kernel(pos_ref, x_ref, w_ref, g_ref, y_hbm, xo_ref, *rest, n_tok):
    *h_ref, ybuf, sem = rest
    i = pl.program_id(0)
    nb = pl.num_programs(0)
    tm = x_ref.shape[0]
    slot = i % 2

    def fetch(tile, sl):
        def group(g, c):
            for u in range(DMA_GROUP):
                r = g * DMA_GROUP + u
                for k in range(TOP_K):
                    src = pos_ref[k * n_tok + tile * tm + r]
                    pltpu.make_async_copy(y_hbm.at[pl.ds(src, 1)], ybuf.at[sl, k, pl.ds(r, 1)],
                                          sem.at[sl]).start()
            return c
        lax.fori_loop(0, tm // DMA_GROUP, group, 0)

    @pl.when(i == 0)
    def _():
        fetch(0, 0)

    @pl.when(i + 1 < nb)
    def _():
        fetch(i + 1, 1 - slot)

    for k in range(TOP_K):
        pltpu.make_async_copy(y_hbm.at[pl.ds(0, tm)], ybuf.at[slot, k], sem.at[slot]).wait()
    w = w_ref[...]
    x = x_ref[...] + w[:, 0:1] * ybuf[slot, 0] + w[:, 1:2] * ybuf[slot, 1]
    xo_ref[...] = x
    if h_ref:
        h_ref[0][...] = (_rms(x) * g_ref[...]).astype(BF16)


def _combine_call(pos_flat, x, y, wts_t, g_next, *, tm=256):
    m, d = x.shape
    with_norm = g_next is not None
    g = g_next.reshape(1, d) if with_norm else jnp.ones((1, d), F32)
    row = pl.BlockSpec((tm, d), lambda i, pos_r: (i, 0))
    out_specs = [row, row] if with_norm else [row]
    out_shape = [jax.ShapeDtypeStruct((m, d), F32)]
    if with_norm:
        out_shape.append(jax.ShapeDtypeStruct((m, d), BF16))
    grid_spec = pltpu.PrefetchScalarGridSpec(
        num_scalar_prefetch=1,
        grid=(m // tm,),
        in_specs=[row,
                  pl.BlockSpec((tm, TOP_K), lambda i, pos_r: (i, 0)),
                  pl.BlockSpec((1, d), lambda i, pos_r: (0, 0)),
                  pl.BlockSpec(memory_space=pl.ANY)],
        out_specs=out_specs,
        scratch_shapes=[pltpu.VMEM((2, TOP_K, tm, d), F32),
                        pltpu.SemaphoreType.DMA((2,))],
    )
    return pl.pallas_call(
        functools.partial(_combine_kernel, n_tok=m),
        grid_spec=grid_spec,
        out_shape=out_shape,
        compiler_params=_cparams(1, VMEM_LIMIT),
        name="moe_combine",
    )(pos_flat, x, wts_t, g, y)


def _moe_layer(x, norm_g, w_group, b_group, w_router, b_router, w_gate, w_up, w_down, g_next):
    n_tok = x.shape[0]
    pad_rows = LANES - N_GROUPS - N_EXPERTS
    w_t = jnp.concatenate([w_group.T, w_router.T, jnp.zeros((pad_rows, D_MODEL), F32)], axis=0)
    b_col = jnp.concatenate([b_group, b_router, jnp.zeros((pad_rows,), F32)]).reshape(LANES, 1)
    xn, ids, wts = _route_call(x, norm_g.reshape(1, D_MODEL), w_t, b_col)
    pos, te, nu, rs, rc, cnt, start = _sort_call(ids)
    pos_flat = pos.reshape(-1)
    xs = _dispatch_call(pos_flat, cnt.reshape(-1), start.reshape(-1), xn)
    y = _moe_call(te.reshape(-1), nu.reshape(-1)[:1], rs.reshape(-1), rc.reshape(-1), xs,
                  w_gate.reshape(N_EXPERTS, D_MODEL, D_EXPERT),
                  w_up.reshape(N_EXPERTS, D_MODEL, D_EXPERT),
                  w_down.reshape(N_EXPERTS, D_EXPERT, D_MODEL), n_tok=n_tok)
    return _combine_call(pos_flat, x, y, wts.T, g_next)


def _rope_tables(pos_flat, dim):
    inv_freq = ROPE_THETA ** (-jnp.arange(0, dim, 2, dtype=F32) / dim)
    ang = pos_flat.astype(F32)[:, None] * inv_freq
    return jnp.cos(ang), jnp.sin(ang)


def kernel(x, mem, positions, mem_norm_g, w_mem_kv, mem_k_norm_g, l0_attn_norm_g, l0_w_in, l0_q_norm_g, l0_k_norm_g, l0_lambda_q1, l0_lambda_k1, l0_lambda_q2, l0_lambda_k2, l0_subln_g, l0_cross_q_norm_g, l0_w_out, l0_ffn_norm_g, l0_w_group, l0_b_group, l0_w_router, l0_b_router, l0_w_gate, l0_w_up, l0_w_down, l1_attn_norm_g, l1_w_in, l1_cq_norm_g, l1_ckv_norm_g, l1_w_uq, l1_w_ukv, l1_q_norm_g, l1_k_norm_g, l1_cross_q_norm_g, l1_w_out, l1_ffn_norm_g, l1_w_group, l1_b_group, l1_w_router, l1_b_router, l1_w_gate, l1_w_up, l1_w_down):
    batch, seq, d = x.shape
    n_tok = batch * seq
    xf = x.reshape(n_tok, d)
    pos_flat = positions.reshape(n_tok)
    posq = pos_flat.reshape(n_tok, 1)
    posk = pos_flat.reshape(1, n_tok)
    ones128 = jnp.ones((n_tok, LANES), F32)

    c64, s64 = _rope_tables(pos_flat, DIFF_HEAD_DIM)
    cos_full = jnp.concatenate([c64, c64], axis=1)
    sin_full = jnp.concatenate([-s64, s64], axis=1)
    c32, s32 = _rope_tables(pos_flat, MLA_ROPE_DIM)
    z32 = jnp.zeros_like(c32)
    cos_k = jnp.concatenate([c32, c32, z32, z32], axis=1)
    sin_k = jnp.concatenate([-s32, s32, z32, z32], axis=1)

    cross_scale = CROSS_HEAD_DIM ** -0.5

    memn = _norm_call(mem.reshape(batch * N_MEM, d), mem_norm_g)
    mem_gain = jnp.concatenate([jnp.tile(mem_k_norm_g, CROSS_HEADS), jnp.ones((CROSS_WIDTH,), F32)])
    k_tiles = CROSS_WIDTH // 512
    memkv = _mm_call(memn, w_mem_kv, mem_gain.reshape(1, -1), ones128[:batch * N_MEM],
                     ones128[:batch * N_MEM], ((0, k_tiles, "head256"), (k_tiles, 2 * k_tiles, "plain")),
                     tm=batch * N_MEM, tn=512, name="mem_kv")

    h0 = _norm_call(xf, l0_attn_norm_g)
    qk_tiles = SELF_WIDTH // 512
    gain0 = jnp.concatenate([
        jnp.tile(l0_q_norm_g, 2 * DIFF_HEADS) * (DIFF_HEAD_DIM ** -0.5),
        jnp.tile(l0_k_norm_g, 2 * DIFF_HEADS),
        jnp.ones((SELF_WIDTH,), F32),
        jnp.tile(l0_cross_q_norm_g, CROSS_HEADS) * cross_scale]).reshape(1, -1)
    proj0 = _mm_call(h0, l0_w_in, gain0, cos_full, sin_full,
                     ((0, 2 * qk_tiles, "head128_rope"), (2 * qk_tiles, 3 * qk_tiles, "plain"),
                      (3 * qk_tiles, 3 * qk_tiles + CROSS_WIDTH // 512, "head256")),
                     tm=1024, tn=512, name="l0_in_proj")
    lam_init = 0.8 - 0.6 * math.exp(-0.3 * 0)
    row = lambda v: v.reshape(1, -1)
    o_self = _attn_call(proj0, proj0, proj0, posq, posk,
                        (row(l0_lambda_q1), row(l0_lambda_k1), row(l0_lambda_q2),
                         row(l0_lambda_k2), row(l0_subln_g)),
                        batch=batch, seq=seq, heads=DIFF_HEADS, n_maps=2, dk=DIFF_HEAD_DIM,
                        dv=DIFF_V_DIM, qcol=0, kcol=DIFF_HEADS, vcol=2 * DIFF_HEADS,
                        lam_init=lam_init, name="diff_attn")
    o_cross = _cross_call(proj0, memkv, batch=batch, seq=seq, qcol=3 * SELF_WIDTH // CROSS_WIDTH)
    x1 = _outproj_call(o_self, o_cross, l0_w_out, xf)
    x2, h1 = _moe_layer(x1, l0_ffn_norm_g, l0_w_group, l0_b_group, l0_w_router, l0_b_router,
                        l0_w_gate, l0_w_up, l0_w_down, l1_attn_norm_g)

    a = MLA_Q_RANK
    b = a + MLA_KV_RANK
    c = b + MLA_ROPE_DIM
    wcat = jnp.concatenate([l1_w_in[:, :b], l1_w_in[:, c:], l1_w_in[:, b:c],
                            jnp.zeros((d, LANES - MLA_ROPE_DIM), F32)], axis=1).astype(BF16)
    cq, ckv, qm, kpe = _mla_in_call(
        h1, wcat, row(l1_cq_norm_g), row(l1_ckv_norm_g),
        row(jnp.tile(l1_cross_q_norm_g, CROSS_HEADS) * cross_scale))
    w_uq3 = l1_w_uq.reshape(MLA_Q_RANK, MLA_HEADS, MLA_QK_DIM)
    w_q_nope = w_uq3[:, :, :MLA_NOPE_DIM].reshape(MLA_Q_RANK, -1).astype(BF16)
    partner = (jnp.arange(MLA_ROPE_DIM) + MLA_ROPE_DIM // 2) % MLA_ROPE_DIM
    w_pe3 = w_uq3[:, :, MLA_NOPE_DIM:]
    pad3 = jnp.zeros_like(w_pe3)
    w_q_pe = jnp.concatenate([w_pe3, pad3], axis=2).reshape(MLA_Q_RANK, -1).astype(BF16)
    w_q_rot = jnp.concatenate([w_pe3[:, :, partner], pad3], axis=2).reshape(MLA_Q_RANK, -1).astype(BF16)
    q_scale = MLA_QK_DIM ** -0.5
    zeros64 = jnp.zeros((MLA_ROPE_DIM,), F32)
    g_q_pe = l1_q_norm_g[MLA_NOPE_DIM:] * q_scale
    q_pad = _mla_q_call(cq, w_q_nope, w_q_pe, w_q_rot,
                        row(l1_q_norm_g[:MLA_NOPE_DIM] * q_scale),
                        row(jnp.concatenate([g_q_pe, zeros64])),
                        row(jnp.concatenate([g_q_pe[partner], zeros64])), cos_k, sin_k)
    k_pad, v1 = _mla_kv_call(ckv, l1_w_ukv.astype(BF16), kpe,
                             row(l1_k_norm_g[:MLA_NOPE_DIM]),
                             row(jnp.concatenate([l1_k_norm_g[MLA_NOPE_DIM:], zeros64])),
                             cos_k, sin_k)
    o_self1 = _attn_call(q_pad, k_pad, v1, posq, posk, (),
                         batch=batch, seq=seq, heads=MLA_HEADS, n_maps=1, dk=MLA_PAD_DIM,
                         dv=MLA_V_DIM, qcol=0, kcol=0, vcol=0, lam_init=0.0, name="mla_attn")
    o_cross1 = _cross_call(qm, memkv, batch=batch, seq=seq, qcol=0)
    x3 = _outproj_call(o_self1, o_cross1, l1_w_out, x2)
    (x4,) = _moe_layer(x3, l1_ffn_norm_g, l1_w_group, l1_b_group, l1_w_router, l1_b_router,
                       l1_w_gate, l1_w_up, l1_w_down, None)
    return x4.reshape(batch, seq, d)
```

```python
import functools
import math

import jax
import jax.numpy as jnp
from jax import lax
from jax.experimental import pallas as pl
from jax.experimental.pallas import tpu as pltpu

F32 = jnp.float32
BF16 = jnp.bfloat16
I32 = jnp.int32

D_MODEL = 4096
N_MEM = 256
ROPE_THETA = 10000.0
NORM_EPS = 1e-6

SELF_WIDTH = 3 * D_MODEL // 4
CROSS_HEADS = 4
CROSS_HEAD_DIM = (D_MODEL // 4) // CROSS_HEADS
CROSS_WIDTH = CROSS_HEADS * CROSS_HEAD_DIM

DIFF_HEAD_DIM = 128
DIFF_V_DIM = 2 * DIFF_HEAD_DIM
DIFF_HEADS = SELF_WIDTH // DIFF_V_DIM

MLA_NOPE_DIM = 128
MLA_ROPE_DIM = 64
MLA_V_DIM = 128
MLA_HEADS = SELF_WIDTH // MLA_V_DIM
MLA_QK_DIM = MLA_NOPE_DIM + MLA_ROPE_DIM
MLA_Q_RANK = 3 * D_MODEL // 16
MLA_KV_RANK = D_MODEL // 8
MLA_PAD_DIM = 256

N_GROUPS = 8
EXPERTS_PER_GROUP = 8
N_EXPERTS = N_GROUPS * EXPERTS_PER_GROUP
D_EXPERT = 3 * D_MODEL // 32
TOP_K = 2

LANES = 128
MOE_TM = 320
MOE_TK = 2048
DMA_GROUP = 8
ROW_ALIGN = 8
ATTN_TQ = 256
VMEM_LIMIT = 56 * 1024 * 1024


def _cparams(n_axes, vmem=None):
    return pltpu.CompilerParams(dimension_semantics=("arbitrary",) * n_axes,
                                vmem_limit_bytes=vmem)


def _rms(x):
    return x * lax.rsqrt(jnp.mean(x * x, axis=-1, keepdims=True) + NORM_EPS)


def _dot_nt(a, b):
    return lax.dot_general(a, b, (((1,), (1,)), ((), ())), preferred_element_type=F32)


def _norm_kernel(x_ref, g_ref, o_ref):
    o_ref[...] = (_rms(x_ref[...]) * g_ref[...]).astype(o_ref.dtype)


def _norm_call(x, g, tm=256):
    m, d = x.shape
    return pl.pallas_call(
        _norm_kernel,
        grid=(m // tm,),
        in_specs=[pl.BlockSpec((tm, d), lambda i: (i, 0)),
                  pl.BlockSpec((1, d), lambda i: (0, 0))],
        out_specs=pl.BlockSpec((tm, d), lambda i: (i, 0)),
        out_shape=jax.ShapeDtypeStruct((m, d), BF16),
        compiler_params=_cparams(1),
        name="rmsnorm",
    )(x, g.reshape(1, d))


def _epilogue(kind, acc, rows, gain_ref, cos_ref, sin_ref, o_ref):
    tn = acc.shape[1]
    if kind == "plain":
        o_ref[rows, :] = acc.astype(o_ref.dtype)
    elif kind == "head128_rope":
        c = cos_ref[rows, :]
        s = sin_ref[rows, :]
        for j in range(tn // 128):
            sl = slice(j * 128, (j + 1) * 128)
            y = _rms(acc[:, sl]) * gain_ref[:, sl]
            y = y * c + pltpu.roll(y, 64, axis=1) * s
            o_ref[rows, sl] = y.astype(o_ref.dtype)
    elif kind == "head256":
        for j in range(tn // 256):
            sl = slice(j * 256, (j + 1) * 256)
            o_ref[rows, sl] = (_rms(acc[:, sl]) * gain_ref[:, sl]).astype(o_ref.dtype)
    else:
        raise ValueError(kind)


MM_SUB = 256


def _mm_kernel(x_ref, w_ref, gain_ref, cos_ref, sin_ref, o_ref, wbf_ref, *, kinds):
    n = pl.program_id(0)

    @pl.when(pl.program_id(1) == 0)
    def _():
        wbf_ref[...] = w_ref[...].astype(BF16)

    def run(kind):
        for r in range(x_ref.shape[0] // MM_SUB):
            rows = slice(r * MM_SUB, (r + 1) * MM_SUB)
            acc = jnp.dot(x_ref[rows, :], wbf_ref[...], preferred_element_type=F32)
            _epilogue(kind, acc, rows, gain_ref, cos_ref, sin_ref, o_ref)

    if len(kinds) == 1:
        run(kinds[0][2])
    else:
        for lo, hi, kind in kinds:
            pl.when((n >= lo) & (n < hi))(functools.partial(run, kind))


def _mm_call(x, w, gain, cos, sin, kinds, *, tm, tn, name):
    m, k = x.shape
    n = w.shape[1]
    return pl.pallas_call(
        functools.partial(_mm_kernel, kinds=kinds),
        grid=(n // tn, m // tm),
        in_specs=[pl.BlockSpec((tm, k), lambda j, i: (i, 0)),
                  pl.BlockSpec((k, tn), lambda j, i: (0, j)),
                  pl.BlockSpec((1, tn), lambda j, i: (0, j)),
                  pl.BlockSpec((tm, LANES), lambda j, i: (i, 0)),
                  pl.BlockSpec((tm, LANES), lambda j, i: (i, 0))],
        out_specs=pl.BlockSpec((tm, tn), lambda j, i: (i, j)),
        out_shape=jax.ShapeDtypeStruct((m, n), BF16),
        scratch_shapes=[pltpu.VMEM((k, tn), BF16)],
        compiler_params=_cparams(2, VMEM_LIMIT),
        name=name,
    )(x, w, gain, cos, sin)


def _attn_kernel(*refs, n_maps, dk, tq, nq, lam_init):
    if n_maps == 2:
        (q_ref, k_ref, v_ref, posq_ref, posk_ref, lq1_ref, lk1_ref, lq2_ref, lk2_ref, subg_ref,
         o_ref) = refs
        lam = (jnp.exp(jnp.sum(lq1_ref[...] * lk1_ref[...], axis=-1, keepdims=True))
               - jnp.exp(jnp.sum(lq2_ref[...] * lk2_ref[...], axis=-1, keepdims=True))
               + lam_init)
    else:
        q_ref, k_ref, v_ref, posq_ref, posk_ref, o_ref = refs

    for i in range(nq):
        rows = slice(i * tq, (i + 1) * tq)
        n_past = i * tq
        mask = posk_ref[:, rows] <= posq_ref[rows, :]
        probs = []
        for mi in range(n_maps):
            cols = slice(mi * dk, (mi + 1) * dk)
            qm = q_ref[rows, cols]
            s_diag = jnp.where(mask, _dot_nt(qm, k_ref[rows, cols]), -jnp.inf)
            mx = jnp.max(s_diag, axis=-1, keepdims=True)
            if n_past:
                s_past = _dot_nt(qm, k_ref[0:n_past, cols])
                mx = jnp.maximum(mx, jnp.max(s_past, axis=-1, keepdims=True))
            p_diag = jnp.exp(s_diag - mx)
            den = jnp.sum(p_diag, axis=-1, keepdims=True)
            p_past = None
            if n_past:
                p_past = jnp.exp(s_past - mx)
                den = den + jnp.sum(p_past, axis=-1, keepdims=True)
            probs.append((p_diag, p_past, 1.0 / den))

        def pv(w_diag, w_past):
            o = jnp.dot(w_diag.astype(BF16), v_ref[rows, :], preferred_element_type=F32)
            if n_past:
                o = o + jnp.dot(w_past.astype(BF16), v_ref[0:n_past, :],
                                preferred_element_type=F32)
            return o

        if n_maps == 2:
            (d1, p1, r1), (d2, p2, r2) = probs
            c2 = lam * r2
            o = pv(d1 * r1 - d2 * c2, (p1 * r1 - p2 * c2) if n_past else None)
            o = _rms(o) * subg_ref[...] * (1.0 - lam_init)
        else:
            (d1, p1, r1), = probs
            o = pv(d1, p1) * r1
        o_ref[rows, :] = o.astype(o_ref.dtype)


def _attn_call(q, k, v, posq, posk, extra, *, batch, seq, heads, n_maps, dk, dv,
               qcol, kcol, vcol, lam_init, name):
    tq = ATTN_TQ
    in_specs = [
        pl.BlockSpec((seq, n_maps * dk), lambda b, h: (b, qcol + h)),
        pl.BlockSpec((seq, n_maps * dk), lambda b, h: (b, kcol + h)),
        pl.BlockSpec((seq, dv), lambda b, h: (b, vcol + h)),
        pl.BlockSpec((seq, 1), lambda b, h: (b, 0)),
        pl.BlockSpec((1, seq), lambda b, h: (0, b)),
    ] + [pl.BlockSpec(e.shape, lambda b, h: (0, 0)) for e in extra]
    return pl.pallas_call(
        functools.partial(_attn_kernel, n_maps=n_maps, dk=dk, tq=tq, nq=seq // tq,
                          lam_init=lam_init),
        grid=(batch, heads),
        in_specs=in_specs,
        out_specs=pl.BlockSpec((seq, dv), lambda b, h: (b, h)),
        out_shape=jax.ShapeDtypeStruct((batch * seq, heads * dv), BF16),
        compiler_params=_cparams(2, VMEM_LIMIT),
        name=name,
    )(q, k, v, posq, posk, *extra)


def _cross_kernel(q_ref, k_ref, v_ref, o_ref):
    for h in range(CROSS_HEADS):
        sl = slice(h * CROSS_HEAD_DIM, (h + 1) * CROSS_HEAD_DIM)
        s = _dot_nt(q_ref[:, sl], k_ref[:, sl])
        p = jnp.exp(s - jnp.max(s, axis=-1, keepdims=True))
        inv = 1.0 / jnp.sum(p, axis=-1, keepdims=True)
        o = jnp.dot(p.astype(BF16), v_ref[:, sl], preferred_element_type=F32)
        o_ref[:, sl] = (o * inv).astype(o_ref.dtype)


def _cross_call(q, memkv, *, batch, seq, qcol, tq=512):
    nq = seq // tq
    return pl.pallas_call(
        _cross_kernel,
        grid=(batch, nq),
        in_specs=[pl.BlockSpec((tq, CROSS_WIDTH), lambda b, i: (b * nq + i, qcol)),
                  pl.BlockSpec((N_MEM, CROSS_WIDTH), lambda b, i: (b, 0)),
                  pl.BlockSpec((N_MEM, CROSS_WIDTH), lambda b, i: (b, 1))],
        out_specs=pl.BlockSpec((tq, CROSS_WIDTH), lambda b, i: (b * nq + i, 0)),
        out_shape=jax.ShapeDtypeStruct((batch * seq, CROSS_WIDTH), BF16),
        compiler_params=_cparams(2),
        name="cross_attn",
    )(q, memkv, memkv)


def _outproj_kernel(os_ref, oc_ref, w_ref, x_ref, o_ref, wbf_ref):
    @pl.when(pl.program_id(1) == 0)
    def _():
        wbf_ref[...] = w_ref[...].astype(BF16)

    acc = jnp.dot(os_ref[...], wbf_ref[:SELF_WIDTH, :], preferred_element_type=F32)
    acc = acc + jnp.dot(oc_ref[...], wbf_ref[SELF_WIDTH:, :], preferred_element_type=F32)
    o_ref[...] = x_ref[...] + acc


def _outproj_call(o_self, o_cross, w_out, x, *, tm=1024, tn=512):
    m = x.shape[0]
    return pl.pallas_call(
        _outproj_kernel,
        grid=(D_MODEL // tn, m // tm),
        in_specs=[pl.BlockSpec((tm, SELF_WIDTH), lambda j, i: (i, 0)),
                  pl.BlockSpec((tm, CROSS_WIDTH), lambda j, i: (i, 0)),
                  pl.BlockSpec((D_MODEL, tn), lambda j, i: (0, j)),
                  pl.BlockSpec((tm, tn), lambda j, i: (i, j))],
        out_specs=pl.BlockSpec((tm, tn), lambda j, i: (i, j)),
        out_shape=jax.ShapeDtypeStruct((m, D_MODEL), F32),
        scratch_shapes=[pltpu.VMEM((D_MODEL, tn), BF16)],
        compiler_params=_cparams(2, VMEM_LIMIT),
        name="out_proj",
    )(o_self, o_cross, w_out, x)


def _mla_in_kernel(h_ref, w_ref, gcq_ref, gckv_ref, gqm_ref, cq_ref, ckv_ref, qm_ref, kpe_ref):
    a = MLA_Q_RANK
    b = a + MLA_KV_RANK
    c = b + CROSS_WIDTH
    for r in range(h_ref.shape[0] // MM_SUB):
        rows = slice(r * MM_SUB, (r + 1) * MM_SUB)
        acc = jnp.dot(h_ref[rows, :], w_ref[...], preferred_element_type=F32)
        cq_ref[rows, :] = (_rms(acc[:, :a]) * gcq_ref[...]).astype(cq_ref.dtype)
        ckv_ref[rows, :] = (_rms(acc[:, a:b]) * gckv_ref[...]).astype(ckv_ref.dtype)
        for j in range(CROSS_HEADS):
            sl = slice(j * CROSS_HEAD_DIM, (j + 1) * CROSS_HEAD_DIM)
            qm_ref[rows, sl] = (_rms(acc[:, b + j * CROSS_HEAD_DIM:b + (j + 1) * CROSS_HEAD_DIM])
                                * gqm_ref[:, sl]).astype(qm_ref.dtype)
        kpe_ref[rows, :] = acc[:, c:]


def _mla_in_call(h, wcat, gcq, gckv, gqm, *, tm=512):
    m = h.shape[0]
    ncat = wcat.shape[1]
    row = lambda w: pl.BlockSpec((tm, w), lambda i: (i, 0))
    full = lambda r, w: pl.BlockSpec((r, w), lambda i: (0, 0))
    w_spec = pl.BlockSpec((D_MODEL, ncat), lambda i: (0, 0), pipeline_mode=pl.Buffered(1))
    return pl.pallas_call(
        _mla_in_kernel,
        grid=(m // tm,),
        in_specs=[row(D_MODEL), w_spec, full(1, MLA_Q_RANK), full(1, MLA_KV_RANK),
                  full(1, CROSS_WIDTH)],
        out_specs=[row(MLA_Q_RANK), row(MLA_KV_RANK), row(CROSS_WIDTH), row(LANES)],
        out_shape=[jax.ShapeDtypeStruct((m, MLA_Q_RANK), BF16),
                   jax.ShapeDtypeStruct((m, MLA_KV_RANK), BF16),
                   jax.ShapeDtypeStruct((m, CROSS_WIDTH), BF16),
                   jax.ShapeDtypeStruct((m, LANES), F32)],
        compiler_params=_cparams(1, VMEM_LIMIT),
        name="mla_in_proj",
    )(h, wcat, gcq, gckv, gqm)


MLA_GROUP = 4


def _rope_pe(x, c, s):
    lane = lax.broadcasted_iota(I32, x.shape, 1)
    partner = jnp.where((lane & 63) < 32, pltpu.roll(x, 96, axis=1), pltpu.roll(x, 32, axis=1))
    return x * c + partner * s


def _mla_q_kernel(cq_ref, wn_ref, wp_ref, wr_ref, gn_ref, gp_ref, gr_ref, cos_ref, sin_ref, o_ref):
    x = cq_ref[...]
    an = jnp.dot(x, wn_ref[...], preferred_element_type=F32)
    ap = jnp.dot(x, wp_ref[...], preferred_element_type=F32)
    ar = jnp.dot(x, wr_ref[...], preferred_element_type=F32)
    gc = cos_ref[...] * gp_ref[...]
    gs = sin_ref[...] * gr_ref[...]
    for j in range(MLA_GROUP):
        sl = slice(j * 128, (j + 1) * 128)
        nj = an[:, sl]
        pj = ap[:, sl]
        ss = jnp.sum(nj * nj + pj * pj, axis=-1, keepdims=True)
        rs = lax.rsqrt(ss * (1.0 / MLA_QK_DIM) + NORM_EPS)
        o_ref[:, j * 256:j * 256 + 128] = (nj * rs * gn_ref[...]).astype(o_ref.dtype)
        o_ref[:, j * 256 + 128:(j + 1) * 256] = ((pj * gc + ar[:, sl] * gs) * rs).astype(o_ref.dtype)


def _mla_q_call(cq, w_nope, w_pe, w_rot, gn, gp, gr, cos, sin, *, tm=1024):
    m = cq.shape[0]
    g = MLA_HEADS // MLA_GROUP
    tm = min(tm, m)
    wspec = pl.BlockSpec((MLA_Q_RANK, MLA_GROUP * 128), lambda j, i: (0, j))
    gspec = pl.BlockSpec((1, LANES), lambda j, i: (0, 0))
    tspec = pl.BlockSpec((tm, LANES), lambda j, i: (i, 0))
    return pl.pallas_call(
        _mla_q_kernel,
        grid=(g, m // tm),
        in_specs=[pl.BlockSpec((tm, MLA_Q_RANK), lambda j, i: (i, 0)),
                  wspec, wspec, wspec, gspec, gspec, gspec, tspec, tspec],
        out_specs=pl.BlockSpec((tm, MLA_GROUP * MLA_PAD_DIM), lambda j, i: (i, j)),
        out_shape=jax.ShapeDtypeStruct((m, MLA_HEADS * MLA_PAD_DIM), BF16),
        compiler_params=_cparams(2),
        name="mla_q_up",
    )(cq, w_nope, w_pe, w_rot, gn, gp, gr, cos, sin)


def _mla_kv_kernel(ckv_ref, w_ref, kpe_ref, gn_ref, gp_ref, cos_ref, sin_ref, k_ref, v_ref):
    acc = jnp.dot(ckv_ref[...], w_ref[...], preferred_element_type=F32)
    kpe = kpe_ref[...]
    ss_pe = jnp.sum(kpe * kpe, axis=-1, keepdims=True)
    pe = _rope_pe(kpe * gp_ref[...], cos_ref[...], sin_ref[...])
    for j in range(MLA_GROUP):
        kn = acc[:, j * 256:j * 256 + 128]
        ss = jnp.sum(kn * kn, axis=-1, keepdims=True) + ss_pe
        rs = lax.rsqrt(ss * (1.0 / MLA_QK_DIM) + NORM_EPS)
        k_ref[:, j * 256:j * 256 + 128] = (kn * rs * gn_ref[...]).astype(k_ref.dtype)
        k_ref[:, j * 256 + 128:(j + 1) * 256] = (pe * rs).astype(k_ref.dtype)
        v_ref[:, j * 128:(j + 1) * 128] = acc[:, j * 256 + 128:(j + 1) * 256].astype(v_ref.dtype)


def _mla_kv_call(ckv, w_ukv, kpe, gn, gp, cos, sin, *, tm=1024):
    m = ckv.shape[0]
    g = MLA_HEADS // MLA_GROUP
    tm = min(tm, m)
    return pl.pallas_call(
        _mla_kv_kernel,
        grid=(g, m // tm),
        in_specs=[pl.BlockSpec((tm, MLA_KV_RANK), lambda j, i: (i, 0)),
                  pl.BlockSpec((MLA_KV_RANK, MLA_GROUP * 256), lambda j, i: (0, j)),
                  pl.BlockSpec((tm, LANES), lambda j, i: (i, 0)),
                  pl.BlockSpec((1, LANES), lambda j, i: (0, 0)),
                  pl.BlockSpec((1, LANES), lambda j, i: (0, 0)),
                  pl.BlockSpec((tm, LANES), lambda j, i: (i, 0)),
                  pl.BlockSpec((tm, LANES), lambda j, i: (i, 0))],
        out_specs=[pl.BlockSpec((tm, MLA_GROUP * MLA_PAD_DIM), lambda j, i: (i, j)),
                   pl.BlockSpec((tm, MLA_GROUP * MLA_V_DIM), lambda j, i: (i, j))],
        out_shape=[jax.ShapeDtypeStruct((m, MLA_HEADS * MLA_PAD_DIM), BF16),
                   jax.ShapeDtypeStruct((m, MLA_HEADS * MLA_V_DIM), BF16)],
        compiler_params=_cparams(2),
        name="mla_kv_up",
    )(ckv, w_ukv, kpe, gn, gp, cos, sin)


def _route_kernel(x_ref, g_ref, w_ref, b_ref, xn_ref, ids_ref, wts_ref):
    xn = _rms(x_ref[...]) * g_ref[...]
    xn_ref[...] = xn
    x_hi = xn.astype(BF16)
    x_lo = (xn - x_hi.astype(F32)).astype(BF16)
    w = w_ref[...]
    w_hi = w.astype(BF16)
    w_lo = (w - w_hi.astype(F32)).astype(BF16)
    lg = _dot_nt(w_hi, x_hi) + _dot_nt(w_hi, x_lo) + _dot_nt(w_lo, x_hi) + b_ref[...]
    tm = lg.shape[1]
    iota = lax.broadcasted_iota(I32, (EXPERTS_PER_GROUP, tm), 0)

    def first_argmax(v):
        mx = jnp.max(v, axis=0, keepdims=True)
        idx = jnp.min(jnp.where(v == mx, iota, EXPERTS_PER_GROUP), axis=0, keepdims=True)
        return mx, idx

    gl = lg[0:N_GROUPS, :]
    gmax, gsel = first_argmax(gl)
    g_gate = 1.0 / jnp.sum(jnp.exp(gl - gmax), axis=0, keepdims=True)
    el = jnp.zeros((EXPERTS_PER_GROUP, tm), F32)
    for g in range(N_GROUPS):
        lo = N_GROUPS + g * EXPERTS_PER_GROUP
        el = jnp.where(gsel == g, lg[lo:lo + EXPERTS_PER_GROUP, :], el)
    v1, i1 = first_argmax(el)
    v2, i2 = first_argmax(jnp.where(iota == i1, -jnp.inf, el))
    e = jnp.exp(v2 - v1)
    w1 = g_gate / (1.0 + e)
    w2 = g_gate * e / (1.0 + e)
    ids_ref[...] = jnp.concatenate([gsel * EXPERTS_PER_GROUP + i1,
                                    gsel * EXPERTS_PER_GROUP + i2], axis=0)
    wts_ref[...] = jnp.concatenate([w1, w2], axis=0)


def _route_call(x, g, w_t, b_col, *, tm=256):
    m = x.shape[0]
    return pl.pallas_call(
        _route_kernel,
        grid=(m // tm,),
        in_specs=[pl.BlockSpec((tm, D_MODEL), lambda i: (i, 0)),
                  pl.BlockSpec((1, D_MODEL), lambda i: (0, 0)),
                  pl.BlockSpec((LANES, D_MODEL), lambda i: (0, 0)),
                  pl.BlockSpec((LANES, 1), lambda i: (0, 0))],
        out_specs=[pl.BlockSpec((tm, D_MODEL), lambda i: (i, 0)),
                   pl.BlockSpec((TOP_K, tm), lambda i: (0, i)),
                   pl.BlockSpec((TOP_K, tm), lambda i: (0, i))],
        out_shape=[jax.ShapeDtypeStruct((m, D_MODEL), F32),
                   jax.ShapeDtypeStruct((TOP_K, m), I32),
                   jax.ShapeDtypeStruct((TOP_K, m), F32)],
        compiler_params=_cparams(1, VMEM_LIMIT),
        name="moe_route",
    )(x, g, w_t, b_col)


def _sort_kernel(ids_ref, pos_ref, te_ref, nu_ref, rs_ref, rc_ref, rn_ref, *, ts):
    n_tok = ids_ref.shape[1]
    nck = n_tok // ts
    iota_e = lax.broadcasted_iota(I32, (N_EXPERTS, ts), 0)

    def chunk(c):
        return pl.ds(pl.multiple_of(c * ts, ts), ts)

    def count_body(c, acc):
        for k in range(TOP_K):
            oh = jnp.where(iota_e == ids_ref[pl.ds(k, 1), chunk(c)], 1.0, 0.0)
            acc = acc + jnp.sum(oh, axis=1, keepdims=True)
        return acc

    counts = lax.fori_loop(0, nck, count_body, jnp.zeros((N_EXPERTS, 1), F32))
    er = lax.broadcasted_iota(I32, (N_EXPERTS, N_EXPERTS), 0)
    ec = lax.broadcasted_iota(I32, (N_EXPERTS, N_EXPERTS), 1)
    strict_lower = jnp.where(ec < er, 1.0, 0.0).astype(BF16)

    def excl_cumsum(v):
        vb = jnp.broadcast_to(v, (N_EXPERTS, LANES)).astype(BF16)
        return jnp.dot(strict_lower, vb, preferred_element_type=F32)

    padded = jnp.floor((counts + (ROW_ALIGN - 1)) * (1.0 / ROW_ALIGN)) * ROW_ALIGN
    hi = jnp.floor(padded * (1.0 / LANES))
    start = excl_cumsum(hi) * LANES + excl_cumsum(padded - hi * LANES)
    chunks = jnp.floor((counts + (MOE_TM - 0.5)) * (1.0 / MOE_TM))
    first = excl_cumsum(chunks)
    ends = first + chunks
    total = jnp.max(ends, axis=0, keepdims=True)
    chunk_i = lax.broadcasted_iota(I32, (N_EXPERTS, LANES), 1).astype(F32)
    chunk_c = jnp.minimum(chunk_i, total - 1.0)
    te = jnp.sum(jnp.where(ends <= chunk_c, 1.0, 0.0), axis=0, keepdims=True)
    mine = lax.broadcasted_iota(I32, (N_EXPERTS, LANES), 0).astype(F32) == te
    rs = jnp.sum(jnp.where(mine, start + (chunk_c - first) * MOE_TM, 0.0), axis=0, keepdims=True)
    te_ref[...] = te.astype(I32)
    nu_ref[...] = total.astype(I32)
    rs_ref[...] = rs.astype(I32)
    left = jnp.sum(jnp.where(mine, padded - (chunk_c - first) * MOE_TM, 0.0), axis=0, keepdims=True)
    rc_ref[...] = jnp.clip(left, 0.0, float(MOE_TM)).astype(I32)
    real = jnp.sum(jnp.where(mine, counts - (chunk_c - first) * MOE_TM, 0.0), axis=0, keepdims=True)
    rn_ref[...] = jnp.clip(real, 0.0, float(MOE_TM)).astype(I32)

    row_base = start[:, 0:1]
    ur = lax.broadcasted_iota(I32, (ts, ts), 0)
    uc = lax.broadcasted_iota(I32, (ts, ts), 1)
    upper = jnp.where(ur <= uc, 1.0, 0.0).astype(BF16)

    def pos_body(k):
        def body(c, carry):
            hit = iota_e == ids_ref[pl.ds(k, 1), chunk(c)]
            incl = jnp.dot(jnp.where(hit, 1.0, 0.0).astype(BF16), upper,
                           preferred_element_type=F32)
            val = row_base + carry + incl - 1.0
            p = jnp.sum(jnp.where(hit, val, 0.0), axis=0, keepdims=True)
            pos_ref[pl.ds(k, 1), chunk(c)] = p.astype(I32)
            return carry + incl[:, ts - 1:ts]
        return body

    carry = jnp.zeros((N_EXPERTS, 1), F32)
    for k in range(TOP_K):
        carry = lax.fori_loop(0, nck, pos_body(k), carry)


def _sort_call(ids, *, ts=512):
    n_tok = ids.shape[1]
    ts = min(ts, n_tok)
    return pl.pallas_call(
        functools.partial(_sort_kernel, ts=ts),
        out_shape=[jax.ShapeDtypeStruct((TOP_K, n_tok), I32),
                   jax.ShapeDtypeStruct((1, LANES), I32),
                   jax.ShapeDtypeStruct((1, LANES), I32),
                   jax.ShapeDtypeStruct((1, LANES), I32),
                   jax.ShapeDtypeStruct((1, LANES), I32),
                   jax.ShapeDtypeStruct((1, LANES), I32)],
        name="moe_sort",
    )(ids)


def _sorted_rows(n_tok):
    return TOP_K * n_tok + N_EXPERTS * ROW_ALIGN


def _moe_chunks(n_tok):
    return -(-(TOP_K * n_tok) // MOE_TM) + N_EXPERTS


INVERT_BATCH = 16


def _invert_kernel(pos_ref, inv_ref):
    def init(i, c):
        for u in range(INVERT_BATCH):
            inv_ref[i * INVERT_BATCH + u] = jnp.int32(0)
        return c
    lax.fori_loop(0, inv_ref.shape[0] // INVERT_BATCH, init, 0)

    def put(i, c):
        base = i * INVERT_BATCH
        dst = [pos_ref[base + u] for u in range(INVERT_BATCH)]
        for u in range(INVERT_BATCH):
            inv_ref[dst[u]] = base + u
        return c
    lax.fori_loop(0, pos_ref.shape[0] // INVERT_BATCH, put, 0)


def _invert_call(pos_flat, n_rows):
    assert n_rows % INVERT_BATCH == 0 and pos_flat.shape[0] % INVERT_BATCH == 0
    return pl.pallas_call(
        _invert_kernel,
        in_specs=[pl.BlockSpec(memory_space=pltpu.SMEM)],
        out_specs=pl.BlockSpec(memory_space=pltpu.SMEM),
        out_shape=jax.ShapeDtypeStruct((n_rows,), I32),
        name="moe_invert",
    )(pos_flat)


def _moe_kernel(te_ref, nu_ref, rs_ref, rc_ref, rn_ref, inv_ref, xn_hbm, wg_ref, wu_ref, wd_ref,
                y_hbm, xbuf, acc, gu, wgu, gsem, ssem, *, n_tok):
    i = pl.program_id(0)
    k = pl.program_id(1)
    nk = pl.num_programs(1)
    n_used = nu_ref[0]
    valid = i < n_used
    slot = i % 2
    n_slots = TOP_K * n_tok

    def x_row_copy(tok, r, sl):
        return pltpu.make_async_copy(xn_hbm.at[pl.ds(tok, 1)], xbuf.at[sl, pl.ds(r, 1)], gsem.at[sl])

    def x_gather(c, sl):
        base = rs_ref[c]

        def one(r):
            s = inv_ref[base + r]
            x_row_copy(jnp.where(s >= n_tok, s - n_tok, s), r, sl).start()

        def group(g, carry):
            for u in range(DMA_GROUP):
                one(g * DMA_GROUP + u)
            return carry
        n = rn_ref[c]
        n_groups = n >> 3
        lax.fori_loop(0, n_groups, group, 0)

        def rest(r, carry):
            one(r)
            return carry
        lax.fori_loop(n_groups * DMA_GROUP, n, rest, 0)

    def x_wait(c, sl):
        n = rn_ref[c]
        n8 = pl.multiple_of((n >> 3) << 3, ROW_ALIGN)

        @pl.when(n8 > 0)
        def _():
            pltpu.make_async_copy(xn_hbm.at[pl.ds(0, n8)], xbuf.at[sl, pl.ds(0, n8)],
                                  gsem.at[sl]).wait()

        def rest(r, carry):
            x_row_copy(0, 0, sl).wait()
            return carry
        lax.fori_loop(0, n - n8, rest, 0)

    def y_copy(c):
        n = pl.multiple_of(rc_ref[c], ROW_ALIGN)
        dst = pl.ds(pl.multiple_of(rs_ref[c], ROW_ALIGN), n)
        return pltpu.make_async_copy(acc.at[pl.ds(0, n)], y_hbm.at[dst], ssem.at[0])

    @pl.when((i == 0) & (k == 0))
    def _():
        xbuf[...] = jnp.zeros(xbuf.shape, xbuf.dtype)
        x_gather(0, 0)
        acc[...] = jnp.zeros(acc.shape, acc.dtype)
        n_rows = y_hbm.shape[0]
        tails = [(off, min(MOE_TM, n_rows - off)) for off in range(n_slots, n_rows, MOE_TM)]
        for off, size in tails:
            pltpu.make_async_copy(acc.at[pl.ds(0, size)], y_hbm.at[pl.ds(off, size)],
                                  ssem.at[0]).start()
        for off, size in tails:
            pltpu.make_async_copy(acc.at[pl.ds(0, size)], y_hbm.at[pl.ds(off, size)],
                                  ssem.at[0]).wait()

    @pl.when(valid & (k == 0))
    def _():
        x_wait(i, slot)

        @pl.when(i + 1 < n_used)
        def _():
            x_gather(i + 1, 1 - slot)

    @pl.when(valid)
    def _():
        wgu[:, :D_EXPERT] = wg_ref[0].astype(BF16)
        wgu[:, D_EXPERT:] = wu_ref[0].astype(BF16)
        xs = xbuf[slot, :, pl.ds(pl.multiple_of(k * MOE_TK, MOE_TK), MOE_TK)].astype(BF16)
        part = jnp.dot(xs, wgu[...], preferred_element_type=F32)

        @pl.when(k == 0)
        def _():
            gu[...] = part

        @pl.when(k > 0)
        def _():
            gu[...] += part

        @pl.when(k == nk - 1)
        def _():
            a = gu[:, :D_EXPERT]
            hid = (a * jax.nn.sigmoid(a)) * gu[:, D_EXPERT:]
            out = jnp.dot(hid.astype(BF16), wd_ref[0].astype(BF16), preferred_element_type=F32)

            @pl.when(i > 0)
            def _():
                y_copy(i - 1).wait()
            acc[...] = out
            y_copy(i).start()

            @pl.when(i == n_used - 1)
            def _():
                y_copy(i).wait()


def _moe_call(te, nu, rs, rc, rn, inv, xn, w_gate, w_up, w_down):
    n_tok = xn.shape[0]
    nk = D_MODEL // MOE_TK
    last_k = nk - 1

    def w_in_map(i, k, te_r, nu_r, *_):
        return te_r[i], jnp.where(i < nu_r[0], k, last_k), 0

    def w_out_map(i, k, te_r, *_):
        return te_r[i], 0, 0

    grid_spec = pltpu.PrefetchScalarGridSpec(
        num_scalar_prefetch=6,
        grid=(_moe_chunks(n_tok), nk),
        in_specs=[
            pl.BlockSpec(memory_space=pl.ANY),
            pl.BlockSpec((1, MOE_TK, D_EXPERT), w_in_map),
            pl.BlockSpec((1, MOE_TK, D_EXPERT), w_in_map),
            pl.BlockSpec((1, D_EXPERT, D_MODEL), w_out_map),
        ],
        out_specs=pl.BlockSpec(memory_space=pl.ANY),
        scratch_shapes=[pltpu.VMEM((2, MOE_TM, D_MODEL), F32),
                        pltpu.VMEM((MOE_TM, D_MODEL), F32),
                        pltpu.VMEM((MOE_TM, 2 * D_EXPERT), F32),
                        pltpu.VMEM((MOE_TK, 2 * D_EXPERT), BF16),
                        pltpu.SemaphoreType.DMA((2,)),
                        pltpu.SemaphoreType.DMA((1,))],
    )
    return pl.pallas_call(
        functools.partial(_moe_kernel, n_tok=n_tok),
        grid_spec=grid_spec,
        out_shape=jax.ShapeDtypeStruct((_sorted_rows(n_tok), D_MODEL), F32),
        compiler_params=_cparams(2, VMEM_LIMIT),
        name="moe_experts",
    )(te, nu, rs, rc, rn, inv, xn, w_gate, w_up, w_down)


def _combine_kernel(pos_ref, x_ref, w_ref, g_ref, y_hbm, xo_ref, *rest, n_tok):
    *h_ref, ybuf, sem = rest
    i = pl.program_id(0)
    nb = pl.num_programs(0)
    tm = x_ref.shape[0]
    slot = i % 2

    def fetch(tile, sl):
        def group(g, c):
            for u in range(DMA_GROUP):
                r = g * DMA_GROUP + u
                for k in range(TOP_K):
                    src = pos_ref[k * n_tok + tile * tm + r]
                    pltpu.make_async_copy(y_hbm.at[pl.ds(src, 1)], ybuf.at[sl, k, pl.ds(r, 1)],
                                          sem.at[sl]).start()
            return c
        lax.fori_loop(0, tm // DMA_GROUP, group, 0)

    @pl.when(i == 0)
    def _():
        fetch(0, 0)

    @pl.when(i + 1 < nb)
    def _():
        fetch(i + 1, 1 - slot)

    for k in range(TOP_K):
        pltpu.make_async_copy(y_hbm.at[pl.ds(0, tm)], ybuf.at[slot, k], sem.at[slot]).wait()
    w = w_ref[...]
    x = x_ref[...] + w[:, 0:1] * ybuf[slot, 0] + w[:, 1:2] * ybuf[slot, 1]
    xo_ref[...] = x
    if h_ref:
        h_ref[0][...] = (_rms(x) * g_ref[...]).astype(BF16)


def _combine_call(pos_flat, x, y, wts_t, g_next, *, tm=256):
    m, d = x.shape
    with_norm = g_next is not None
    g = g_next.reshape(1, d) if with_norm else jnp.ones((1, d), F32)
    row = pl.BlockSpec((tm, d), lambda i, pos_r: (i, 0))
    out_specs = [row, row] if with_norm else [row]
    out_shape = [jax.ShapeDtypeStruct((m, d), F32)]
    if with_norm:
        out_shape.append(jax.ShapeDtypeStruct((m, d), BF16))
    grid_spec = pltpu.PrefetchScalarGridSpec(
        num_scalar_prefetch=1,
        grid=(m // tm,),
        in_specs=[row,
                  pl.BlockSpec((tm, TOP_K), lambda i, pos_r: (i, 0)),
                  pl.BlockSpec((1, d), lambda i, pos_r: (0, 0)),
                  pl.BlockSpec(memory_space=pl.ANY)],
        out_specs=out_specs,
        scratch_shapes=[pltpu.VMEM((2, TOP_K, tm, d), F32),
                        pltpu.SemaphoreType.DMA((2,))],
    )
    return pl.pallas_call(
        functools.partial(_combine_kernel, n_tok=m),
        grid_spec=grid_spec,
        out_shape=out_shape,
        compiler_params=_cparams(1, VMEM_LIMIT),
        name="moe_combine",
    )(pos_flat, x, wts_t, g, y)


def _moe_layer(x, norm_g, w_group, b_group, w_router, b_router, w_gate, w_up, w_down, g_next):
    n_tok = x.shape[0]
    pad_rows = LANES - N_GROUPS - N_EXPERTS
    w_t = jnp.concatenate([w_group.T, w_router.T, jnp.zeros((pad_rows, D_MODEL), F32)], axis=0)
    b_col = jnp.concatenate([b_group, b_router, jnp.zeros((pad_rows,), F32)]).reshape(LANES, 1)
    xn, ids, wts = _route_call(x, norm_g.reshape(1, D_MODEL), w_t, b_col)
    pos, te, nu, rs, rc, rn = _sort_call(ids)
    pos_flat = pos.reshape(-1)
    inv = _invert_call(pos_flat, _sorted_rows(n_tok))
    y = _moe_call(te.reshape(-1), nu.reshape(-1)[:1], rs.reshape(-1), rc.reshape(-1),
                  rn.reshape(-1), inv, xn,
                  w_gate.reshape(N_EXPERTS, D_MODEL, D_EXPERT),
                  w_up.reshape(N_EXPERTS, D_MODEL, D_EXPERT),
                  w_down.reshape(N_EXPERTS, D_EXPERT, D_MODEL))
    return _combine_call(pos_flat, x, y, wts.T, g_next)


def _rope_tables(pos_flat, dim):
    inv_freq = ROPE_THETA ** (-jnp.arange(0, dim, 2, dtype=F32) / dim)
    ang = pos_flat.astype(F32)[:, None] * inv_freq
    return jnp.cos(ang), jnp.sin(ang)


def kernel(x, mem, positions, mem_norm_g, w_mem_kv, mem_k_norm_g, l0_attn_norm_g, l0_w_in, l0_q_norm_g, l0_k_norm_g, l0_lambda_q1, l0_lambda_k1, l0_lambda_q2, l0_lambda_k2, l0_subln_g, l0_cross_q_norm_g, l0_w_out, l0_ffn_norm_g, l0_w_group, l0_b_group, l0_w_router, l0_b_router, l0_w_gate, l0_w_up, l0_w_down, l1_attn_norm_g, l1_w_in, l1_cq_norm_g, l1_ckv_norm_g, l1_w_uq, l1_w_ukv, l1_q_norm_g, l1_k_norm_g, l1_cross_q_norm_g, l1_w_out, l1_ffn_norm_g, l1_w_group, l1_b_group, l1_w_router, l1_b_router, l1_w_gate, l1_w_up, l1_w_down):
    batch, seq, d = x.shape
    n_tok = batch * seq
    xf = x.reshape(n_tok, d)
    pos_flat = positions.reshape(n_tok)
    posq = pos_flat.reshape(n_tok, 1)
    posk = pos_flat.reshape(1, n_tok)
    ones128 = jnp.ones((n_tok, LANES), F32)

    c64, s64 = _rope_tables(pos_flat, DIFF_HEAD_DIM)
    cos_full = jnp.concatenate([c64, c64], axis=1)
    sin_full = jnp.concatenate([-s64, s64], axis=1)
    c32, s32 = _rope_tables(pos_flat, MLA_ROPE_DIM)
    z32 = jnp.zeros_like(c32)
    cos_k = jnp.concatenate([c32, c32, z32, z32], axis=1)
    sin_k = jnp.concatenate([-s32, s32, z32, z32], axis=1)

    cross_scale = CROSS_HEAD_DIM ** -0.5

    memn = _norm_call(mem.reshape(batch * N_MEM, d), mem_norm_g)
    mem_gain = jnp.concatenate([jnp.tile(mem_k_norm_g, CROSS_HEADS), jnp.ones((CROSS_WIDTH,), F32)])
    k_tiles = CROSS_WIDTH // 512
    memkv = _mm_call(memn, w_mem_kv, mem_gain.reshape(1, -1), ones128[:batch * N_MEM],
                     ones128[:batch * N_MEM], ((0, k_tiles, "head256"), (k_tiles, 2 * k_tiles, "plain")),
                     tm=batch * N_MEM, tn=512, name="mem_kv")

    h0 = _norm_call(xf, l0_attn_norm_g)
    qk_tiles = SELF_WIDTH // 512
    gain0 = jnp.concatenate([
        jnp.tile(l0_q_norm_g, 2 * DIFF_HEADS) * (DIFF_HEAD_DIM ** -0.5),
        jnp.tile(l0_k_norm_g, 2 * DIFF_HEADS),
        jnp.ones((SELF_WIDTH,), F32),
        jnp.tile(l0_cross_q_norm_g, CROSS_HEADS) * cross_scale]).reshape(1, -1)
    proj0 = _mm_call(h0, l0_w_in, gain0, cos_full, sin_full,
                     ((0, 2 * qk_tiles, "head128_rope"), (2 * qk_tiles, 3 * qk_tiles, "plain"),
                      (3 * qk_tiles, 3 * qk_tiles + CROSS_WIDTH // 512, "head256")),
                     tm=1024, tn=512, name="l0_in_proj")
    lam_init = 0.8 - 0.6 * math.exp(-0.3 * 0)
    row = lambda v: v.reshape(1, -1)
    o_self = _attn_call(proj0, proj0, proj0, posq, posk,
                        (row(l0_lambda_q1), row(l0_lambda_k1), row(l0_lambda_q2),
                         row(l0_lambda_k2), row(l0_subln_g)),
                        batch=batch, seq=seq, heads=DIFF_HEADS, n_maps=2, dk=DIFF_HEAD_DIM,
                        dv=DIFF_V_DIM, qcol=0, kcol=DIFF_HEADS, vcol=2 * DIFF_HEADS,
                        lam_init=lam_init, name="diff_attn")
    o_cross = _cross_call(proj0, memkv, batch=batch, seq=seq, qcol=3 * SELF_WIDTH // CROSS_WIDTH)
    x1 = _outproj_call(o_self, o_cross, l0_w_out, xf)
    x2, h1 = _moe_layer(x1, l0_ffn_norm_g, l0_w_group, l0_b_group, l0_w_router, l0_b_router,
                        l0_w_gate, l0_w_up, l0_w_down, l1_attn_norm_g)

    a = MLA_Q_RANK
    b = a + MLA_KV_RANK
    c = b + MLA_ROPE_DIM
    wcat = jnp.concatenate([l1_w_in[:, :b], l1_w_in[:, c:], l1_w_in[:, b:c],
                            jnp.zeros((d, LANES - MLA_ROPE_DIM), F32)], axis=1).astype(BF16)
    cq, ckv, qm, kpe = _mla_in_call(
        h1, wcat, row(l1_cq_norm_g), row(l1_ckv_norm_g),
        row(jnp.tile(l1_cross_q_norm_g, CROSS_HEADS) * cross_scale))
    w_uq3 = l1_w_uq.reshape(MLA_Q_RANK, MLA_HEADS, MLA_QK_DIM)
    w_q_nope = w_uq3[:, :, :MLA_NOPE_DIM].reshape(MLA_Q_RANK, -1).astype(BF16)
    partner = (jnp.arange(MLA_ROPE_DIM) + MLA_ROPE_DIM // 2) % MLA_ROPE_DIM
    w_pe3 = w_uq3[:, :, MLA_NOPE_DIM:]
    pad3 = jnp.zeros_like(w_pe3)
    w_q_pe = jnp.concatenate([w_pe3, pad3], axis=2).reshape(MLA_Q_RANK, -1).astype(BF16)
    w_q_rot = jnp.concatenate([w_pe3[:, :, partner], pad3], axis=2).reshape(MLA_Q_RANK, -1).astype(BF16)
    q_scale = MLA_QK_DIM ** -0.5
    zeros64 = jnp.zeros((MLA_ROPE_DIM,), F32)
    g_q_pe = l1_q_norm_g[MLA_NOPE_DIM:] * q_scale
    q_pad = _mla_q_call(cq, w_q_nope, w_q_pe, w_q_rot,
                        row(l1_q_norm_g[:MLA_NOPE_DIM] * q_scale),
                        row(jnp.concatenate([g_q_pe, zeros64])),
                        row(jnp.concatenate([g_q_pe[partner], zeros64])), cos_k, sin_k)
    k_pad, v1 = _mla_kv_call(ckv, l1_w_ukv.astype(BF16), kpe,
                             row(l1_k_norm_g[:MLA_NOPE_DIM]),
                             row(jnp.concatenate([l1_k_norm_g[MLA_NOPE_DIM:], zeros64])),
                             cos_k, sin_k)
    o_self1 = _attn_call(q_pad, k_pad, v1, posq, posk, (),
                         batch=batch, seq=seq, heads=MLA_HEADS, n_maps=1, dk=MLA_PAD_DIM,
                         dv=MLA_V_DIM, qcol=0, kcol=0, vcol=0, lam_init=0.0, name="mla_attn")
    o_cross1 = _cross_call(qm, memkv, batch=batch, seq=seq, qcol=0)
    x3 = _outproj_call(o_self1, o_cross1, l1_w_out, x2)
    (x4,) = _moe_layer(x3, l1_ffn_norm_g, l1_w_group, l1_b_group, l1_w_router, l1_b_router,
                       l1_w_gate, l1_w_up, l1_w_down, None)
    return x4.reshape(batch, seq, d)
```

```python
import functools
import math

import jax
import jax.numpy as jnp
from jax import lax
from jax.experimental import pallas as pl
from jax.experimental.pallas import tpu as pltpu

F32 = jnp.float32
BF16 = jnp.bfloat16
I32 = jnp.int32

D_MODEL = 4096
N_MEM = 256
ROPE_THETA = 10000.0
NORM_EPS = 1e-6
LOG2E = math.log2(math.e)

SELF_WIDTH = 3 * D_MODEL // 4
CROSS_HEADS = 4
CROSS_HEAD_DIM = (D_MODEL // 4) // CROSS_HEADS
CROSS_WIDTH = CROSS_HEADS * CROSS_HEAD_DIM

DIFF_HEAD_DIM = 128
DIFF_V_DIM = 2 * DIFF_HEAD_DIM
DIFF_HEADS = SELF_WIDTH // DIFF_V_DIM

MLA_NOPE_DIM = 128
MLA_ROPE_DIM = 64
MLA_V_DIM = 128
MLA_HEADS = SELF_WIDTH // MLA_V_DIM
MLA_QK_DIM = MLA_NOPE_DIM + MLA_ROPE_DIM
MLA_Q_RANK = 3 * D_MODEL // 16
MLA_KV_RANK = D_MODEL // 8
MLA_PAD_DIM = 256

N_GROUPS = 8
EXPERTS_PER_GROUP = 8
N_EXPERTS = N_GROUPS * EXPERTS_PER_GROUP
D_EXPERT = 3 * D_MODEL // 32
TOP_K = 2

LANES = 128
MOE_TM = 320
MOE_TK = 2048
DMA_GROUP = 8
ROW_ALIGN = 8
ATTN_TQ = 256
VMEM_LIMIT = 56 * 1024 * 1024


def _cparams(n_axes, vmem=None):
    return pltpu.CompilerParams(dimension_semantics=("arbitrary",) * n_axes,
                                vmem_limit_bytes=vmem)


def _rms(x):
    return x * lax.rsqrt(jnp.mean(x * x, axis=-1, keepdims=True) + NORM_EPS)


def _dot_nt(a, b):
    return lax.dot_general(a, b, (((1,), (1,)), ((), ())), preferred_element_type=F32)


def _norm_kernel(x_ref, g_ref, o_ref):
    o_ref[...] = (_rms(x_ref[...]) * g_ref[...]).astype(o_ref.dtype)


def _norm_call(x, g, tm=256):
    m, d = x.shape
    return pl.pallas_call(
        _norm_kernel,
        grid=(m // tm,),
        in_specs=[pl.BlockSpec((tm, d), lambda i: (i, 0)),
                  pl.BlockSpec((1, d), lambda i: (0, 0))],
        out_specs=pl.BlockSpec((tm, d), lambda i: (i, 0)),
        out_shape=jax.ShapeDtypeStruct((m, d), BF16),
        compiler_params=_cparams(1),
        name="rmsnorm",
    )(x, g.reshape(1, d))


def _epilogue(kind, acc, rows, gain_ref, cos_ref, sin_ref, o_ref):
    tn = acc.shape[1]
    if kind == "plain":
        o_ref[rows, :] = acc.astype(o_ref.dtype)
    elif kind == "head128_rope":
        c = cos_ref[rows, :]
        s = sin_ref[rows, :]
        for j in range(tn // 128):
            sl = slice(j * 128, (j + 1) * 128)
            y = _rms(acc[:, sl]) * gain_ref[:, sl]
            y = y * c + pltpu.roll(y, 64, axis=1) * s
            o_ref[rows, sl] = y.astype(o_ref.dtype)
    elif kind == "head256":
        for j in range(tn // 256):
            sl = slice(j * 256, (j + 1) * 256)
            o_ref[rows, sl] = (_rms(acc[:, sl]) * gain_ref[:, sl]).astype(o_ref.dtype)
    else:
        raise ValueError(kind)


MM_SUB = 256


def _mm_kernel(x_ref, w_ref, gain_ref, cos_ref, sin_ref, o_ref, wbf_ref, *, kinds):
    n = pl.program_id(0)

    @pl.when(pl.program_id(1) == 0)
    def _():
        wbf_ref[...] = w_ref[...].astype(BF16)

    def run(kind):
        for r in range(x_ref.shape[0] // MM_SUB):
            rows = slice(r * MM_SUB, (r + 1) * MM_SUB)
            acc = jnp.dot(x_ref[rows, :], wbf_ref[...], preferred_element_type=F32)
            _epilogue(kind, acc, rows, gain_ref, cos_ref, sin_ref, o_ref)

    if len(kinds) == 1:
        run(kinds[0][2])
    else:
        for lo, hi, kind in kinds:
            pl.when((n >= lo) & (n < hi))(functools.partial(run, kind))


def _mm_call(x, w, gain, cos, sin, kinds, *, tm, tn, name):
    m, k = x.shape
    n = w.shape[1]
    return pl.pallas_call(
        functools.partial(_mm_kernel, kinds=kinds),
        grid=(n // tn, m // tm),
        in_specs=[pl.BlockSpec((tm, k), lambda j, i: (i, 0)),
                  pl.BlockSpec((k, tn), lambda j, i: (0, j)),
                  pl.BlockSpec((1, tn), lambda j, i: (0, j)),
                  pl.BlockSpec((tm, LANES), lambda j, i: (i, 0)),
                  pl.BlockSpec((tm, LANES), lambda j, i: (i, 0))],
        out_specs=pl.BlockSpec((tm, tn), lambda j, i: (i, j)),
        out_shape=jax.ShapeDtypeStruct((m, n), BF16),
        scratch_shapes=[pltpu.VMEM((k, tn), BF16)],
        compiler_params=_cparams(2, VMEM_LIMIT),
        name=name,
    )(x, w, gain, cos, sin)


def _attn_kernel(*refs, n_maps, dk, tq, nq, lam_init):
    if n_maps == 2:
        (q_ref, k_ref, v_ref, posq_ref, posk_ref, lq1_ref, lk1_ref, lq2_ref, lk2_ref, subg_ref,
         o_ref) = refs
        lam = (jnp.exp(jnp.sum(lq1_ref[...] * lk1_ref[...], axis=-1, keepdims=True))
               - jnp.exp(jnp.sum(lq2_ref[...] * lk2_ref[...], axis=-1, keepdims=True))
               + lam_init)
    else:
        q_ref, k_ref, v_ref, posq_ref, posk_ref, o_ref = refs

    for i in range(nq):
        rows = slice(i * tq, (i + 1) * tq)
        n_past = i * tq
        mask = posk_ref[:, rows] <= posq_ref[rows, :]
        probs = []
        for mi in range(n_maps):
            cols = slice(mi * dk, (mi + 1) * dk)
            qm = q_ref[rows, cols]
            s_diag = jnp.where(mask, _dot_nt(qm, k_ref[rows, cols]), -jnp.inf)
            mx = jnp.max(s_diag, axis=-1, keepdims=True)
            if n_past:
                s_past = _dot_nt(qm, k_ref[0:n_past, cols])
                mx = jnp.maximum(mx, jnp.max(s_past, axis=-1, keepdims=True))
            p_diag = jnp.exp2(s_diag - mx)
            den = jnp.sum(p_diag, axis=-1, keepdims=True)
            p_past = None
            if n_past:
                p_past = jnp.exp2(s_past - mx)
                den = den + jnp.sum(p_past, axis=-1, keepdims=True)
            probs.append((p_diag, p_past, 1.0 / den))

        def pv(w_diag, w_past):
            o = jnp.dot(w_diag.astype(BF16), v_ref[rows, :], preferred_element_type=F32)
            if n_past:
                o = o + jnp.dot(w_past.astype(BF16), v_ref[0:n_past, :],
                                preferred_element_type=F32)
            return o

        if n_maps == 2:
            (d1, p1, r1), (d2, p2, r2) = probs
            c2 = lam * r2
            o = pv(d1 * r1 - d2 * c2, (p1 * r1 - p2 * c2) if n_past else None)
            o = _rms(o) * subg_ref[...] * (1.0 - lam_init)
        else:
            (d1, p1, r1), = probs
            o = pv(d1, p1) * r1
        o_ref[rows, :] = o.astype(o_ref.dtype)


def _attn_call(q, k, v, posq, posk, extra, *, batch, seq, heads, n_maps, dk, dv,
               qcol, kcol, vcol, lam_init, name):
    tq = ATTN_TQ
    in_specs = [
        pl.BlockSpec((seq, n_maps * dk), lambda b, h: (b, qcol + h)),
        pl.BlockSpec((seq, n_maps * dk), lambda b, h: (b, kcol + h)),
        pl.BlockSpec((seq, dv), lambda b, h: (b, vcol + h)),
        pl.BlockSpec((seq, 1), lambda b, h: (b, 0)),
        pl.BlockSpec((1, seq), lambda b, h: (0, b)),
    ] + [pl.BlockSpec(e.shape, lambda b, h: (0, 0)) for e in extra]
    return pl.pallas_call(
        functools.partial(_attn_kernel, n_maps=n_maps, dk=dk, tq=tq, nq=seq // tq,
                          lam_init=lam_init),
        grid=(batch, heads),
        in_specs=in_specs,
        out_specs=pl.BlockSpec((seq, dv), lambda b, h: (b, h)),
        out_shape=jax.ShapeDtypeStruct((batch * seq, heads * dv), BF16),
        compiler_params=_cparams(2, VMEM_LIMIT),
        name=name,
    )(q, k, v, posq, posk, *extra)


def _cross_kernel(q_ref, k_ref, v_ref, o_ref):
    for h in range(CROSS_HEADS):
        sl = slice(h * CROSS_HEAD_DIM, (h + 1) * CROSS_HEAD_DIM)
        s = _dot_nt(q_ref[:, sl], k_ref[:, sl])
        p = jnp.exp2(s - jnp.max(s, axis=-1, keepdims=True))
        inv = 1.0 / jnp.sum(p, axis=-1, keepdims=True)
        o = jnp.dot(p.astype(BF16), v_ref[:, sl], preferred_element_type=F32)
        o_ref[:, sl] = (o * inv).astype(o_ref.dtype)


def _cross_call(q, memkv, *, batch, seq, qcol, tq=512):
    nq = seq // tq
    return pl.pallas_call(
        _cross_kernel,
        grid=(batch, nq),
        in_specs=[pl.BlockSpec((tq, CROSS_WIDTH), lambda b, i: (b * nq + i, qcol)),
                  pl.BlockSpec((N_MEM, CROSS_WIDTH), lambda b, i: (b, 0)),
                  pl.BlockSpec((N_MEM, CROSS_WIDTH), lambda b, i: (b, 1))],
        out_specs=pl.BlockSpec((tq, CROSS_WIDTH), lambda b, i: (b * nq + i, 0)),
        out_shape=jax.ShapeDtypeStruct((batch * seq, CROSS_WIDTH), BF16),
        compiler_params=_cparams(2),
        name="cross_attn",
    )(q, memkv, memkv)


def _outproj_kernel(os_ref, oc_ref, w_ref, x_ref, o_ref, wbf_ref):
    @pl.when(pl.program_id(1) == 0)
    def _():
        wbf_ref[...] = w_ref[...].astype(BF16)

    acc = jnp.dot(os_ref[...], wbf_ref[:SELF_WIDTH, :], preferred_element_type=F32)
    acc = acc + jnp.dot(oc_ref[...], wbf_ref[SELF_WIDTH:, :], preferred_element_type=F32)
    o_ref[...] = x_ref[...] + acc


def _outproj_call(o_self, o_cross, w_out, x, *, tm=1024, tn=512):
    m = x.shape[0]
    return pl.pallas_call(
        _outproj_kernel,
        grid=(D_MODEL // tn, m // tm),
        in_specs=[pl.BlockSpec((tm, SELF_WIDTH), lambda j, i: (i, 0)),
                  pl.BlockSpec((tm, CROSS_WIDTH), lambda j, i: (i, 0)),
                  pl.BlockSpec((D_MODEL, tn), lambda j, i: (0, j)),
                  pl.BlockSpec((tm, tn), lambda j, i: (i, j))],
        out_specs=pl.BlockSpec((tm, tn), lambda j, i: (i, j)),
        out_shape=jax.ShapeDtypeStruct((m, D_MODEL), F32),
        scratch_shapes=[pltpu.VMEM((D_MODEL, tn), BF16)],
        compiler_params=_cparams(2, VMEM_LIMIT),
        name="out_proj",
    )(o_self, o_cross, w_out, x)


def _mla_in_kernel(h_ref, w_ref, gcq_ref, gckv_ref, gqm_ref, cq_ref, ckv_ref, qm_ref, kpe_ref):
    a = MLA_Q_RANK
    b = a + MLA_KV_RANK
    c = b + CROSS_WIDTH
    for r in range(h_ref.shape[0] // MM_SUB):
        rows = slice(r * MM_SUB, (r + 1) * MM_SUB)
        acc = jnp.dot(h_ref[rows, :], w_ref[...], preferred_element_type=F32)
        cq_ref[rows, :] = (_rms(acc[:, :a]) * gcq_ref[...]).astype(cq_ref.dtype)
        ckv_ref[rows, :] = (_rms(acc[:, a:b]) * gckv_ref[...]).astype(ckv_ref.dtype)
        for j in range(CROSS_HEADS):
            sl = slice(j * CROSS_HEAD_DIM, (j + 1) * CROSS_HEAD_DIM)
            qm_ref[rows, sl] = (_rms(acc[:, b + j * CROSS_HEAD_DIM:b + (j + 1) * CROSS_HEAD_DIM])
                                * gqm_ref[:, sl]).astype(qm_ref.dtype)
        kpe_ref[rows, :] = acc[:, c:]


def _mla_in_call(h, wcat, gcq, gckv, gqm, *, tm=512):
    m = h.shape[0]
    ncat = wcat.shape[1]
    row = lambda w: pl.BlockSpec((tm, w), lambda i: (i, 0))
    full = lambda r, w: pl.BlockSpec((r, w), lambda i: (0, 0))
    w_spec = pl.BlockSpec((D_MODEL, ncat), lambda i: (0, 0), pipeline_mode=pl.Buffered(1))
    return pl.pallas_call(
        _mla_in_kernel,
        grid=(m // tm,),
        in_specs=[row(D_MODEL), w_spec, full(1, MLA_Q_RANK), full(1, MLA_KV_RANK),
                  full(1, CROSS_WIDTH)],
        out_specs=[row(MLA_Q_RANK), row(MLA_KV_RANK), row(CROSS_WIDTH), row(LANES)],
        out_shape=[jax.ShapeDtypeStruct((m, MLA_Q_RANK), BF16),
                   jax.ShapeDtypeStruct((m, MLA_KV_RANK), BF16),
                   jax.ShapeDtypeStruct((m, CROSS_WIDTH), BF16),
                   jax.ShapeDtypeStruct((m, LANES), F32)],
        compiler_params=_cparams(1, VMEM_LIMIT),
        name="mla_in_proj",
    )(h, wcat, gcq, gckv, gqm)


MLA_GROUP = 4


def _rope_pe(x, c, s):
    lane = lax.broadcasted_iota(I32, x.shape, 1)
    partner = jnp.where((lane & 63) < 32, pltpu.roll(x, 96, axis=1), pltpu.roll(x, 32, axis=1))
    return x * c + partner * s


def _mla_q_kernel(cq_ref, wn_ref, wp_ref, wr_ref, gn_ref, gp_ref, gr_ref, cos_ref, sin_ref, o_ref):
    x = cq_ref[...]
    an = jnp.dot(x, wn_ref[...], preferred_element_type=F32)
    ap = jnp.dot(x, wp_ref[...], preferred_element_type=F32)
    ar = jnp.dot(x, wr_ref[...], preferred_element_type=F32)
    gc = cos_ref[...] * gp_ref[...]
    gs = sin_ref[...] * gr_ref[...]
    for j in range(MLA_GROUP):
        sl = slice(j * 128, (j + 1) * 128)
        nj = an[:, sl]
        pj = ap[:, sl]
        ss = jnp.sum(nj * nj + pj * pj, axis=-1, keepdims=True)
        rs = lax.rsqrt(ss * (1.0 / MLA_QK_DIM) + NORM_EPS)
        o_ref[:, j * 256:j * 256 + 128] = (nj * rs * gn_ref[...]).astype(o_ref.dtype)
        o_ref[:, j * 256 + 128:(j + 1) * 256] = ((pj * gc + ar[:, sl] * gs) * rs).astype(o_ref.dtype)


def _mla_q_call(cq, w_nope, w_pe, w_rot, gn, gp, gr, cos, sin, *, tm=1024):
    m = cq.shape[0]
    g = MLA_HEADS // MLA_GROUP
    tm = min(tm, m)
    wspec = pl.BlockSpec((MLA_Q_RANK, MLA_GROUP * 128), lambda j, i: (0, j))
    gspec = pl.BlockSpec((1, LANES), lambda j, i: (0, 0))
    tspec = pl.BlockSpec((tm, LANES), lambda j, i: (i, 0))
    return pl.pallas_call(
        _mla_q_kernel,
        grid=(g, m // tm),
        in_specs=[pl.BlockSpec((tm, MLA_Q_RANK), lambda j, i: (i, 0)),
                  wspec, wspec, wspec, gspec, gspec, gspec, tspec, tspec],
        out_specs=pl.BlockSpec((tm, MLA_GROUP * MLA_PAD_DIM), lambda j, i: (i, j)),
        out_shape=jax.ShapeDtypeStruct((m, MLA_HEADS * MLA_PAD_DIM), BF16),
        compiler_params=_cparams(2),
        name="mla_q_up",
    )(cq, w_nope, w_pe, w_rot, gn, gp, gr, cos, sin)


def _mla_kv_kernel(ckv_ref, w_ref, kpe_ref, gn_ref, gp_ref, cos_ref, sin_ref, k_ref, v_ref):
    acc = jnp.dot(ckv_ref[...], w_ref[...], preferred_element_type=F32)
    kpe = kpe_ref[...]
    ss_pe = jnp.sum(kpe * kpe, axis=-1, keepdims=True)
    pe = _rope_pe(kpe * gp_ref[...], cos_ref[...], sin_ref[...])
    for j in range(MLA_GROUP):
        kn = acc[:, j * 256:j * 256 + 128]
        ss = jnp.sum(kn * kn, axis=-1, keepdims=True) + ss_pe
        rs = lax.rsqrt(ss * (1.0 / MLA_QK_DIM) + NORM_EPS)
        k_ref[:, j * 256:j * 256 + 128] = (kn * rs * gn_ref[...]).astype(k_ref.dtype)
        k_ref[:, j * 256 + 128:(j + 1) * 256] = (pe * rs).astype(k_ref.dtype)
        v_ref[:, j * 128:(j + 1) * 128] = acc[:, j * 256 + 128:(j + 1) * 256].astype(v_ref.dtype)


def _mla_kv_call(ckv, w_ukv, kpe, gn, gp, cos, sin, *, tm=1024):
    m = ckv.shape[0]
    g = MLA_HEADS // MLA_GROUP
    tm = min(tm, m)
    return pl.pallas_call(
        _mla_kv_kernel,
        grid=(g, m // tm),
        in_specs=[pl.BlockSpec((tm, MLA_KV_RANK), lambda j, i: (i, 0)),
                  pl.BlockSpec((MLA_KV_RANK, MLA_GROUP * 256), lambda j, i: (0, j)),
                  pl.BlockSpec((tm, LANES), lambda j, i: (i, 0)),
                  pl.BlockSpec((1, LANES), lambda j, i: (0, 0)),
                  pl.BlockSpec((1, LANES), lambda j, i: (0, 0)),
                  pl.BlockSpec((tm, LANES), lambda j, i: (i, 0)),
                  pl.BlockSpec((tm, LANES), lambda j, i: (i, 0))],
        out_specs=[pl.BlockSpec((tm, MLA_GROUP * MLA_PAD_DIM), lambda j, i: (i, j)),
                   pl.BlockSpec((tm, MLA_GROUP * MLA_V_DIM), lambda j, i: (i, j))],
        out_shape=[jax.ShapeDtypeStruct((m, MLA_HEADS * MLA_PAD_DIM), BF16),
                   jax.ShapeDtypeStruct((m, MLA_HEADS * MLA_V_DIM), BF16)],
        compiler_params=_cparams(2),
        name="mla_kv_up",
    )(ckv, w_ukv, kpe, gn, gp, cos, sin)


def _route_kernel(x_ref, g_ref, w_ref, b_ref, ids_ref, wts_ref):
    xn = _rms(x_ref[...]) * g_ref[...]
    x_hi = xn.astype(BF16)
    x_lo = (xn - x_hi.astype(F32)).astype(BF16)
    w = w_ref[...]
    w_hi = w.astype(BF16)
    w_lo = (w - w_hi.astype(F32)).astype(BF16)
    lg = _dot_nt(w_hi, x_hi) + _dot_nt(w_hi, x_lo) + _dot_nt(w_lo, x_hi) + b_ref[...]
    tm = lg.shape[1]
    iota = lax.broadcasted_iota(I32, (EXPERTS_PER_GROUP, tm), 0)

    def first_argmax(v):
        mx = jnp.max(v, axis=0, keepdims=True)
        idx = jnp.min(jnp.where(v == mx, iota, EXPERTS_PER_GROUP), axis=0, keepdims=True)
        return mx, idx

    gl = lg[0:N_GROUPS, :]
    gmax, gsel = first_argmax(gl)
    g_gate = 1.0 / jnp.sum(jnp.exp(gl - gmax), axis=0, keepdims=True)
    el = jnp.zeros((EXPERTS_PER_GROUP, tm), F32)
    for g in range(N_GROUPS):
        lo = N_GROUPS + g * EXPERTS_PER_GROUP
        el = jnp.where(gsel == g, lg[lo:lo + EXPERTS_PER_GROUP, :], el)
    v1, i1 = first_argmax(el)
    v2, i2 = first_argmax(jnp.where(iota == i1, -jnp.inf, el))
    e = jnp.exp(v2 - v1)
    w1 = g_gate / (1.0 + e)
    w2 = g_gate * e / (1.0 + e)
    ids_ref[...] = jnp.concatenate([gsel * EXPERTS_PER_GROUP + i1,
                                    gsel * EXPERTS_PER_GROUP + i2], axis=0)
    wts_ref[...] = jnp.concatenate([w1, w2], axis=0)


def _route_call(x, g, w_t, b_col, *, tm=256):
    m = x.shape[0]
    return pl.pallas_call(
        _route_kernel,
        grid=(m // tm,),
        in_specs=[pl.BlockSpec((tm, D_MODEL), lambda i: (i, 0)),
                  pl.BlockSpec((1, D_MODEL), lambda i: (0, 0)),
                  pl.BlockSpec((LANES, D_MODEL), lambda i: (0, 0)),
                  pl.BlockSpec((LANES, 1), lambda i: (0, 0))],
        out_specs=[pl.BlockSpec((TOP_K, tm), lambda i: (0, i)),
                   pl.BlockSpec((TOP_K, tm), lambda i: (0, i))],
        out_shape=[jax.ShapeDtypeStruct((TOP_K, m), I32),
                   jax.ShapeDtypeStruct((TOP_K, m), F32)],
        compiler_params=_cparams(1, VMEM_LIMIT),
        name="moe_route",
    )(x, g, w_t, b_col)


def _sort_kernel(ids_ref, pos_ref, te_ref, nu_ref, rs_ref, rc_ref, rn_ref, *, ts):
    n_tok = ids_ref.shape[1]
    nck = n_tok // ts
    iota_e = lax.broadcasted_iota(I32, (N_EXPERTS, ts), 0)

    def chunk(c):
        return pl.ds(pl.multiple_of(c * ts, ts), ts)

    def count_body(c, acc):
        for k in range(TOP_K):
            oh = jnp.where(iota_e == ids_ref[pl.ds(k, 1), chunk(c)], 1.0, 0.0)
            acc = acc + jnp.sum(oh, axis=1, keepdims=True)
        return acc

    counts = lax.fori_loop(0, nck, count_body, jnp.zeros((N_EXPERTS, 1), F32))
    er = lax.broadcasted_iota(I32, (N_EXPERTS, N_EXPERTS), 0)
    ec = lax.broadcasted_iota(I32, (N_EXPERTS, N_EXPERTS), 1)
    strict_lower = jnp.where(ec < er, 1.0, 0.0).astype(BF16)

    def excl_cumsum(v):
        vb = jnp.broadcast_to(v, (N_EXPERTS, LANES)).astype(BF16)
        return jnp.dot(strict_lower, vb, preferred_element_type=F32)

    padded = jnp.floor((counts + (ROW_ALIGN - 1)) * (1.0 / ROW_ALIGN)) * ROW_ALIGN
    hi = jnp.floor(padded * (1.0 / LANES))
    start = excl_cumsum(hi) * LANES + excl_cumsum(padded - hi * LANES)
    chunks = jnp.floor((counts + (MOE_TM - 0.5)) * (1.0 / MOE_TM))
    first = excl_cumsum(chunks)
    ends = first + chunks
    total = jnp.max(ends, axis=0, keepdims=True)
    chunk_i = lax.broadcasted_iota(I32, (N_EXPERTS, LANES), 1).astype(F32)
    chunk_c = jnp.minimum(chunk_i, total - 1.0)
    te = jnp.sum(jnp.where(ends <= chunk_c, 1.0, 0.0), axis=0, keepdims=True)
    mine = lax.broadcasted_iota(I32, (N_EXPERTS, LANES), 0).astype(F32) == te
    rs = jnp.sum(jnp.where(mine, start + (chunk_c - first) * MOE_TM, 0.0), axis=0, keepdims=True)
    te_ref[...] = te.astype(I32)
    nu_ref[...] = total.astype(I32)
    rs_ref[...] = rs.astype(I32)
    left = jnp.sum(jnp.where(mine, padded - (chunk_c - first) * MOE_TM, 0.0), axis=0, keepdims=True)
    rc_ref[...] = jnp.clip(left, 0.0, float(MOE_TM)).astype(I32)
    real = jnp.sum(jnp.where(mine, counts - (chunk_c - first) * MOE_TM, 0.0), axis=0, keepdims=True)
    rn_ref[...] = jnp.clip(real, 0.0, float(MOE_TM)).astype(I32)

    row_base = start[:, 0:1]
    ur = lax.broadcasted_iota(I32, (ts, ts), 0)
    uc = lax.broadcasted_iota(I32, (ts, ts), 1)
    upper = jnp.where(ur <= uc, 1.0, 0.0).astype(BF16)

    def pos_body(k):
        def body(c, carry):
            hit = iota_e == ids_ref[pl.ds(k, 1), chunk(c)]
            incl = jnp.dot(jnp.where(hit, 1.0, 0.0).astype(BF16), upper,
                           preferred_element_type=F32)
            val = row_base + carry + incl - 1.0
            p = jnp.sum(jnp.where(hit, val, 0.0), axis=0, keepdims=True)
            pos_ref[pl.ds(k, 1), chunk(c)] = p.astype(I32)
            return carry + incl[:, ts - 1:ts]
        return body

    carry = jnp.zeros((N_EXPERTS, 1), F32)
    for k in range(TOP_K):
        carry = lax.fori_loop(0, nck, pos_body(k), carry)


def _sort_call(ids, *, ts=512):
    n_tok = ids.shape[1]
    ts = min(ts, n_tok)
    return pl.pallas_call(
        functools.partial(_sort_kernel, ts=ts),
        out_shape=[jax.ShapeDtypeStruct((TOP_K, n_tok), I32),
                   jax.ShapeDtypeStruct((1, LANES), I32),
                   jax.ShapeDtypeStruct((1, LANES), I32),
                   jax.ShapeDtypeStruct((1, LANES), I32),
                   jax.ShapeDtypeStruct((1, LANES), I32),
                   jax.ShapeDtypeStruct((1, LANES), I32)],
        name="moe_sort",
    )(ids)


def _sorted_rows(n_tok):
    return TOP_K * n_tok + N_EXPERTS * ROW_ALIGN


def _moe_chunks(n_tok):
    return -(-(TOP_K * n_tok) // MOE_TM) + N_EXPERTS


INVERT_BATCH = 16


def _invert_kernel(pos_ref, inv_ref):
    def init(i, c):
        for u in range(INVERT_BATCH):
            inv_ref[i * INVERT_BATCH + u] = jnp.int32(0)
        return c
    lax.fori_loop(0, inv_ref.shape[0] // INVERT_BATCH, init, 0)

    def put(i, c):
        base = i * INVERT_BATCH
        dst = [pos_ref[base + u] for u in range(INVERT_BATCH)]
        for u in range(INVERT_BATCH):
            inv_ref[dst[u]] = base + u
        return c
    lax.fori_loop(0, pos_ref.shape[0] // INVERT_BATCH, put, 0)


def _invert_call(pos_flat, n_rows):
    assert n_rows % INVERT_BATCH == 0 and pos_flat.shape[0] % INVERT_BATCH == 0
    return pl.pallas_call(
        _invert_kernel,
        in_specs=[pl.BlockSpec(memory_space=pltpu.SMEM)],
        out_specs=pl.BlockSpec(memory_space=pltpu.SMEM),
        out_shape=jax.ShapeDtypeStruct((n_rows,), I32),
        name="moe_invert",
    )(pos_flat)


def _moe_kernel(te_ref, nu_ref, rs_ref, rc_ref, rn_ref, inv_ref, x_hbm, g_ref, wg_ref, wu_ref,
                wd_ref, y_hbm, xbuf, rscale, acc, gu, wgu, gsem, ssem, *, n_tok):
    i = pl.program_id(0)
    k = pl.program_id(1)
    nk = pl.num_programs(1)
    n_used = nu_ref[0]
    valid = i < n_used
    slot = i % 2
    n_slots = TOP_K * n_tok

    def x_row_copy(tok, r, sl):
        return pltpu.make_async_copy(x_hbm.at[pl.ds(tok, 1)], xbuf.at[sl, pl.ds(r, 1)], gsem.at[sl])

    def x_gather(c, sl):
        base = rs_ref[c]

        def one(r):
            s = inv_ref[base + r]
            x_row_copy(jnp.where(s >= n_tok, s - n_tok, s), r, sl).start()

        def group(g, carry):
            for u in range(DMA_GROUP):
                one(g * DMA_GROUP + u)
            return carry
        n = rn_ref[c]
        n_groups = n >> 3
        lax.fori_loop(0, n_groups, group, 0)

        def rest(r, carry):
            one(r)
            return carry
        lax.fori_loop(n_groups * DMA_GROUP, n, rest, 0)

    def x_wait(c, sl):
        n = rn_ref[c]
        n8 = pl.multiple_of((n >> 3) << 3, ROW_ALIGN)

        @pl.when(n8 > 0)
        def _():
            pltpu.make_async_copy(x_hbm.at[pl.ds(0, n8)], xbuf.at[sl, pl.ds(0, n8)],
                                  gsem.at[sl]).wait()

        def rest(r, carry):
            x_row_copy(0, 0, sl).wait()
            return carry
        lax.fori_loop(0, n - n8, rest, 0)

    def y_copy(c):
        n = pl.multiple_of(rc_ref[c], ROW_ALIGN)
        dst = pl.ds(pl.multiple_of(rs_ref[c], ROW_ALIGN), n)
        return pltpu.make_async_copy(acc.at[pl.ds(0, n)], y_hbm.at[dst], ssem.at[0])

    @pl.when((i == 0) & (k == 0))
    def _():
        xbuf[...] = jnp.zeros(xbuf.shape, xbuf.dtype)
        x_gather(0, 0)
        acc[...] = jnp.zeros(acc.shape, acc.dtype)
        n_rows = y_hbm.shape[0]
        tails = [(off, min(MOE_TM, n_rows - off)) for off in range(n_slots, n_rows, MOE_TM)]
        for off, size in tails:
            pltpu.make_async_copy(acc.at[pl.ds(0, size)], y_hbm.at[pl.ds(off, size)],
                                  ssem.at[0]).start()
        for off, size in tails:
            pltpu.make_async_copy(acc.at[pl.ds(0, size)], y_hbm.at[pl.ds(off, size)],
                                  ssem.at[0]).wait()

    @pl.when(valid & (k == 0))
    def _():
        x_wait(i, slot)

        @pl.when(i + 1 < n_used)
        def _():
            x_gather(i + 1, 1 - slot)

        x = xbuf[slot]
        rscale[...] = lax.rsqrt(jnp.mean(x * x, axis=-1, keepdims=True) + NORM_EPS)

    @pl.when(valid)
    def _():
        wgu[:, :D_EXPERT] = wg_ref[0].astype(BF16)
        wgu[:, D_EXPERT:] = wu_ref[0].astype(BF16)
        cols = pl.ds(pl.multiple_of(k * MOE_TK, MOE_TK), MOE_TK)
        xs = (xbuf[slot, :, cols] * rscale[...] * g_ref[:, cols]).astype(BF16)
        part = jnp.dot(xs, wgu[...], preferred_element_type=F32)

        @pl.when(k == 0)
        def _():
            gu[...] = part

        @pl.when(k > 0)
        def _():
            gu[...] += part

        @pl.when(k == nk - 1)
        def _():
            a = gu[:, :D_EXPERT]
            hid = (a * jax.nn.sigmoid(a)) * gu[:, D_EXPERT:]
            out = jnp.dot(hid.astype(BF16), wd_ref[0].astype(BF16), preferred_element_type=F32)

            @pl.when(i > 0)
            def _():
                y_copy(i - 1).wait()
            acc[...] = out
            y_copy(i).start()

            @pl.when(i == n_used - 1)
            def _():
                y_copy(i).wait()


def _moe_call(te, nu, rs, rc, rn, inv, x, g, w_gate, w_up, w_down):
    n_tok = x.shape[0]
    nk = D_MODEL // MOE_TK
    last_k = nk - 1

    def w_in_map(i, k, te_r, nu_r, *_):
        return te_r[i], jnp.where(i < nu_r[0], k, last_k), 0

    def w_out_map(i, k, te_r, *_):
        return te_r[i], 0, 0

    grid_spec = pltpu.PrefetchScalarGridSpec(
        num_scalar_prefetch=6,
        grid=(_moe_chunks(n_tok), nk),
        in_specs=[
            pl.BlockSpec(memory_space=pl.ANY),
            pl.BlockSpec((1, D_MODEL), lambda i, k, *_: (0, 0)),
            pl.BlockSpec((1, MOE_TK, D_EXPERT), w_in_map),
            pl.BlockSpec((1, MOE_TK, D_EXPERT), w_in_map),
            pl.BlockSpec((1, D_EXPERT, D_MODEL), w_out_map),
        ],
        out_specs=pl.BlockSpec(memory_space=pl.ANY),
        scratch_shapes=[pltpu.VMEM((2, MOE_TM, D_MODEL), F32),
                        pltpu.VMEM((MOE_TM, 1), F32),
                        pltpu.VMEM((MOE_TM, D_MODEL), F32),
                        pltpu.VMEM((MOE_TM, 2 * D_EXPERT), F32),
                        pltpu.VMEM((MOE_TK, 2 * D_EXPERT), BF16),
                        pltpu.SemaphoreType.DMA((2,)),
                        pltpu.SemaphoreType.DMA((1,))],
    )
    return pl.pallas_call(
        functools.partial(_moe_kernel, n_tok=n_tok),
        grid_spec=grid_spec,
        out_shape=jax.ShapeDtypeStruct((_sorted_rows(n_tok), D_MODEL), F32),
        compiler_params=_cparams(2, VMEM_LIMIT),
        name="moe_experts",
    )(te, nu, rs, rc, rn, inv, x, g, w_gate, w_up, w_down)


def _combine_kernel(pos_ref, x_ref, w_ref, g_ref, y_hbm, xo_ref, *rest, n_tok):
    *h_ref, ybuf, sem = rest
    i = pl.program_id(0)
    nb = pl.num_programs(0)
    tm = x_ref.shape[0]
    slot = i % 2

    def fetch(tile, sl):
        def group(g, c):
            for u in range(DMA_GROUP):
                r = g * DMA_GROUP + u
                for k in range(TOP_K):
                    src = pos_ref[k * n_tok + tile * tm + r]
                    pltpu.make_async_copy(y_hbm.at[pl.ds(src, 1)], ybuf.at[sl, k, pl.ds(r, 1)],
                                          sem.at[sl]).start()
            return c
        lax.fori_loop(0, tm // DMA_GROUP, group, 0)

    @pl.when(i == 0)
    def _():
        fetch(0, 0)

    @pl.when(i + 1 < nb)
    def _():
        fetch(i + 1, 1 - slot)

    for k in range(TOP_K):
        pltpu.make_async_copy(y_hbm.at[pl.ds(0, tm)], ybuf.at[slot, k], sem.at[slot]).wait()
    w = w_ref[...]
    x = x_ref[...] + w[:, 0:1] * ybuf[slot, 0] + w[:, 1:2] * ybuf[slot, 1]
    xo_ref[...] = x
    if h_ref:
        h_ref[0][...] = (_rms(x) * g_ref[...]).astype(BF16)


def _combine_call(pos_flat, x, y, wts_t, g_next, *, tm=256):
    m, d = x.shape
    with_norm = g_next is not None
    g = g_next.reshape(1, d) if with_norm else jnp.ones((1, d), F32)
    row = pl.BlockSpec((tm, d), lambda i, pos_r: (i, 0))
    out_specs = [row, row] if with_norm else [row]
    out_shape = [jax.ShapeDtypeStruct((m, d), F32)]
    if with_norm:
        out_shape.append(jax.ShapeDtypeStruct((m, d), BF16))
    grid_spec = pltpu.PrefetchScalarGridSpec(
        num_scalar_prefetch=1,
        grid=(m // tm,),
        in_specs=[row,
                  pl.BlockSpec((tm, TOP_K), lambda i, pos_r: (i, 0)),
                  pl.BlockSpec((1, d), lambda i, pos_r: (0, 0)),
                  pl.BlockSpec(memory_space=pl.ANY)],
        out_specs=out_specs,
        scratch_shapes=[pltpu.VMEM((2, TOP_K, tm, d), F32),
                        pltpu.SemaphoreType.DMA((2,))],
    )
    return pl.pallas_call(
        functools.partial(_combine_kernel, n_tok=m),
        grid_spec=grid_spec,
        out_shape=out_shape,
        compiler_params=_cparams(1, VMEM_LIMIT),
        name="moe_combine",
    )(pos_flat, x, wts_t, g, y)


def _moe_layer(x, norm_g, w_group, b_group, w_router, b_router, w_gate, w_up, w_down, g_next):
    n_tok = x.shape[0]
    pad_rows = LANES - N_GROUPS - N_EXPERTS
    w_t = jnp.concatenate([w_group.T, w_router.T, jnp.zeros((pad_rows, D_MODEL), F32)], axis=0)
    b_col = jnp.concatenate([b_group, b_router, jnp.zeros((pad_rows,), F32)]).reshape(LANES, 1)
    g_row = norm_g.reshape(1, D_MODEL)
    ids, wts = _route_call(x, g_row, w_t, b_col)
    pos, te, nu, rs, rc, rn = _sort_call(ids)
    pos_flat = pos.reshape(-1)
    inv = _invert_call(pos_flat, _sorted_rows(n_tok))
    y = _moe_call(te.reshape(-1), nu.reshape(-1)[:1], rs.reshape(-1), rc.reshape(-1),
                  rn.reshape(-1), inv, x, g_row,
                  w_gate.reshape(N_EXPERTS, D_MODEL, D_EXPERT),
                  w_up.reshape(N_EXPERTS, D_MODEL, D_EXPERT),
                  w_down.reshape(N_EXPERTS, D_EXPERT, D_MODEL))
    return _combine_call(pos_flat, x, y, wts.T, g_next)


def _rope_tables(pos_flat, dim):
    inv_freq = ROPE_THETA ** (-jnp.arange(0, dim, 2, dtype=F32) / dim)
    ang = pos_flat.astype(F32)[:, None] * inv_freq
    return jnp.cos(ang), jnp.sin(ang)


def kernel(x, mem, positions, mem_norm_g, w_mem_kv, mem_k_norm_g, l0_attn_norm_g, l0_w_in, l0_q_norm_g, l0_k_norm_g, l0_lambda_q1, l0_lambda_k1, l0_lambda_q2, l0_lambda_k2, l0_subln_g, l0_cross_q_norm_g, l0_w_out, l0_ffn_norm_g, l0_w_group, l0_b_group, l0_w_router, l0_b_router, l0_w_gate, l0_w_up, l0_w_down, l1_attn_norm_g, l1_w_in, l1_cq_norm_g, l1_ckv_norm_g, l1_w_uq, l1_w_ukv, l1_q_norm_g, l1_k_norm_g, l1_cross_q_norm_g, l1_w_out, l1_ffn_norm_g, l1_w_group, l1_b_group, l1_w_router, l1_b_router, l1_w_gate, l1_w_up, l1_w_down):
    batch, seq, d = x.shape
    n_tok = batch * seq
    xf = x.reshape(n_tok, d)
    pos_flat = positions.reshape(n_tok)
    posq = pos_flat.reshape(n_tok, 1)
    posk = pos_flat.reshape(1, n_tok)
    ones128 = jnp.ones((n_tok, LANES), F32)

    c64, s64 = _rope_tables(pos_flat, DIFF_HEAD_DIM)
    cos_full = jnp.concatenate([c64, c64], axis=1)
    sin_full = jnp.concatenate([-s64, s64], axis=1)
    c32, s32 = _rope_tables(pos_flat, MLA_ROPE_DIM)
    z32 = jnp.zeros_like(c32)
    cos_k = jnp.concatenate([c32, c32, z32, z32], axis=1)
    sin_k = jnp.concatenate([-s32, s32, z32, z32], axis=1)

    cross_scale = CROSS_HEAD_DIM ** -0.5 * LOG2E

    memn = _norm_call(mem.reshape(batch * N_MEM, d), mem_norm_g)
    mem_gain = jnp.concatenate([jnp.tile(mem_k_norm_g, CROSS_HEADS), jnp.ones((CROSS_WIDTH,), F32)])
    k_tiles = CROSS_WIDTH // 512
    memkv = _mm_call(memn, w_mem_kv, mem_gain.reshape(1, -1), ones128[:batch * N_MEM],
                     ones128[:batch * N_MEM], ((0, k_tiles, "head256"), (k_tiles, 2 * k_tiles, "plain")),
                     tm=batch * N_MEM, tn=512, name="mem_kv")

    h0 = _norm_call(xf, l0_attn_norm_g)
    qk_tiles = SELF_WIDTH // 512
    gain0 = jnp.concatenate([
        jnp.tile(l0_q_norm_g, 2 * DIFF_HEADS) * (DIFF_HEAD_DIM ** -0.5 * LOG2E),
        jnp.tile(l0_k_norm_g, 2 * DIFF_HEADS),
        jnp.ones((SELF_WIDTH,), F32),
        jnp.tile(l0_cross_q_norm_g, CROSS_HEADS) * cross_scale]).reshape(1, -1)
    proj0 = _mm_call(h0, l0_w_in, gain0, cos_full, sin_full,
                     ((0, 2 * qk_tiles, "head128_rope"), (2 * qk_tiles, 3 * qk_tiles, "plain"),
                      (3 * qk_tiles, 3 * qk_tiles + CROSS_WIDTH // 512, "head256")),
                     tm=1024, tn=512, name="l0_in_proj")
    lam_init = 0.8 - 0.6 * math.exp(-0.3 * 0)
    row = lambda v: v.reshape(1, -1)
    o_self = _attn_call(proj0, proj0, proj0, posq, posk,
                        (row(l0_lambda_q1), row(l0_lambda_k1), row(l0_lambda_q2),
                         row(l0_lambda_k2), row(l0_subln_g)),
                        batch=batch, seq=seq, heads=DIFF_HEADS, n_maps=2, dk=DIFF_HEAD_DIM,
                        dv=DIFF_V_DIM, qcol=0, kcol=DIFF_HEADS, vcol=2 * DIFF_HEADS,
                        lam_init=lam_init, name="diff_attn")
    o_cross = _cross_call(proj0, memkv, batch=batch, seq=seq, qcol=3 * SELF_WIDTH // CROSS_WIDTH)
    x1 = _outproj_call(o_self, o_cross, l0_w_out, xf)
    x2, h1 = _moe_layer(x1, l0_ffn_norm_g, l0_w_group, l0_b_group, l0_w_router, l0_b_router,
                        l0_w_gate, l0_w_up, l0_w_down, l1_attn_norm_g)

    a = MLA_Q_RANK
    b = a + MLA_KV_RANK
    c = b + MLA_ROPE_DIM
    wcat = jnp.concatenate([l1_w_in[:, :b], l1_w_in[:, c:], l1_w_in[:, b:c],
                            jnp.zeros((d, LANES - MLA_ROPE_DIM), F32)], axis=1).astype(BF16)
    cq, ckv, qm, kpe = _mla_in_call(
        h1, wcat, row(l1_cq_norm_g), row(l1_ckv_norm_g),
        row(jnp.tile(l1_cross_q_norm_g, CROSS_HEADS) * cross_scale))
    w_uq3 = l1_w_uq.reshape(MLA_Q_RANK, MLA_HEADS, MLA_QK_DIM)
    w_q_nope = w_uq3[:, :, :MLA_NOPE_DIM].reshape(MLA_Q_RANK, -1).astype(BF16)
    partner = (jnp.arange(MLA_ROPE_DIM) + MLA_ROPE_DIM // 2) % MLA_ROPE_DIM
    w_pe3 = w_uq3[:, :, MLA_NOPE_DIM:]
    pad3 = jnp.zeros_like(w_pe3)
    w_q_pe = jnp.concatenate([w_pe3, pad3], axis=2).reshape(MLA_Q_RANK, -1).astype(BF16)
    w_q_rot = jnp.concatenate([w_pe3[:, :, partner], pad3], axis=2).reshape(MLA_Q_RANK, -1).astype(BF16)
    q_scale = MLA_QK_DIM ** -0.5 * LOG2E
    zeros64 = jnp.zeros((MLA_ROPE_DIM,), F32)
    g_q_pe = l1_q_norm_g[MLA_NOPE_DIM:] * q_scale
    q_pad = _mla_q_call(cq, w_q_nope, w_q_pe, w_q_rot,
                        row(l1_q_norm_g[:MLA_NOPE_DIM] * q_scale),
                        row(jnp.concatenate([g_q_pe, zeros64])),
                        row(jnp.concatenate([g_q_pe[partner], zeros64])), cos_k, sin_k)
    k_pad, v1 = _mla_kv_call(ckv, l1_w_ukv.astype(BF16), kpe,
                             row(l1_k_norm_g[:MLA_NOPE_DIM]),
                             row(jnp.concatenate([l1_k_norm_g[MLA_NOPE_DIM:], zeros64])),
                             cos_k, sin_k)
    o_self1 = _attn_call(q_pad, k_pad, v1, posq, posk, (),
                         batch=batch, seq=seq, heads=MLA_HEADS, n_maps=1, dk=MLA_PAD_DIM,
                         dv=MLA_V_DIM, qcol=0, kcol=0, vcol=0, lam_init=0.0, name="mla_attn")
    o_cross1 = _cross_call(qm, memkv, batch=batch, seq=seq, qcol=0)
    x3 = _outproj_call(o_self1, o_cross1, l1_w_out, x2)
    (x4,) = _moe_layer(x3, l1_ffn_norm_g, l1_w_group, l1_b_group, l1_w_router, l1_b_router,
                       l1_w_gate, l1_w_up, l1_w_down, None)
    return x4.reshape(batch, seq, d)
```

```python
import functools
import math

import jax
import jax.numpy as jnp
from jax import lax
from jax.experimental import pallas as pl
from jax.experimental.pallas import tpu as pltpu

F32 = jnp.float32
BF16 = jnp.bfloat16
I32 = jnp.int32

D_MODEL = 4096
N_MEM = 256
ROPE_THETA = 10000.0
NORM_EPS = 1e-6
LOG2E = math.log2(math.e)

SELF_WIDTH = 3 * D_MODEL // 4
CROSS_HEADS = 4
CROSS_HEAD_DIM = (D_MODEL // 4) // CROSS_HEADS
CROSS_WIDTH = CROSS_HEADS * CROSS_HEAD_DIM

DIFF_HEAD_DIM = 128
DIFF_V_DIM = 2 * DIFF_HEAD_DIM
DIFF_HEADS = SELF_WIDTH // DIFF_V_DIM

MLA_NOPE_DIM = 128
MLA_ROPE_DIM = 64
MLA_V_DIM = 128
MLA_HEADS = SELF_WIDTH // MLA_V_DIM
MLA_QK_DIM = MLA_NOPE_DIM + MLA_ROPE_DIM
MLA_Q_RANK = 3 * D_MODEL // 16
MLA_KV_RANK = D_MODEL // 8
MLA_PAD_DIM = 256

N_GROUPS = 8
EXPERTS_PER_GROUP = 8
N_EXPERTS = N_GROUPS * EXPERTS_PER_GROUP
D_EXPERT = 3 * D_MODEL // 32
TOP_K = 2

LANES = 128
MOE_TM = 320
MOE_TK = 2048
DMA_GROUP = 8
ROW_ALIGN = 8
ATTN_TQ = 256
VMEM_LIMIT = 56 * 1024 * 1024


def _cparams(n_axes, vmem=None):
    return pltpu.CompilerParams(dimension_semantics=("arbitrary",) * n_axes,
                                vmem_limit_bytes=vmem)


def _rms(x):
    return x * lax.rsqrt(jnp.mean(x * x, axis=-1, keepdims=True) + NORM_EPS)


def _dot_nt(a, b):
    return lax.dot_general(a, b, (((1,), (1,)), ((), ())), preferred_element_type=F32)


def _norm_kernel(x_ref, g_ref, o_ref):
    o_ref[...] = (_rms(x_ref[...]) * g_ref[...]).astype(o_ref.dtype)


def _norm_call(x, g, tm=256):
    m, d = x.shape
    return pl.pallas_call(
        _norm_kernel,
        grid=(m // tm,),
        in_specs=[pl.BlockSpec((tm, d), lambda i: (i, 0)),
                  pl.BlockSpec((1, d), lambda i: (0, 0))],
        out_specs=pl.BlockSpec((tm, d), lambda i: (i, 0)),
        out_shape=jax.ShapeDtypeStruct((m, d), BF16),
        compiler_params=_cparams(1),
        name="rmsnorm",
    )(x, g.reshape(1, d))


def _epilogue(kind, acc, rows, gain_ref, cos_ref, sin_ref, o_ref):
    tn = acc.shape[1]
    if kind == "plain":
        o_ref[rows, :] = acc.astype(o_ref.dtype)
    elif kind == "head128_rope":
        c = cos_ref[rows, :]
        s = sin_ref[rows, :]
        for j in range(tn // 128):
            sl = slice(j * 128, (j + 1) * 128)
            y = _rms(acc[:, sl]) * gain_ref[:, sl]
            y = y * c + pltpu.roll(y, 64, axis=1) * s
            o_ref[rows, sl] = y.astype(o_ref.dtype)
    elif kind == "head256":
        for j in range(tn // 256):
            sl = slice(j * 256, (j + 1) * 256)
            o_ref[rows, sl] = (_rms(acc[:, sl]) * gain_ref[:, sl]).astype(o_ref.dtype)
    else:
        raise ValueError(kind)


MM_SUB = 512


def _mm_kernel(x_ref, w_ref, gain_ref, cos_ref, sin_ref, o_ref, wbf_ref, *, kinds):
    n = pl.program_id(0)

    @pl.when(pl.program_id(1) == 0)
    def _():
        wbf_ref[...] = w_ref[...].astype(BF16)

    def run(kind):
        for r in range(x_ref.shape[0] // MM_SUB):
            rows = slice(r * MM_SUB, (r + 1) * MM_SUB)
            acc = jnp.dot(x_ref[rows, :], wbf_ref[...], preferred_element_type=F32)
            _epilogue(kind, acc, rows, gain_ref, cos_ref, sin_ref, o_ref)

    if len(kinds) == 1:
        run(kinds[0][2])
    else:
        for lo, hi, kind in kinds:
            pl.when((n >= lo) & (n < hi))(functools.partial(run, kind))


def _mm_call(x, w, gain, cos, sin, kinds, *, tm, tn, name):
    m, k = x.shape
    n = w.shape[1]
    return pl.pallas_call(
        functools.partial(_mm_kernel, kinds=kinds),
        grid=(n // tn, m // tm),
        in_specs=[pl.BlockSpec((tm, k), lambda j, i: (i, 0)),
                  pl.BlockSpec((k, tn), lambda j, i: (0, j)),
                  pl.BlockSpec((1, tn), lambda j, i: (0, j)),
                  pl.BlockSpec((tm, LANES), lambda j, i: (i, 0)),
                  pl.BlockSpec((tm, LANES), lambda j, i: (i, 0))],
        out_specs=pl.BlockSpec((tm, tn), lambda j, i: (i, j)),
        out_shape=jax.ShapeDtypeStruct((m, n), BF16),
        scratch_shapes=[pltpu.VMEM((k, tn), BF16)],
        compiler_params=_cparams(2, VMEM_LIMIT),
        name=name,
    )(x, w, gain, cos, sin)


def _attn_kernel(*refs, n_maps, dk, tq, nq, lam_init):
    if n_maps == 2:
        (q_ref, k_ref, v_ref, posq_ref, posk_ref, lq1_ref, lk1_ref, lq2_ref, lk2_ref, subg_ref,
         o_ref) = refs
        lam = (jnp.exp(jnp.sum(lq1_ref[...] * lk1_ref[...], axis=-1, keepdims=True))
               - jnp.exp(jnp.sum(lq2_ref[...] * lk2_ref[...], axis=-1, keepdims=True))
               + lam_init)
    else:
        q_ref, k_ref, v_ref, posq_ref, posk_ref, o_ref = refs

    for i in range(nq):
        rows = slice(i * tq, (i + 1) * tq)
        n_past = i * tq
        mask = posk_ref[:, rows] <= posq_ref[rows, :]
        probs = []
        for mi in range(n_maps):
            cols = slice(mi * dk, (mi + 1) * dk)
            qm = q_ref[rows, cols]
            s_diag = jnp.where(mask, _dot_nt(qm, k_ref[rows, cols]), -jnp.inf)
            mx = jnp.max(s_diag, axis=-1, keepdims=True)
            if n_past:
                s_past = _dot_nt(qm, k_ref[0:n_past, cols])
                mx = jnp.maximum(mx, jnp.max(s_past, axis=-1, keepdims=True))
            p_diag = jnp.exp2(s_diag - mx)
            den = jnp.sum(p_diag, axis=-1, keepdims=True)
            p_past = None
            if n_past:
                p_past = jnp.exp2(s_past - mx)
                den = den + jnp.sum(p_past, axis=-1, keepdims=True)
            probs.append((p_diag, p_past, 1.0 / den))

        def pv(w_diag, w_past):
            o = jnp.dot(w_diag.astype(BF16), v_ref[rows, :], preferred_element_type=F32)
            if n_past:
                o = o + jnp.dot(w_past.astype(BF16), v_ref[0:n_past, :],
                                preferred_element_type=F32)
            return o

        if n_maps == 2:
            (d1, p1, r1), (d2, p2, r2) = probs
            c2 = lam * r2
            o = pv(d1 * r1 - d2 * c2, (p1 * r1 - p2 * c2) if n_past else None)
            o = _rms(o) * subg_ref[...] * (1.0 - lam_init)
        else:
            (d1, p1, r1), = probs
            o = pv(d1, p1) * r1
        o_ref[rows, :] = o.astype(o_ref.dtype)


def _attn_call(q, k, v, posq, posk, extra, *, batch, seq, heads, n_maps, dk, dv,
               qcol, kcol, vcol, lam_init, name):
    tq = ATTN_TQ
    in_specs = [
        pl.BlockSpec((seq, n_maps * dk), lambda b, h: (b, qcol + h)),
        pl.BlockSpec((seq, n_maps * dk), lambda b, h: (b, kcol + h)),
        pl.BlockSpec((seq, dv), lambda b, h: (b, vcol + h)),
        pl.BlockSpec((seq, 1), lambda b, h: (b, 0)),
        pl.BlockSpec((1, seq), lambda b, h: (0, b)),
    ] + [pl.BlockSpec(e.shape, lambda b, h: (0, 0)) for e in extra]
    return pl.pallas_call(
        functools.partial(_attn_kernel, n_maps=n_maps, dk=dk, tq=tq, nq=seq // tq,
                          lam_init=lam_init),
        grid=(batch, heads),
        in_specs=in_specs,
        out_specs=pl.BlockSpec((seq, dv), lambda b, h: (b, h)),
        out_shape=jax.ShapeDtypeStruct((batch * seq, heads * dv), BF16),
        compiler_params=_cparams(2, VMEM_LIMIT),
        name=name,
    )(q, k, v, posq, posk, *extra)


def _cross_kernel(q_ref, k_ref, v_ref, o_ref):
    for h in range(CROSS_HEADS):
        sl = slice(h * CROSS_HEAD_DIM, (h + 1) * CROSS_HEAD_DIM)
        s = _dot_nt(q_ref[:, sl], k_ref[:, sl])
        p = jnp.exp2(s - jnp.max(s, axis=-1, keepdims=True))
        inv = 1.0 / jnp.sum(p, axis=-1, keepdims=True)
        o = jnp.dot(p.astype(BF16), v_ref[:, sl], preferred_element_type=F32)
        o_ref[:, sl] = (o * inv).astype(o_ref.dtype)


def _cross_call(q, memkv, *, batch, seq, qcol, tq=512):
    nq = seq // tq
    return pl.pallas_call(
        _cross_kernel,
        grid=(batch, nq),
        in_specs=[pl.BlockSpec((tq, CROSS_WIDTH), lambda b, i: (b * nq + i, qcol)),
                  pl.BlockSpec((N_MEM, CROSS_WIDTH), lambda b, i: (b, 0)),
                  pl.BlockSpec((N_MEM, CROSS_WIDTH), lambda b, i: (b, 1))],
        out_specs=pl.BlockSpec((tq, CROSS_WIDTH), lambda b, i: (b * nq + i, 0)),
        out_shape=jax.ShapeDtypeStruct((batch * seq, CROSS_WIDTH), BF16),
        compiler_params=_cparams(2),
        name="cross_attn",
    )(q, memkv, memkv)


def _outproj_kernel(os_ref, oc_ref, w_ref, x_ref, o_ref, wbf_ref):
    @pl.when(pl.program_id(1) == 0)
    def _():
        wbf_ref[...] = w_ref[...].astype(BF16)

    acc = jnp.dot(os_ref[...], wbf_ref[:SELF_WIDTH, :], preferred_element_type=F32)
    acc = acc + jnp.dot(oc_ref[...], wbf_ref[SELF_WIDTH:, :], preferred_element_type=F32)
    o_ref[...] = x_ref[...] + acc


def _outproj_call(o_self, o_cross, w_out, x, *, tm=1024, tn=512):
    m = x.shape[0]
    return pl.pallas_call(
        _outproj_kernel,
        grid=(D_MODEL // tn, m // tm),
        in_specs=[pl.BlockSpec((tm, SELF_WIDTH), lambda j, i: (i, 0)),
                  pl.BlockSpec((tm, CROSS_WIDTH), lambda j, i: (i, 0)),
                  pl.BlockSpec((D_MODEL, tn), lambda j, i: (0, j)),
                  pl.BlockSpec((tm, tn), lambda j, i: (i, j))],
        out_specs=pl.BlockSpec((tm, tn), lambda j, i: (i, j)),
        out_shape=jax.ShapeDtypeStruct((m, D_MODEL), F32),
        scratch_shapes=[pltpu.VMEM((D_MODEL, tn), BF16)],
        compiler_params=_cparams(2, VMEM_LIMIT),
        name="out_proj",
    )(o_self, o_cross, w_out, x)


def _mla_in_kernel(h_ref, w_ref, gcq_ref, gckv_ref, gqm_ref, cq_ref, ckv_ref, qm_ref, kpe_ref):
    a = MLA_Q_RANK
    b = a + MLA_KV_RANK
    c = b + CROSS_WIDTH
    sub = h_ref.shape[0] // 2
    for r in range(2):
        rows = slice(r * sub, (r + 1) * sub)
        acc = jnp.dot(h_ref[rows, :], w_ref[...], preferred_element_type=F32)
        cq_ref[rows, :] = (_rms(acc[:, :a]) * gcq_ref[...]).astype(cq_ref.dtype)
        ckv_ref[rows, :] = (_rms(acc[:, a:b]) * gckv_ref[...]).astype(ckv_ref.dtype)
        for j in range(CROSS_HEADS):
            sl = slice(j * CROSS_HEAD_DIM, (j + 1) * CROSS_HEAD_DIM)
            qm_ref[rows, sl] = (_rms(acc[:, b + j * CROSS_HEAD_DIM:b + (j + 1) * CROSS_HEAD_DIM])
                                * gqm_ref[:, sl]).astype(qm_ref.dtype)
        kpe_ref[rows, :] = acc[:, c:]


def _mla_in_call(h, wcat, gcq, gckv, gqm, *, tm=512):
    m = h.shape[0]
    ncat = wcat.shape[1]
    row = lambda w: pl.BlockSpec((tm, w), lambda i: (i, 0))
    full = lambda r, w: pl.BlockSpec((r, w), lambda i: (0, 0))
    w_spec = pl.BlockSpec((D_MODEL, ncat), lambda i: (0, 0), pipeline_mode=pl.Buffered(1))
    return pl.pallas_call(
        _mla_in_kernel,
        grid=(m // tm,),
        in_specs=[row(D_MODEL), w_spec, full(1, MLA_Q_RANK), full(1, MLA_KV_RANK),
                  full(1, CROSS_WIDTH)],
        out_specs=[row(MLA_Q_RANK), row(MLA_KV_RANK), row(CROSS_WIDTH), row(LANES)],
        out_shape=[jax.ShapeDtypeStruct((m, MLA_Q_RANK), BF16),
                   jax.ShapeDtypeStruct((m, MLA_KV_RANK), BF16),
                   jax.ShapeDtypeStruct((m, CROSS_WIDTH), BF16),
                   jax.ShapeDtypeStruct((m, LANES), F32)],
        compiler_params=_cparams(1, VMEM_LIMIT),
        name="mla_in_proj",
    )(h, wcat, gcq, gckv, gqm)


MLA_GROUP = 4


def _rope_pe(x, c, s):
    lane = lax.broadcasted_iota(I32, x.shape, 1)
    partner = jnp.where((lane & 63) < 32, pltpu.roll(x, 96, axis=1), pltpu.roll(x, 32, axis=1))
    return x * c + partner * s


def _mla_q_kernel(cq_ref, wn_ref, wp_ref, wr_ref, gn_ref, gp_ref, gr_ref, cos_ref, sin_ref, o_ref):
    x = cq_ref[...]
    an = jnp.dot(x, wn_ref[...], preferred_element_type=F32)
    ap = jnp.dot(x, wp_ref[...], preferred_element_type=F32)
    ar = jnp.dot(x, wr_ref[...], preferred_element_type=F32)
    gc = cos_ref[...] * gp_ref[...]
    gs = sin_ref[...] * gr_ref[...]
    for j in range(MLA_GROUP):
        sl = slice(j * 128, (j + 1) * 128)
        nj = an[:, sl]
        pj = ap[:, sl]
        ss = jnp.sum(nj * nj + pj * pj, axis=-1, keepdims=True)
        rs = lax.rsqrt(ss * (1.0 / MLA_QK_DIM) + NORM_EPS)
        o_ref[:, j * 256:j * 256 + 128] = (nj * rs * gn_ref[...]).astype(o_ref.dtype)
        o_ref[:, j * 256 + 128:(j + 1) * 256] = ((pj * gc + ar[:, sl] * gs) * rs).astype(o_ref.dtype)


def _mla_q_call(cq, w_nope, w_pe, w_rot, gn, gp, gr, cos, sin, *, tm=1024):
    m = cq.shape[0]
    g = MLA_HEADS // MLA_GROUP
    tm = min(tm, m)
    wspec = pl.BlockSpec((MLA_Q_RANK, MLA_GROUP * 128), lambda j, i: (0, j))
    gspec = pl.BlockSpec((1, LANES), lambda j, i: (0, 0))
    tspec = pl.BlockSpec((tm, LANES), lambda j, i: (i, 0))
    return pl.pallas_call(
        _mla_q_kernel,
        grid=(g, m // tm),
        in_specs=[pl.BlockSpec((tm, MLA_Q_RANK), lambda j, i: (i, 0)),
                  wspec, wspec, wspec, gspec, gspec, gspec, tspec, tspec],
        out_specs=pl.BlockSpec((tm, MLA_GROUP * MLA_PAD_DIM), lambda j, i: (i, j)),
        out_shape=jax.ShapeDtypeStruct((m, MLA_HEADS * MLA_PAD_DIM), BF16),
        compiler_params=_cparams(2),
        name="mla_q_up",
    )(cq, w_nope, w_pe, w_rot, gn, gp, gr, cos, sin)


def _mla_kv_kernel(ckv_ref, w_ref, kpe_ref, gn_ref, gp_ref, cos_ref, sin_ref, k_ref, v_ref):
    acc = jnp.dot(ckv_ref[...], w_ref[...], preferred_element_type=F32)
    kpe = kpe_ref[...]
    ss_pe = jnp.sum(kpe * kpe, axis=-1, keepdims=True)
    pe = _rope_pe(kpe * gp_ref[...], cos_ref[...], sin_ref[...])
    for j in range(MLA_GROUP):
        kn = acc[:, j * 256:j * 256 + 128]
        ss = jnp.sum(kn * kn, axis=-1, keepdims=True) + ss_pe
        rs = lax.rsqrt(ss * (1.0 / MLA_QK_DIM) + NORM_EPS)
        k_ref[:, j * 256:j * 256 + 128] = (kn * rs * gn_ref[...]).astype(k_ref.dtype)
        k_ref[:, j * 256 + 128:(j + 1) * 256] = (pe * rs).astype(k_ref.dtype)
        v_ref[:, j * 128:(j + 1) * 128] = acc[:, j * 256 + 128:(j + 1) * 256].astype(v_ref.dtype)


def _mla_kv_call(ckv, w_ukv, kpe, gn, gp, cos, sin, *, tm=1024):
    m = ckv.shape[0]
    g = MLA_HEADS // MLA_GROUP
    tm = min(tm, m)
    return pl.pallas_call(
        _mla_kv_kernel,
        grid=(m // tm, g),
        in_specs=[pl.BlockSpec((tm, MLA_KV_RANK), lambda i, j: (i, 0)),
                  pl.BlockSpec((MLA_KV_RANK, MLA_GROUP * 256), lambda i, j: (0, j)),
                  pl.BlockSpec((tm, LANES), lambda i, j: (i, 0)),
                  pl.BlockSpec((1, LANES), lambda i, j: (0, 0)),
                  pl.BlockSpec((1, LANES), lambda i, j: (0, 0)),
                  pl.BlockSpec((tm, LANES), lambda i, j: (i, 0)),
                  pl.BlockSpec((tm, LANES), lambda i, j: (i, 0))],
        out_specs=[pl.BlockSpec((tm, MLA_GROUP * MLA_PAD_DIM), lambda i, j: (i, j)),
                   pl.BlockSpec((tm, MLA_GROUP * MLA_V_DIM), lambda i, j: (i, j))],
        out_shape=[jax.ShapeDtypeStruct((m, MLA_HEADS * MLA_PAD_DIM), BF16),
                   jax.ShapeDtypeStruct((m, MLA_HEADS * MLA_V_DIM), BF16)],
        compiler_params=_cparams(2),
        name="mla_kv_up",
    )(ckv, w_ukv, kpe, gn, gp, cos, sin)


def _route_kernel(x_ref, g_ref, w_ref, b_ref, ids_ref, wts_ref):
    xn = _rms(x_ref[...]) * g_ref[...]
    x_hi = xn.astype(BF16)
    x_lo = (xn - x_hi.astype(F32)).astype(BF16)
    w = w_ref[...]
    w_hi = w.astype(BF16)
    w_lo = (w - w_hi.astype(F32)).astype(BF16)
    lg = _dot_nt(w_hi, x_hi) + _dot_nt(w_hi, x_lo) + _dot_nt(w_lo, x_hi) + b_ref[...]
    tm = lg.shape[1]
    iota = lax.broadcasted_iota(I32, (EXPERTS_PER_GROUP, tm), 0)

    def first_argmax(v):
        mx = jnp.max(v, axis=0, keepdims=True)
        idx = jnp.min(jnp.where(v == mx, iota, EXPERTS_PER_GROUP), axis=0, keepdims=True)
        return mx, idx

    gl = lg[0:N_GROUPS, :]
    gmax, gsel = first_argmax(gl)
    g_gate = 1.0 / jnp.sum(jnp.exp(gl - gmax), axis=0, keepdims=True)
    el = jnp.zeros((EXPERTS_PER_GROUP, tm), F32)
    for g in range(N_GROUPS):
        lo = N_GROUPS + g * EXPERTS_PER_GROUP
        el = jnp.where(gsel == g, lg[lo:lo + EXPERTS_PER_GROUP, :], el)
    v1, i1 = first_argmax(el)
    v2, i2 = first_argmax(jnp.where(iota == i1, -jnp.inf, el))
    e = jnp.exp(v2 - v1)
    w1 = g_gate / (1.0 + e)
    w2 = g_gate * e / (1.0 + e)
    ids_ref[...] = jnp.concatenate([gsel * EXPERTS_PER_GROUP + i1,
                                    gsel * EXPERTS_PER_GROUP + i2], axis=0)
    wts_ref[...] = jnp.concatenate([w1, w2], axis=0)


def _route_call(x, g, w_t, b_col, *, tm=256):
    m = x.shape[0]
    return pl.pallas_call(
        _route_kernel,
        grid=(m // tm,),
        in_specs=[pl.BlockSpec((tm, D_MODEL), lambda i: (i, 0)),
                  pl.BlockSpec((1, D_MODEL), lambda i: (0, 0)),
                  pl.BlockSpec((LANES, D_MODEL), lambda i: (0, 0)),
                  pl.BlockSpec((LANES, 1), lambda i: (0, 0))],
        out_specs=[pl.BlockSpec((TOP_K, tm), lambda i: (0, i)),
                   pl.BlockSpec((TOP_K, tm), lambda i: (0, i))],
        out_shape=[jax.ShapeDtypeStruct((TOP_K, m), I32),
                   jax.ShapeDtypeStruct((TOP_K, m), F32)],
        compiler_params=_cparams(1, VMEM_LIMIT),
        name="moe_route",
    )(x, g, w_t, b_col)


def _sort_kernel(ids_ref, pos_ref, te_ref, nu_ref, rs_ref, rc_ref, rn_ref, *, ts):
    n_tok = ids_ref.shape[1]
    nck = n_tok // ts
    iota_e = lax.broadcasted_iota(I32, (N_EXPERTS, ts), 0)

    def chunk(c):
        return pl.ds(pl.multiple_of(c * ts, ts), ts)

    def count_body(c, acc):
        for k in range(TOP_K):
            oh = jnp.where(iota_e == ids_ref[pl.ds(k, 1), chunk(c)], 1.0, 0.0)
            acc = acc + jnp.sum(oh, axis=1, keepdims=True)
        return acc

    counts = lax.fori_loop(0, nck, count_body, jnp.zeros((N_EXPERTS, 1), F32))
    er = lax.broadcasted_iota(I32, (N_EXPERTS, N_EXPERTS), 0)
    ec = lax.broadcasted_iota(I32, (N_EXPERTS, N_EXPERTS), 1)
    strict_lower = jnp.where(ec < er, 1.0, 0.0).astype(BF16)

    def excl_cumsum(v):
        vb = jnp.broadcast_to(v, (N_EXPERTS, LANES)).astype(BF16)
        return jnp.dot(strict_lower, vb, preferred_element_type=F32)

    padded = jnp.floor((counts + (ROW_ALIGN - 1)) * (1.0 / ROW_ALIGN)) * ROW_ALIGN
    hi = jnp.floor(padded * (1.0 / LANES))
    start = excl_cumsum(hi) * LANES + excl_cumsum(padded - hi * LANES)
    chunks = jnp.floor((counts + (MOE_TM - 0.5)) * (1.0 / MOE_TM))
    first = excl_cumsum(chunks)
    ends = first + chunks
    total = jnp.max(ends, axis=0, keepdims=True)
    chunk_i = lax.broadcasted_iota(I32, (N_EXPERTS, LANES), 1).astype(F32)
    chunk_c = jnp.minimum(chunk_i, total - 1.0)
    te = jnp.sum(jnp.where(ends <= chunk_c, 1.0, 0.0), axis=0, keepdims=True)
    mine = lax.broadcasted_iota(I32, (N_EXPERTS, LANES), 0).astype(F32) == te
    rs = jnp.sum(jnp.where(mine, start + (chunk_c - first) * MOE_TM, 0.0), axis=0, keepdims=True)
    te_ref[...] = te.astype(I32)
    nu_ref[...] = total.astype(I32)
    rs_ref[...] = rs.astype(I32)
    left = jnp.sum(jnp.where(mine, padded - (chunk_c - first) * MOE_TM, 0.0), axis=0, keepdims=True)
    rc_ref[...] = jnp.clip(left, 0.0, float(MOE_TM)).astype(I32)
    real = jnp.sum(jnp.where(mine, counts - (chunk_c - first) * MOE_TM, 0.0), axis=0, keepdims=True)
    rn_ref[...] = jnp.clip(real, 0.0, float(MOE_TM)).astype(I32)

    row_base = start[:, 0:1]
    ur = lax.broadcasted_iota(I32, (ts, ts), 0)
    uc = lax.broadcasted_iota(I32, (ts, ts), 1)
    upper = jnp.where(ur <= uc, 1.0, 0.0).astype(BF16)

    def pos_body(k):
        def body(c, carry):
            hit = iota_e == ids_ref[pl.ds(k, 1), chunk(c)]
            incl = jnp.dot(jnp.where(hit, 1.0, 0.0).astype(BF16), upper,
                           preferred_element_type=F32)
            val = row_base + carry + incl - 1.0
            p = jnp.sum(jnp.where(hit, val, 0.0), axis=0, keepdims=True)
            pos_ref[pl.ds(k, 1), chunk(c)] = p.astype(I32)
            return carry + incl[:, ts - 1:ts]
        return body

    carry = jnp.zeros((N_EXPERTS, 1), F32)
    for k in range(TOP_K):
        carry = lax.fori_loop(0, nck, pos_body(k), carry)


def _sort_call(ids, *, ts=512):
    n_tok = ids.shape[1]
    ts = min(ts, n_tok)
    return pl.pallas_call(
        functools.partial(_sort_kernel, ts=ts),
        out_shape=[jax.ShapeDtypeStruct((TOP_K, n_tok), I32),
                   jax.ShapeDtypeStruct((1, LANES), I32),
                   jax.ShapeDtypeStruct((1, LANES), I32),
                   jax.ShapeDtypeStruct((1, LANES), I32),
                   jax.ShapeDtypeStruct((1, LANES), I32),
                   jax.ShapeDtypeStruct((1, LANES), I32)],
        name="moe_sort",
    )(ids)


def _sorted_rows(n_tok):
    return TOP_K * n_tok + N_EXPERTS * ROW_ALIGN


def _moe_chunks(n_tok):
    return -(-(TOP_K * n_tok) // MOE_TM) + N_EXPERTS


INVERT_BATCH = 16


def _invert_kernel(pos_ref, inv_ref):
    def init(i, c):
        for u in range(INVERT_BATCH):
            inv_ref[i * INVERT_BATCH + u] = jnp.int32(0)
        return c
    lax.fori_loop(0, inv_ref.shape[0] // INVERT_BATCH, init, 0)

    def put(i, c):
        base = i * INVERT_BATCH
        dst = [pos_ref[base + u] for u in range(INVERT_BATCH)]
        for u in range(INVERT_BATCH):
            inv_ref[dst[u]] = base + u
        return c
    lax.fori_loop(0, pos_ref.shape[0] // INVERT_BATCH, put, 0)


def _invert_call(pos_flat, n_rows):
    assert n_rows % INVERT_BATCH == 0 and pos_flat.shape[0] % INVERT_BATCH == 0
    return pl.pallas_call(
        _invert_kernel,
        in_specs=[pl.BlockSpec(memory_space=pltpu.SMEM)],
        out_specs=pl.BlockSpec(memory_space=pltpu.SMEM),
        out_shape=jax.ShapeDtypeStruct((n_rows,), I32),
        name="moe_invert",
    )(pos_flat)


def _moe_kernel(te_ref, nu_ref, rs_ref, rc_ref, rn_ref, inv_ref, x_hbm, g_ref, wg_ref, wu_ref,
                wd_ref, y_hbm, xbuf, rscale, acc, gu, wgu, gsem, ssem, *, n_tok):
    i = pl.program_id(0)
    k = pl.program_id(1)
    nk = pl.num_programs(1)
    n_used = nu_ref[0]
    valid = i < n_used
    slot = i % 2
    n_slots = TOP_K * n_tok

    def x_row_copy(tok, r, sl):
        return pltpu.make_async_copy(x_hbm.at[pl.ds(tok, 1)], xbuf.at[sl, pl.ds(r, 1)], gsem.at[sl])

    def x_gather(c, sl):
        base = rs_ref[c]

        def one(r):
            s = inv_ref[base + r]
            x_row_copy(jnp.where(s >= n_tok, s - n_tok, s), r, sl).start()

        def group(g, carry):
            for u in range(DMA_GROUP):
                one(g * DMA_GROUP + u)
            return carry
        n = rn_ref[c]
        n_groups = n >> 3
        lax.fori_loop(0, n_groups, group, 0)

        def rest(r, carry):
            one(r)
            return carry
        lax.fori_loop(n_groups * DMA_GROUP, n, rest, 0)

    def x_wait(c, sl):
        n = rn_ref[c]
        n8 = pl.multiple_of((n >> 3) << 3, ROW_ALIGN)

        @pl.when(n8 > 0)
        def _():
            pltpu.make_async_copy(x_hbm.at[pl.ds(0, n8)], xbuf.at[sl, pl.ds(0, n8)],
                                  gsem.at[sl]).wait()

        def rest(r, carry):
            x_row_copy(0, 0, sl).wait()
            return carry
        lax.fori_loop(0, n - n8, rest, 0)

    def y_copy(c):
        n = pl.multiple_of(rc_ref[c], ROW_ALIGN)
        dst = pl.ds(pl.multiple_of(rs_ref[c], ROW_ALIGN), n)
        return pltpu.make_async_copy(acc.at[pl.ds(0, n)], y_hbm.at[dst], ssem.at[0])

    @pl.when((i == 0) & (k == 0))
    def _():
        xbuf[...] = jnp.zeros(xbuf.shape, xbuf.dtype)
        x_gather(0, 0)
        acc[...] = jnp.zeros(acc.shape, acc.dtype)
        n_rows = y_hbm.shape[0]
        tails = [(off, min(MOE_TM, n_rows - off)) for off in range(n_slots, n_rows, MOE_TM)]
        for off, size in tails:
            pltpu.make_async_copy(acc.at[pl.ds(0, size)], y_hbm.at[pl.ds(off, size)],
                                  ssem.at[0]).start()
        for off, size in tails:
            pltpu.make_async_copy(acc.at[pl.ds(0, size)], y_hbm.at[pl.ds(off, size)],
                                  ssem.at[0]).wait()

    @pl.when(valid & (k == 0))
    def _():
        x_wait(i, slot)

        @pl.when(i + 1 < n_used)
        def _():
            x_gather(i + 1, 1 - slot)

        x = xbuf[slot]
        rscale[...] = lax.rsqrt(jnp.mean(x * x, axis=-1, keepdims=True) + NORM_EPS)

    @pl.when(valid)
    def _():
        wgu[:, :D_EXPERT] = wg_ref[0].astype(BF16)
        wgu[:, D_EXPERT:] = wu_ref[0].astype(BF16)
        cols = pl.ds(pl.multiple_of(k * MOE_TK, MOE_TK), MOE_TK)
        xs = (xbuf[slot, :, cols] * rscale[...] * g_ref[:, cols]).astype(BF16)
        part = jnp.dot(xs, wgu[...], preferred_element_type=F32)

        @pl.when(k == 0)
        def _():
            gu[...] = part

        @pl.when(k > 0)
        def _():
            gu[...] += part

        @pl.when(k == nk - 1)
        def _():
            a = gu[:, :D_EXPERT]
            hid = (a * jax.nn.sigmoid(a)) * gu[:, D_EXPERT:]
            out = jnp.dot(hid.astype(BF16), wd_ref[0].astype(BF16), preferred_element_type=F32)

            @pl.when(i > 0)
            def _():
                y_copy(i - 1).wait()
            acc[...] = out
            y_copy(i).start()

            @pl.when(i == n_used - 1)
            def _():
                y_copy(i).wait()


def _moe_call(te, nu, rs, rc, rn, inv, x, g, w_gate, w_up, w_down):
    n_tok = x.shape[0]
    nk = D_MODEL // MOE_TK
    last_k = nk - 1

    def w_in_map(i, k, te_r, nu_r, *_):
        return te_r[i], jnp.where(i < nu_r[0], k, last_k), 0

    def w_out_map(i, k, te_r, *_):
        return te_r[i], 0, 0

    grid_spec = pltpu.PrefetchScalarGridSpec(
        num_scalar_prefetch=6,
        grid=(_moe_chunks(n_tok), nk),
        in_specs=[
            pl.BlockSpec(memory_space=pl.ANY),
            pl.BlockSpec((1, D_MODEL), lambda i, k, *_: (0, 0)),
            pl.BlockSpec((1, MOE_TK, D_EXPERT), w_in_map),
            pl.BlockSpec((1, MOE_TK, D_EXPERT), w_in_map),
            pl.BlockSpec((1, D_EXPERT, D_MODEL), w_out_map),
        ],
        out_specs=pl.BlockSpec(memory_space=pl.ANY),
        scratch_shapes=[pltpu.VMEM((2, MOE_TM, D_MODEL), F32),
                        pltpu.VMEM((MOE_TM, 1), F32),
                        pltpu.VMEM((MOE_TM, D_MODEL), F32),
                        pltpu.VMEM((MOE_TM, 2 * D_EXPERT), F32),
                        pltpu.VMEM((MOE_TK, 2 * D_EXPERT), BF16),
                        pltpu.SemaphoreType.DMA((2,)),
                        pltpu.SemaphoreType.DMA((1,))],
    )
    return pl.pallas_call(
        functools.partial(_moe_kernel, n_tok=n_tok),
        grid_spec=grid_spec,
        out_shape=jax.ShapeDtypeStruct((_sorted_rows(n_tok), D_MODEL), F32),
        compiler_params=_cparams(2, VMEM_LIMIT),
        name="moe_experts",
    )(te, nu, rs, rc, rn, inv, x, g, w_gate, w_up, w_down)


def _combine_kernel(pos_ref, x_ref, w_ref, g_ref, y_hbm, xo_ref, *rest, n_tok):
    *h_ref, ybuf, sem = rest
    i = pl.program_id(0)
    nb = pl.num_programs(0)
    tm = x_ref.shape[0]
    slot = i % 2

    def fetch(tile, sl):
        def group(g, c):
            for u in range(DMA_GROUP):
                r = g * DMA_GROUP + u
                for k in range(TOP_K):
                    src = pos_ref[k * n_tok + tile * tm + r]
                    pltpu.make_async_copy(y_hbm.at[pl.ds(src, 1)], ybuf.at[sl, k, pl.ds(r, 1)],
                                          sem.at[sl]).start()
            return c
        lax.fori_loop(0, tm // DMA_GROUP, group, 0)

    @pl.when(i == 0)
    def _():
        fetch(0, 0)

    @pl.when(i + 1 < nb)
    def _():
        fetch(i + 1, 1 - slot)

    for k in range(TOP_K):
        pltpu.make_async_copy(y_hbm.at[pl.ds(0, tm)], ybuf.at[slot, k], sem.at[slot]).wait()
    w = w_ref[...]
    x = x_ref[...] + w[:, 0:1] * ybuf[slot, 0] + w[:, 1:2] * ybuf[slot, 1]
    xo_ref[...] = x
    if h_ref:
        h_ref[0][...] = (_rms(x) * g_ref[...]).astype(BF16)


def _combine_call(pos_flat, x, y, wts_t, g_next, *, tm=256):
    m, d = x.shape
    with_norm = g_next is not None
    g = g_next.reshape(1, d) if with_norm else jnp.ones((1, d), F32)
    row = pl.BlockSpec((tm, d), lambda i, pos_r: (i, 0))
    out_specs = [row, row] if with_norm else [row]
    out_shape = [jax.ShapeDtypeStruct((m, d), F32)]
    if with_norm:
        out_shape.append(jax.ShapeDtypeStruct((m, d), BF16))
    grid_spec = pltpu.PrefetchScalarGridSpec(
        num_scalar_prefetch=1,
        grid=(m // tm,),
        in_specs=[row,
                  pl.BlockSpec((tm, TOP_K), lambda i, pos_r: (i, 0)),
                  pl.BlockSpec((1, d), lambda i, pos_r: (0, 0)),
                  pl.BlockSpec(memory_space=pl.ANY)],
        out_specs=out_specs,
        scratch_shapes=[pltpu.VMEM((2, TOP_K, tm, d), F32),
                        pltpu.SemaphoreType.DMA((2,))],
    )
    return pl.pallas_call(
        functools.partial(_combine_kernel, n_tok=m),
        grid_spec=grid_spec,
        out_shape=out_shape,
        compiler_params=_cparams(1, VMEM_LIMIT),
        name="moe_combine",
    )(pos_flat, x, wts_t, g, y)


def _moe_layer(x, norm_g, w_group, b_group, w_router, b_router, w_gate, w_up, w_down, g_next):
    n_tok = x.shape[0]
    pad_rows = LANES - N_GROUPS - N_EXPERTS
    w_t = jnp.concatenate([w_group.T, w_router.T, jnp.zeros((pad_rows, D_MODEL), F32)], axis=0)
    b_col = jnp.concatenate([b_group, b_router, jnp.zeros((pad_rows,), F32)]).reshape(LANES, 1)
    g_row = norm_g.reshape(1, D_MODEL)
    ids, wts = _route_call(x, g_row, w_t, b_col)
    pos, te, nu, rs, rc, rn = _sort_call(ids)
    pos_flat = pos.reshape(-1)
    inv = _invert_call(pos_flat, _sorted_rows(n_tok))
    y = _moe_call(te.reshape(-1), nu.reshape(-1)[:1], rs.reshape(-1), rc.reshape(-1),
                  rn.reshape(-1), inv, x, g_row,
                  w_gate.reshape(N_EXPERTS, D_MODEL, D_EXPERT),
                  w_up.reshape(N_EXPERTS, D_MODEL, D_EXPERT),
                  w_down.reshape(N_EXPERTS, D_EXPERT, D_MODEL))
    return _combine_call(pos_flat, x, y, wts.T, g_next)


def _rope_tables(pos_flat, dim):
    inv_freq = ROPE_THETA ** (-jnp.arange(0, dim, 2, dtype=F32) / dim)
    ang = pos_flat.astype(F32)[:, None] * inv_freq
    return jnp.cos(ang), jnp.sin(ang)


def kernel(x, mem, positions, mem_norm_g, w_mem_kv, mem_k_norm_g, l0_attn_norm_g, l0_w_in, l0_q_norm_g, l0_k_norm_g, l0_lambda_q1, l0_lambda_k1, l0_lambda_q2, l0_lambda_k2, l0_subln_g, l0_cross_q_norm_g, l0_w_out, l0_ffn_norm_g, l0_w_group, l0_b_group, l0_w_router, l0_b_router, l0_w_gate, l0_w_up, l0_w_down, l1_attn_norm_g, l1_w_in, l1_cq_norm_g, l1_ckv_norm_g, l1_w_uq, l1_w_ukv, l1_q_norm_g, l1_k_norm_g, l1_cross_q_norm_g, l1_w_out, l1_ffn_norm_g, l1_w_group, l1_b_group, l1_w_router, l1_b_router, l1_w_gate, l1_w_up, l1_w_down):
    batch, seq, d = x.shape
    n_tok = batch * seq
    xf = x.reshape(n_tok, d)
    pos_flat = positions.reshape(n_tok)
    posq = pos_flat.reshape(n_tok, 1)
    posk = pos_flat.reshape(1, n_tok)
    ones128 = jnp.ones((n_tok, LANES), F32)

    c64, s64 = _rope_tables(pos_flat, DIFF_HEAD_DIM)
    cos_full = jnp.concatenate([c64, c64], axis=1)
    sin_full = jnp.concatenate([-s64, s64], axis=1)
    c32, s32 = _rope_tables(pos_flat, MLA_ROPE_DIM)
    z32 = jnp.zeros_like(c32)
    cos_k = jnp.concatenate([c32, c32, z32, z32], axis=1)
    sin_k = jnp.concatenate([-s32, s32, z32, z32], axis=1)

    cross_scale = CROSS_HEAD_DIM ** -0.5 * LOG2E

    memn = _norm_call(mem.reshape(batch * N_MEM, d), mem_norm_g)
    mem_gain = jnp.concatenate([jnp.tile(mem_k_norm_g, CROSS_HEADS), jnp.ones((CROSS_WIDTH,), F32)])
    k_tiles = CROSS_WIDTH // 512
    memkv = _mm_call(memn, w_mem_kv, mem_gain.reshape(1, -1), ones128[:batch * N_MEM],
                     ones128[:batch * N_MEM], ((0, k_tiles, "head256"), (k_tiles, 2 * k_tiles, "plain")),
                     tm=batch * N_MEM, tn=512, name="mem_kv")

    h0 = _norm_call(xf, l0_attn_norm_g)
    qk_tiles = SELF_WIDTH // 512
    gain0 = jnp.concatenate([
        jnp.tile(l0_q_norm_g, 2 * DIFF_HEADS) * (DIFF_HEAD_DIM ** -0.5 * LOG2E),
        jnp.tile(l0_k_norm_g, 2 * DIFF_HEADS),
        jnp.ones((SELF_WIDTH,), F32),
        jnp.tile(l0_cross_q_norm_g, CROSS_HEADS) * cross_scale]).reshape(1, -1)
    proj0 = _mm_call(h0, l0_w_in, gain0, cos_full, sin_full,
                     ((0, 2 * qk_tiles, "head128_rope"), (2 * qk_tiles, 3 * qk_tiles, "plain"),
                      (3 * qk_tiles, 3 * qk_tiles + CROSS_WIDTH // 512, "head256")),
                     tm=1024, tn=512, name="l0_in_proj")
    lam_init = 0.8 - 0.6 * math.exp(-0.3 * 0)
    row = lambda v: v.reshape(1, -1)
    o_self = _attn_call(proj0, proj0, proj0, posq, posk,
                        (row(l0_lambda_q1), row(l0_lambda_k1), row(l0_lambda_q2),
                         row(l0_lambda_k2), row(l0_subln_g)),
                        batch=batch, seq=seq, heads=DIFF_HEADS, n_maps=2, dk=DIFF_HEAD_DIM,
                        dv=DIFF_V_DIM, qcol=0, kcol=DIFF_HEADS, vcol=2 * DIFF_HEADS,
                        lam_init=lam_init, name="diff_attn")
    o_cross = _cross_call(proj0, memkv, batch=batch, seq=seq, qcol=3 * SELF_WIDTH // CROSS_WIDTH)
    x1 = _outproj_call(o_self, o_cross, l0_w_out, xf)
    x2, h1 = _moe_layer(x1, l0_ffn_norm_g, l0_w_group, l0_b_group, l0_w_router, l0_b_router,
                        l0_w_gate, l0_w_up, l0_w_down, l1_attn_norm_g)

    a = MLA_Q_RANK
    b = a + MLA_KV_RANK
    c = b + MLA_ROPE_DIM
    w_in_b = l1_w_in.astype(BF16)
    wcat = jnp.concatenate([w_in_b[:, :b], w_in_b[:, c:], w_in_b[:, b:c],
                            jnp.zeros((d, LANES - MLA_ROPE_DIM), BF16)], axis=1)
    cq, ckv, qm, kpe = _mla_in_call(
        h1, wcat, row(l1_cq_norm_g), row(l1_ckv_norm_g),
        row(jnp.tile(l1_cross_q_norm_g, CROSS_HEADS) * cross_scale))
    w_uq3 = l1_w_uq.astype(BF16).reshape(MLA_Q_RANK, MLA_HEADS, MLA_QK_DIM)
    w_q_nope = w_uq3[:, :, :MLA_NOPE_DIM].reshape(MLA_Q_RANK, -1)
    partner = (jnp.arange(MLA_ROPE_DIM) + MLA_ROPE_DIM // 2) % MLA_ROPE_DIM
    w_pe3 = w_uq3[:, :, MLA_NOPE_DIM:]
    pad3 = jnp.zeros_like(w_pe3)
    w_q_pe = jnp.concatenate([w_pe3, pad3], axis=2).reshape(MLA_Q_RANK, -1)
    half = MLA_ROPE_DIM // 2
    w_q_rot = jnp.concatenate([w_pe3[:, :, half:], w_pe3[:, :, :half], pad3],
                              axis=2).reshape(MLA_Q_RANK, -1)
    q_scale = MLA_QK_DIM ** -0.5 * LOG2E
    zeros64 = jnp.zeros((MLA_ROPE_DIM,), F32)
    g_q_pe = l1_q_norm_g[MLA_NOPE_DIM:] * q_scale
    q_pad = _mla_q_call(cq, w_q_nope, w_q_pe, w_q_rot,
                        row(l1_q_norm_g[:MLA_NOPE_DIM] * q_scale),
                        row(jnp.concatenate([g_q_pe, zeros64])),
                        row(jnp.concatenate([g_q_pe[partner], zeros64])), cos_k, sin_k)
    k_pad, v1 = _mla_kv_call(ckv, l1_w_ukv.astype(BF16), kpe,
                             row(l1_k_norm_g[:MLA_NOPE_DIM]),
                             row(jnp.concatenate([l1_k_norm_g[MLA_NOPE_DIM:], zeros64])),
                             cos_k, sin_k)
    o_self1 = _attn_call(q_pad, k_pad, v1, posq, posk, (),
                         batch=batch, seq=seq, heads=MLA_HEADS, n_maps=1, dk=MLA_PAD_DIM,
                         dv=MLA_V_DIM, qcol=0, kcol=0, vcol=0, lam_init=0.0, name="mla_attn")
    o_cross1 = _cross_call(qm, memkv, batch=batch, seq=seq, qcol=0)
    x3 = _outproj_call(o_self1, o_cross1, l1_w_out, x2)
    (x4,) = _moe_layer(x3, l1_ffn_norm_g, l1_w_group, l1_b_group, l1_w_router, l1_b_router,
                       l1_w_gate, l1_w_up, l1_w_down, None)
    return x4.reshape(batch, seq, d)
```

```python
import functools
import math

import jax
import jax.numpy as jnp
from jax import lax
from jax.experimental import pallas as pl
from jax.experimental.pallas import tpu as pltpu

F32 = jnp.float32
BF16 = jnp.bfloat16
I32 = jnp.int32

D_MODEL = 4096
N_MEM = 256
ROPE_THETA = 10000.0
NORM_EPS = 1e-6
LOG2E = math.log2(math.e)

SELF_WIDTH = 3 * D_MODEL // 4
CROSS_HEADS = 4
CROSS_HEAD_DIM = (D_MODEL // 4) // CROSS_HEADS
CROSS_WIDTH = CROSS_HEADS * CROSS_HEAD_DIM

DIFF_HEAD_DIM = 128
DIFF_V_DIM = 2 * DIFF_HEAD_DIM
DIFF_HEADS = SELF_WIDTH // DIFF_V_DIM

MLA_NOPE_DIM = 128
MLA_ROPE_DIM = 64
MLA_V_DIM = 128
MLA_HEADS = SELF_WIDTH // MLA_V_DIM
MLA_QK_DIM = MLA_NOPE_DIM + MLA_ROPE_DIM
MLA_Q_RANK = 3 * D_MODEL // 16
MLA_KV_RANK = D_MODEL // 8
MLA_PAD_DIM = 256

N_GROUPS = 8
EXPERTS_PER_GROUP = 8
N_EXPERTS = N_GROUPS * EXPERTS_PER_GROUP
D_EXPERT = 3 * D_MODEL // 32
TOP_K = 2

LANES = 128
MOE_TM = 320
MOE_TK = 2048
DMA_GROUP = 8
ROW_ALIGN = 8
ATTN_TQ = 256
VMEM_LIMIT = 56 * 1024 * 1024


def _cparams(n_axes, vmem=None):
    return pltpu.CompilerParams(dimension_semantics=("arbitrary",) * n_axes,
                                vmem_limit_bytes=vmem)


def _rms(x):
    return x * lax.rsqrt(jnp.mean(x * x, axis=-1, keepdims=True) + NORM_EPS)


def _dot_nt(a, b):
    return lax.dot_general(a, b, (((1,), (1,)), ((), ())), preferred_element_type=F32)


def _norm_kernel(x_ref, g_ref, o_ref):
    o_ref[...] = (_rms(x_ref[...]) * g_ref[...]).astype(o_ref.dtype)


def _norm_call(x, g, tm=256):
    m, d = x.shape
    return pl.pallas_call(
        _norm_kernel,
        grid=(m // tm,),
        in_specs=[pl.BlockSpec((tm, d), lambda i: (i, 0)),
                  pl.BlockSpec((1, d), lambda i: (0, 0))],
        out_specs=pl.BlockSpec((tm, d), lambda i: (i, 0)),
        out_shape=jax.ShapeDtypeStruct((m, d), BF16),
        compiler_params=_cparams(1),
        name="rmsnorm",
    )(x, g.reshape(1, d))


def _epilogue(kind, acc, rows, gain_ref, cos_ref, sin_ref, o_ref):
    tn = acc.shape[1]
    if kind == "plain":
        o_ref[rows, :] = acc.astype(o_ref.dtype)
    elif kind == "head128_rope":
        c = cos_ref[rows, :]
        s = sin_ref[rows, :]
        for j in range(tn // 128):
            sl = slice(j * 128, (j + 1) * 128)
            y = _rms(acc[:, sl]) * gain_ref[:, sl]
            y = y * c + pltpu.roll(y, 64, axis=1) * s
            o_ref[rows, sl] = y.astype(o_ref.dtype)
    elif kind == "head256":
        for j in range(tn // 256):
            sl = slice(j * 256, (j + 1) * 256)
            o_ref[rows, sl] = (_rms(acc[:, sl]) * gain_ref[:, sl]).astype(o_ref.dtype)
    else:
        raise ValueError(kind)


MM_SUB = 256


def _mm_kernel(x_ref, w_ref, gain_ref, cos_ref, sin_ref, o_ref, wbf_ref, *, kinds):
    n = pl.program_id(0)

    @pl.when(pl.program_id(1) == 0)
    def _():
        wbf_ref[...] = w_ref[...].astype(BF16)

    def run(kind):
        for r in range(x_ref.shape[0] // MM_SUB):
            rows = slice(r * MM_SUB, (r + 1) * MM_SUB)
            acc = jnp.dot(x_ref[rows, :], wbf_ref[...], preferred_element_type=F32)
            _epilogue(kind, acc, rows, gain_ref, cos_ref, sin_ref, o_ref)

    if len(kinds) == 1:
        run(kinds[0][2])
    else:
        for lo, hi, kind in kinds:
            pl.when((n >= lo) & (n < hi))(functools.partial(run, kind))


def _mm_call(x, w, gain, cos, sin, kinds, *, tm, tn, name):
    m, k = x.shape
    n = w.shape[1]
    return pl.pallas_call(
        functools.partial(_mm_kernel, kinds=kinds),
        grid=(n // tn, m // tm),
        in_specs=[pl.BlockSpec((tm, k), lambda j, i: (i, 0)),
                  pl.BlockSpec((k, tn), lambda j, i: (0, j)),
                  pl.BlockSpec((1, tn), lambda j, i: (0, j)),
                  pl.BlockSpec((tm, LANES), lambda j, i: (i, 0)),
                  pl.BlockSpec((tm, LANES), lambda j, i: (i, 0))],
        out_specs=pl.BlockSpec((tm, tn), lambda j, i: (i, j)),
        out_shape=jax.ShapeDtypeStruct((m, n), BF16),
        scratch_shapes=[pltpu.VMEM((k, tn), BF16)],
        compiler_params=_cparams(2, VMEM_LIMIT),
        name=name,
    )(x, w, gain, cos, sin)


def _attn_kernel(*refs, n_maps, dk, tq, nq, lam_init):
    if n_maps == 2:
        (q_ref, k_ref, v_ref, posq_ref, posk_ref, lq1_ref, lk1_ref, lq2_ref, lk2_ref, subg_ref,
         o_ref) = refs
        lam = (jnp.exp(jnp.sum(lq1_ref[...] * lk1_ref[...], axis=-1, keepdims=True))
               - jnp.exp(jnp.sum(lq2_ref[...] * lk2_ref[...], axis=-1, keepdims=True))
               + lam_init)
    else:
        q_ref, k_ref, v_ref, posq_ref, posk_ref, o_ref = refs

    for i in range(nq):
        rows = slice(i * tq, (i + 1) * tq)
        n_past = i * tq
        mask = posk_ref[:, rows] <= posq_ref[rows, :]
        probs = []
        for mi in range(n_maps):
            cols = slice(mi * dk, (mi + 1) * dk)
            qm = q_ref[rows, cols]
            s_diag = jnp.where(mask, _dot_nt(qm, k_ref[rows, cols]), -jnp.inf)
            mx = jnp.max(s_diag, axis=-1, keepdims=True)
            if n_past:
                s_past = _dot_nt(qm, k_ref[0:n_past, cols])
                mx = jnp.maximum(mx, jnp.max(s_past, axis=-1, keepdims=True))
            p_diag = jnp.exp2(s_diag - mx)
            den = jnp.sum(p_diag, axis=-1, keepdims=True)
            p_past = None
            if n_past:
                p_past = jnp.exp2(s_past - mx)
                den = den + jnp.sum(p_past, axis=-1, keepdims=True)
            probs.append((p_diag, p_past, 1.0 / den))

        def pv(w_diag, w_past):
            o = jnp.dot(w_diag.astype(BF16), v_ref[rows, :], preferred_element_type=F32)
            if n_past:
                o = o + jnp.dot(w_past.astype(BF16), v_ref[0:n_past, :],
                                preferred_element_type=F32)
            return o

        if n_maps == 2:
            (d1, p1, r1), (d2, p2, r2) = probs
            c2 = lam * r2
            o = pv(d1 * r1 - d2 * c2, (p1 * r1 - p2 * c2) if n_past else None)
            o = _rms(o) * subg_ref[...] * (1.0 - lam_init)
        else:
            (d1, p1, r1), = probs
            o = pv(d1, p1) * r1
        o_ref[rows, :] = o.astype(o_ref.dtype)


def _attn_call(q, k, v, posq, posk, extra, *, batch, seq, heads, n_maps, dk, dv,
               qcol, kcol, vcol, lam_init, name):
    tq = ATTN_TQ
    in_specs = [
        pl.BlockSpec((seq, n_maps * dk), lambda b, h: (b, qcol + h)),
        pl.BlockSpec((seq, n_maps * dk), lambda b, h: (b, kcol + h)),
        pl.BlockSpec((seq, dv), lambda b, h: (b, vcol + h)),
        pl.BlockSpec((seq, 1), lambda b, h: (b, 0)),
        pl.BlockSpec((1, seq), lambda b, h: (0, b)),
    ] + [pl.BlockSpec(e.shape, lambda b, h: (0, 0)) for e in extra]
    return pl.pallas_call(
        functools.partial(_attn_kernel, n_maps=n_maps, dk=dk, tq=tq, nq=seq // tq,
                          lam_init=lam_init),
        grid=(batch, heads),
        in_specs=in_specs,
        out_specs=pl.BlockSpec((seq, dv), lambda b, h: (b, h)),
        out_shape=jax.ShapeDtypeStruct((batch * seq, heads * dv), BF16),
        compiler_params=_cparams(2, VMEM_LIMIT),
        name=name,
    )(q, k, v, posq, posk, *extra)


def _cross_kernel(q_ref, k_ref, v_ref, o_ref):
    for h in range(CROSS_HEADS):
        sl = slice(h * CROSS_HEAD_DIM, (h + 1) * CROSS_HEAD_DIM)
        s = _dot_nt(q_ref[:, sl], k_ref[:, sl])
        p = jnp.exp2(s - jnp.max(s, axis=-1, keepdims=True))
        inv = 1.0 / jnp.sum(p, axis=-1, keepdims=True)
        o = jnp.dot(p.astype(BF16), v_ref[:, sl], preferred_element_type=F32)
        o_ref[:, sl] = (o * inv).astype(o_ref.dtype)


def _cross_call(q, memkv, *, batch, seq, qcol, tq=512):
    nq = seq // tq
    return pl.pallas_call(
        _cross_kernel,
        grid=(batch, nq),
        in_specs=[pl.BlockSpec((tq, CROSS_WIDTH), lambda b, i: (b * nq + i, qcol)),
                  pl.BlockSpec((N_MEM, CROSS_WIDTH), lambda b, i: (b, 0)),
                  pl.BlockSpec((N_MEM, CROSS_WIDTH), lambda b, i: (b, 1))],
        out_specs=pl.BlockSpec((tq, CROSS_WIDTH), lambda b, i: (b * nq + i, 0)),
        out_shape=jax.ShapeDtypeStruct((batch * seq, CROSS_WIDTH), BF16),
        compiler_params=_cparams(2),
        name="cross_attn",
    )(q, memkv, memkv)


def _outproj_kernel(os_ref, oc_ref, w_ref, x_ref, o_ref, wbf_ref):
    @pl.when(pl.program_id(1) == 0)
    def _():
        wbf_ref[...] = w_ref[...].astype(BF16)

    for r in range(x_ref.shape[0] // MM_SUB):
        rows = slice(r * MM_SUB, (r + 1) * MM_SUB)
        acc = jnp.dot(os_ref[rows, :], wbf_ref[:SELF_WIDTH, :], preferred_element_type=F32)
        acc = acc + jnp.dot(oc_ref[rows, :], wbf_ref[SELF_WIDTH:, :], preferred_element_type=F32)
        o_ref[rows, :] = x_ref[rows, :] + acc


def _outproj_call(o_self, o_cross, w_out, x, *, tm=1024, tn=512):
    m = x.shape[0]
    return pl.pallas_call(
        _outproj_kernel,
        grid=(D_MODEL // tn, m // tm),
        in_specs=[pl.BlockSpec((tm, SELF_WIDTH), lambda j, i: (i, 0)),
                  pl.BlockSpec((tm, CROSS_WIDTH), lambda j, i: (i, 0)),
                  pl.BlockSpec((D_MODEL, tn), lambda j, i: (0, j)),
                  pl.BlockSpec((tm, tn), lambda j, i: (i, j))],
        out_specs=pl.BlockSpec((tm, tn), lambda j, i: (i, j)),
        out_shape=jax.ShapeDtypeStruct((m, D_MODEL), F32),
        scratch_shapes=[pltpu.VMEM((D_MODEL, tn), BF16)],
        compiler_params=_cparams(2, VMEM_LIMIT),
        name="out_proj",
    )(o_self, o_cross, w_out, x)


def _mla_in_kernel(h_ref, w_ref, gcq_ref, gckv_ref, gqm_ref, cq_ref, ckv_ref, qm_ref, kpe_ref):
    a = MLA_Q_RANK
    b = a + MLA_KV_RANK
    c = b + CROSS_WIDTH
    sub = h_ref.shape[0] // 2
    for r in range(2):
        rows = slice(r * sub, (r + 1) * sub)
        acc = jnp.dot(h_ref[rows, :], w_ref[...], preferred_element_type=F32)
        cq_ref[rows, :] = (_rms(acc[:, :a]) * gcq_ref[...]).astype(cq_ref.dtype)
        ckv_ref[rows, :] = (_rms(acc[:, a:b]) * gckv_ref[...]).astype(ckv_ref.dtype)
        for j in range(CROSS_HEADS):
            sl = slice(j * CROSS_HEAD_DIM, (j + 1) * CROSS_HEAD_DIM)
            qm_ref[rows, sl] = (_rms(acc[:, b + j * CROSS_HEAD_DIM:b + (j + 1) * CROSS_HEAD_DIM])
                                * gqm_ref[:, sl]).astype(qm_ref.dtype)
        kpe_ref[rows, :] = acc[:, c:]


def _mla_in_call(h, wcat, gcq, gckv, gqm, *, tm=512):
    m = h.shape[0]
    ncat = wcat.shape[1]
    row = lambda w: pl.BlockSpec((tm, w), lambda i: (i, 0))
    full = lambda r, w: pl.BlockSpec((r, w), lambda i: (0, 0))
    w_spec = pl.BlockSpec((D_MODEL, ncat), lambda i: (0, 0), pipeline_mode=pl.Buffered(1))
    return pl.pallas_call(
        _mla_in_kernel,
        grid=(m // tm,),
        in_specs=[row(D_MODEL), w_spec, full(1, MLA_Q_RANK), full(1, MLA_KV_RANK),
                  full(1, CROSS_WIDTH)],
        out_specs=[row(MLA_Q_RANK), row(MLA_KV_RANK), row(CROSS_WIDTH), row(LANES)],
        out_shape=[jax.ShapeDtypeStruct((m, MLA_Q_RANK), BF16),
                   jax.ShapeDtypeStruct((m, MLA_KV_RANK), BF16),
                   jax.ShapeDtypeStruct((m, CROSS_WIDTH), BF16),
                   jax.ShapeDtypeStruct((m, LANES), F32)],
        compiler_params=_cparams(1, VMEM_LIMIT),
        name="mla_in_proj",
    )(h, wcat, gcq, gckv, gqm)


MLA_GROUP = 4


def _rope_pe(x, c, s):
    lane = lax.broadcasted_iota(I32, x.shape, 1)
    partner = jnp.where((lane & 63) < 32, pltpu.roll(x, 96, axis=1), pltpu.roll(x, 32, axis=1))
    return x * c + partner * s


def _mla_q_kernel(cq_ref, wn_ref, wp_ref, wr_ref, gn_ref, gp_ref, gr_ref, cos_ref, sin_ref, o_ref):
    x = cq_ref[...]
    an = jnp.dot(x, wn_ref[...], preferred_element_type=F32)
    ap = jnp.dot(x, wp_ref[...], preferred_element_type=F32)
    ar = jnp.dot(x, wr_ref[...], preferred_element_type=F32)
    gc = cos_ref[...] * gp_ref[...]
    gs = sin_ref[...] * gr_ref[...]
    for j in range(MLA_GROUP):
        sl = slice(j * 128, (j + 1) * 128)
        nj = an[:, sl]
        pj = ap[:, sl]
        ss = jnp.sum(nj * nj + pj * pj, axis=-1, keepdims=True)
        rs = lax.rsqrt(ss * (1.0 / MLA_QK_DIM) + NORM_EPS)
        o_ref[:, j * 256:j * 256 + 128] = (nj * rs * gn_ref[...]).astype(o_ref.dtype)
        o_ref[:, j * 256 + 128:(j + 1) * 256] = ((pj * gc + ar[:, sl] * gs) * rs).astype(o_ref.dtype)


def _mla_q_call(cq, w_nope, w_pe, w_rot, gn, gp, gr, cos, sin, *, tm=1024):
    m = cq.shape[0]
    g = MLA_HEADS // MLA_GROUP
    tm = min(tm, m)
    wspec = pl.BlockSpec((MLA_Q_RANK, MLA_GROUP * 128), lambda j, i: (0, j))
    gspec = pl.BlockSpec((1, LANES), lambda j, i: (0, 0))
    tspec = pl.BlockSpec((tm, LANES), lambda j, i: (i, 0))
    return pl.pallas_call(
        _mla_q_kernel,
        grid=(g, m // tm),
        in_specs=[pl.BlockSpec((tm, MLA_Q_RANK), lambda j, i: (i, 0)),
                  wspec, wspec, wspec, gspec, gspec, gspec, tspec, tspec],
        out_specs=pl.BlockSpec((tm, MLA_GROUP * MLA_PAD_DIM), lambda j, i: (i, j)),
        out_shape=jax.ShapeDtypeStruct((m, MLA_HEADS * MLA_PAD_DIM), BF16),
        compiler_params=_cparams(2),
        name="mla_q_up",
    )(cq, w_nope, w_pe, w_rot, gn, gp, gr, cos, sin)


def _mla_kv_kernel(ckv_ref, w_ref, kpe_ref, gn_ref, gp_ref, cos_ref, sin_ref, k_ref, v_ref):
    acc = jnp.dot(ckv_ref[...], w_ref[...], preferred_element_type=F32)
    kpe = kpe_ref[...]
    ss_pe = jnp.sum(kpe * kpe, axis=-1, keepdims=True)
    pe = _rope_pe(kpe * gp_ref[...], cos_ref[...], sin_ref[...])
    for j in range(MLA_GROUP):
        kn = acc[:, j * 256:j * 256 + 128]
        ss = jnp.sum(kn * kn, axis=-1, keepdims=True) + ss_pe
        rs = lax.rsqrt(ss * (1.0 / MLA_QK_DIM) + NORM_EPS)
        k_ref[:, j * 256:j * 256 + 128] = (kn * rs * gn_ref[...]).astype(k_ref.dtype)
        k_ref[:, j * 256 + 128:(j + 1) * 256] = (pe * rs).astype(k_ref.dtype)
        v_ref[:, j * 128:(j + 1) * 128] = acc[:, j * 256 + 128:(j + 1) * 256].astype(v_ref.dtype)


def _mla_kv_call(ckv, w_ukv, kpe, gn, gp, cos, sin, *, tm=1024):
    m = ckv.shape[0]
    g = MLA_HEADS // MLA_GROUP
    tm = min(tm, m)
    return pl.pallas_call(
        _mla_kv_kernel,
        grid=(m // tm, g),
        in_specs=[pl.BlockSpec((tm, MLA_KV_RANK), lambda i, j: (i, 0)),
                  pl.BlockSpec((MLA_KV_RANK, MLA_GROUP * 256), lambda i, j: (0, j)),
                  pl.BlockSpec((tm, LANES), lambda i, j: (i, 0)),
                  pl.BlockSpec((1, LANES), lambda i, j: (0, 0)),
                  pl.BlockSpec((1, LANES), lambda i, j: (0, 0)),
                  pl.BlockSpec((tm, LANES), lambda i, j: (i, 0)),
                  pl.BlockSpec((tm, LANES), lambda i, j: (i, 0))],
        out_specs=[pl.BlockSpec((tm, MLA_GROUP * MLA_PAD_DIM), lambda i, j: (i, j)),
                   pl.BlockSpec((tm, MLA_GROUP * MLA_V_DIM), lambda i, j: (i, j))],
        out_shape=[jax.ShapeDtypeStruct((m, MLA_HEADS * MLA_PAD_DIM), BF16),
                   jax.ShapeDtypeStruct((m, MLA_HEADS * MLA_V_DIM), BF16)],
        compiler_params=_cparams(2),
        name="mla_kv_up",
    )(ckv, w_ukv, kpe, gn, gp, cos, sin)


def _route_kernel(x_ref, g_ref, w_ref, b_ref, ids_ref, wts_ref):
    xn = _rms(x_ref[...]) * g_ref[...]
    x_hi = xn.astype(BF16)
    x_lo = (xn - x_hi.astype(F32)).astype(BF16)
    w = w_ref[...]
    w_hi = w.astype(BF16)
    w_lo = (w - w_hi.astype(F32)).astype(BF16)
    lg = _dot_nt(w_hi, x_hi) + _dot_nt(w_hi, x_lo) + _dot_nt(w_lo, x_hi) + b_ref[...]
    tm = lg.shape[1]
    iota = lax.broadcasted_iota(I32, (EXPERTS_PER_GROUP, tm), 0)

    def first_argmax(v):
        mx = jnp.max(v, axis=0, keepdims=True)
        idx = jnp.min(jnp.where(v == mx, iota, EXPERTS_PER_GROUP), axis=0, keepdims=True)
        return mx, idx

    gl = lg[0:N_GROUPS, :]
    gmax, gsel = first_argmax(gl)
    g_gate = 1.0 / jnp.sum(jnp.exp(gl - gmax), axis=0, keepdims=True)
    el = jnp.zeros((EXPERTS_PER_GROUP, tm), F32)
    for g in range(N_GROUPS):
        lo = N_GROUPS + g * EXPERTS_PER_GROUP
        el = jnp.where(gsel == g, lg[lo:lo + EXPERTS_PER_GROUP, :], el)
    v1, i1 = first_argmax(el)
    v2, i2 = first_argmax(jnp.where(iota == i1, -jnp.inf, el))
    e = jnp.exp(v2 - v1)
    w1 = g_gate / (1.0 + e)
    w2 = g_gate * e / (1.0 + e)
    ids_ref[...] = jnp.concatenate([gsel * EXPERTS_PER_GROUP + i1,
                                    gsel * EXPERTS_PER_GROUP + i2], axis=0)
    wts_ref[...] = jnp.concatenate([w1, w2], axis=0)


def _route_call(x, g, w_t, b_col, *, tm=256):
    m = x.shape[0]
    return pl.pallas_call(
        _route_kernel,
        grid=(m // tm,),
        in_specs=[pl.BlockSpec((tm, D_MODEL), lambda i: (i, 0)),
                  pl.BlockSpec((1, D_MODEL), lambda i: (0, 0)),
                  pl.BlockSpec((LANES, D_MODEL), lambda i: (0, 0)),
                  pl.BlockSpec((LANES, 1), lambda i: (0, 0))],
        out_specs=[pl.BlockSpec((TOP_K, tm), lambda i: (0, i)),
                   pl.BlockSpec((TOP_K, tm), lambda i: (0, i))],
        out_shape=[jax.ShapeDtypeStruct((TOP_K, m), I32),
                   jax.ShapeDtypeStruct((TOP_K, m), F32)],
        compiler_params=_cparams(1, VMEM_LIMIT),
        name="moe_route",
    )(x, g, w_t, b_col)


def _sort_kernel(ids_ref, pos_ref, te_ref, nu_ref, rs_ref, rc_ref, rn_ref, *, ts):
    n_tok = ids_ref.shape[1]
    nck = n_tok // ts
    iota_e = lax.broadcasted_iota(I32, (N_EXPERTS, ts), 0)

    def chunk(c):
        return pl.ds(pl.multiple_of(c * ts, ts), ts)

    def count_body(c, acc):
        for k in range(TOP_K):
            oh = jnp.where(iota_e == ids_ref[pl.ds(k, 1), chunk(c)], 1.0, 0.0)
            acc = acc + jnp.sum(oh, axis=1, keepdims=True)
        return acc

    counts = lax.fori_loop(0, nck, count_body, jnp.zeros((N_EXPERTS, 1), F32))
    er = lax.broadcasted_iota(I32, (N_EXPERTS, N_EXPERTS), 0)
    ec = lax.broadcasted_iota(I32, (N_EXPERTS, N_EXPERTS), 1)
    strict_lower = jnp.where(ec < er, 1.0, 0.0).astype(BF16)

    def excl_cumsum(v):
        vb = jnp.broadcast_to(v, (N_EXPERTS, LANES)).astype(BF16)
        return jnp.dot(strict_lower, vb, preferred_element_type=F32)

    padded = jnp.floor((counts + (ROW_ALIGN - 1)) * (1.0 / ROW_ALIGN)) * ROW_ALIGN
    hi = jnp.floor(padded * (1.0 / LANES))
    start = excl_cumsum(hi) * LANES + excl_cumsum(padded - hi * LANES)
    chunks = jnp.floor((counts + (MOE_TM - 0.5)) * (1.0 / MOE_TM))
    first = excl_cumsum(chunks)
    ends = first + chunks
    total = jnp.max(ends, axis=0, keepdims=True)
    chunk_i = lax.broadcasted_iota(I32, (N_EXPERTS, LANES), 1).astype(F32)
    chunk_c = jnp.minimum(chunk_i, total - 1.0)
    te = jnp.sum(jnp.where(ends <= chunk_c, 1.0, 0.0), axis=0, keepdims=True)
    mine = lax.broadcasted_iota(I32, (N_EXPERTS, LANES), 0).astype(F32) == te
    rs = jnp.sum(jnp.where(mine, start + (chunk_c - first) * MOE_TM, 0.0), axis=0, keepdims=True)
    te_ref[...] = te.astype(I32)
    nu_ref[...] = total.astype(I32)
    rs_ref[...] = rs.astype(I32)
    left = jnp.sum(jnp.where(mine, padded - (chunk_c - first) * MOE_TM, 0.0), axis=0, keepdims=True)
    rc_ref[...] = jnp.clip(left, 0.0, float(MOE_TM)).astype(I32)
    real = jnp.sum(jnp.where(mine, counts - (chunk_c - first) * MOE_TM, 0.0), axis=0, keepdims=True)
    rn_ref[...] = jnp.clip(real, 0.0, float(MOE_TM)).astype(I32)

    row_base = start[:, 0:1]
    ur = lax.broadcasted_iota(I32, (ts, ts), 0)
    uc = lax.broadcasted_iota(I32, (ts, ts), 1)
    upper = jnp.where(ur <= uc, 1.0, 0.0).astype(BF16)

    def pos_body(k):
        def body(c, carry):
            hit = iota_e == ids_ref[pl.ds(k, 1), chunk(c)]
            incl = jnp.dot(jnp.where(hit, 1.0, 0.0).astype(BF16), upper,
                           preferred_element_type=F32)
            val = row_base + carry + incl - 1.0
            p = jnp.sum(jnp.where(hit, val, 0.0), axis=0, keepdims=True)
            pos_ref[pl.ds(k, 1), chunk(c)] = p.astype(I32)
            return carry + incl[:, ts - 1:ts]
        return body

    carry = jnp.zeros((N_EXPERTS, 1), F32)
    for k in range(TOP_K):
        carry = lax.fori_loop(0, nck, pos_body(k), carry)


def _sort_call(ids, *, ts=512):
    n_tok = ids.shape[1]
    ts = min(ts, n_tok)
    return pl.pallas_call(
        functools.partial(_sort_kernel, ts=ts),
        out_shape=[jax.ShapeDtypeStruct((TOP_K, n_tok), I32),
                   jax.ShapeDtypeStruct((1, LANES), I32),
                   jax.ShapeDtypeStruct((1, LANES), I32),
                   jax.ShapeDtypeStruct((1, LANES), I32),
                   jax.ShapeDtypeStruct((1, LANES), I32),
                   jax.ShapeDtypeStruct((1, LANES), I32)],
        name="moe_sort",
    )(ids)


def _sorted_rows(n_tok):
    return TOP_K * n_tok + N_EXPERTS * ROW_ALIGN


def _moe_chunks(n_tok):
    return -(-(TOP_K * n_tok) // MOE_TM) + N_EXPERTS


INVERT_BATCH = 16


def _invert_kernel(pos_ref, inv_ref):
    def init(i, c):
        for u in range(INVERT_BATCH):
            inv_ref[i * INVERT_BATCH + u] = jnp.int32(0)
        return c
    lax.fori_loop(0, inv_ref.shape[0] // INVERT_BATCH, init, 0)

    def put(i, c):
        base = i * INVERT_BATCH
        dst = [pos_ref[base + u] for u in range(INVERT_BATCH)]
        for u in range(INVERT_BATCH):
            inv_ref[dst[u]] = base + u
        return c
    lax.fori_loop(0, pos_ref.shape[0] // INVERT_BATCH, put, 0)


def _invert_call(pos_flat, n_rows):
    assert n_rows % INVERT_BATCH == 0 and pos_flat.shape[0] % INVERT_BATCH == 0
    return pl.pallas_call(
        _invert_kernel,
        in_specs=[pl.BlockSpec(memory_space=pltpu.SMEM)],
        out_specs=pl.BlockSpec(memory_space=pltpu.SMEM),
        out_shape=jax.ShapeDtypeStruct((n_rows,), I32),
        name="moe_invert",
    )(pos_flat)


def _moe_kernel(te_ref, nu_ref, rs_ref, rc_ref, rn_ref, inv_ref, x_hbm, g_ref, wg_ref, wu_ref,
                wd_ref, y_hbm, xbuf, rscale, acc, gu, wgu, gsem, ssem, *, n_tok):
    i = pl.program_id(0)
    k = pl.program_id(1)
    nk = pl.num_programs(1)
    n_used = nu_ref[0]
    valid = i < n_used
    slot = i % 2
    n_slots = TOP_K * n_tok

    def x_row_copy(tok, r, sl):
        return pltpu.make_async_copy(x_hbm.at[pl.ds(tok, 1)], xbuf.at[sl, pl.ds(r, 1)], gsem.at[sl])

    def x_gather(c, sl):
        base = rs_ref[c]

        def one(r):
            s = inv_ref[base + r]
            x_row_copy(jnp.where(s >= n_tok, s - n_tok, s), r, sl).start()

        def group(g, carry):
            for u in range(DMA_GROUP):
                one(g * DMA_GROUP + u)
            return carry
        n = rn_ref[c]
        n_groups = n >> 3
        lax.fori_loop(0, n_groups, group, 0)

        def rest(r, carry):
            one(r)
            return carry
        lax.fori_loop(n_groups * DMA_GROUP, n, rest, 0)

    def x_wait(c, sl):
        n = rn_ref[c]
        n8 = pl.multiple_of((n >> 3) << 3, ROW_ALIGN)

        @pl.when(n8 > 0)
        def _():
            pltpu.make_async_copy(x_hbm.at[pl.ds(0, n8)], xbuf.at[sl, pl.ds(0, n8)],
                                  gsem.at[sl]).wait()

        def rest(r, carry):
            x_row_copy(0, 0, sl).wait()
            return carry
        lax.fori_loop(0, n - n8, rest, 0)

    def y_copy(c):
        n = pl.multiple_of(rc_ref[c], ROW_ALIGN)
        dst = pl.ds(pl.multiple_of(rs_ref[c], ROW_ALIGN), n)
        return pltpu.make_async_copy(acc.at[pl.ds(0, n)], y_hbm.at[dst], ssem.at[0])

    @pl.when((i == 0) & (k == 0))
    def _():
        xbuf[...] = jnp.zeros(xbuf.shape, xbuf.dtype)
        x_gather(0, 0)
        acc[...] = jnp.zeros(acc.shape, acc.dtype)
        n_rows = y_hbm.shape[0]
        tails = [(off, min(MOE_TM, n_rows - off)) for off in range(n_slots, n_rows, MOE_TM)]
        for off, size in tails:
            pltpu.make_async_copy(acc.at[pl.ds(0, size)], y_hbm.at[pl.ds(off, size)],
                                  ssem.at[0]).start()
        for off, size in tails:
            pltpu.make_async_copy(acc.at[pl.ds(0, size)], y_hbm.at[pl.ds(off, size)],
                                  ssem.at[0]).wait()

    @pl.when(valid & (k == 0))
    def _():
        x_wait(i, slot)

        @pl.when(i + 1 < n_used)
        def _():
            x_gather(i + 1, 1 - slot)

        x = xbuf[slot]
        rscale[...] = lax.rsqrt(jnp.mean(x * x, axis=-1, keepdims=True) + NORM_EPS)

    @pl.when(valid)
    def _():
        wgu[:, :D_EXPERT] = wg_ref[0].astype(BF16)
        wgu[:, D_EXPERT:] = wu_ref[0].astype(BF16)
        cols = pl.ds(pl.multiple_of(k * MOE_TK, MOE_TK), MOE_TK)
        xs = (xbuf[slot, :, cols] * rscale[...] * g_ref[:, cols]).astype(BF16)
        part = jnp.dot(xs, wgu[...], preferred_element_type=F32)

        @pl.when(k == 0)
        def _():
            gu[...] = part

        @pl.when(k > 0)
        def _():
            gu[...] += part

        @pl.when(k == nk - 1)
        def _():
            a = gu[:, :D_EXPERT]
            hid = (a * jax.nn.sigmoid(a)) * gu[:, D_EXPERT:]
            out = jnp.dot(hid.astype(BF16), wd_ref[0].astype(BF16), preferred_element_type=F32)

            @pl.when(i > 0)
            def _():
                y_copy(i - 1).wait()
            acc[...] = out
            y_copy(i).start()

            @pl.when(i == n_used - 1)
            def _():
                y_copy(i).wait()


def _moe_call(te, nu, rs, rc, rn, inv, x, g, w_gate, w_up, w_down):
    n_tok = x.shape[0]
    nk = D_MODEL // MOE_TK
    last_k = nk - 1

    def w_in_map(i, k, te_r, nu_r, *_):
        return te_r[i], jnp.where(i < nu_r[0], k, last_k), 0

    def w_out_map(i, k, te_r, *_):
        return te_r[i], 0, 0

    grid_spec = pltpu.PrefetchScalarGridSpec(
        num_scalar_prefetch=6,
        grid=(_moe_chunks(n_tok), nk),
        in_specs=[
            pl.BlockSpec(memory_space=pl.ANY),
            pl.BlockSpec((1, D_MODEL), lambda i, k, *_: (0, 0)),
            pl.BlockSpec((1, MOE_TK, D_EXPERT), w_in_map),
            pl.BlockSpec((1, MOE_TK, D_EXPERT), w_in_map),
            pl.BlockSpec((1, D_EXPERT, D_MODEL), w_out_map),
        ],
        out_specs=pl.BlockSpec(memory_space=pl.ANY),
        scratch_shapes=[pltpu.VMEM((2, MOE_TM, D_MODEL), F32),
                        pltpu.VMEM((MOE_TM, 1), F32),
                        pltpu.VMEM((MOE_TM, D_MODEL), F32),
                        pltpu.VMEM((MOE_TM, 2 * D_EXPERT), F32),
                        pltpu.VMEM((MOE_TK, 2 * D_EXPERT), BF16),
                        pltpu.SemaphoreType.DMA((2,)),
                        pltpu.SemaphoreType.DMA((1,))],
    )
    return pl.pallas_call(
        functools.partial(_moe_kernel, n_tok=n_tok),
        grid_spec=grid_spec,
        out_shape=jax.ShapeDtypeStruct((_sorted_rows(n_tok), D_MODEL), F32),
        compiler_params=_cparams(2, VMEM_LIMIT),
        name="moe_experts",
    )(te, nu, rs, rc, rn, inv, x, g, w_gate, w_up, w_down)


def _combine_kernel(pos_ref, x_ref, w_ref, g_ref, y_hbm, xo_ref, *rest, n_tok):
    *h_ref, ybuf, sem = rest
    i = pl.program_id(0)
    nb = pl.num_programs(0)
    tm = x_ref.shape[0]
    slot = i % 2

    def fetch(tile, sl):
        def group(g, c):
            for u in range(DMA_GROUP):
                r = g * DMA_GROUP + u
                for k in range(TOP_K):
                    src = pos_ref[k * n_tok + tile * tm + r]
                    pltpu.make_async_copy(y_hbm.at[pl.ds(src, 1)], ybuf.at[sl, k, pl.ds(r, 1)],
                                          sem.at[sl]).start()
            return c
        lax.fori_loop(0, tm // DMA_GROUP, group, 0)

    @pl.when(i == 0)
    def _():
        fetch(0, 0)

    @pl.when(i + 1 < nb)
    def _():
        fetch(i + 1, 1 - slot)

    for k in range(TOP_K):
        pltpu.make_async_copy(y_hbm.at[pl.ds(0, tm)], ybuf.at[slot, k], sem.at[slot]).wait()
    w = w_ref[...]
    x = x_ref[...] + w[:, 0:1] * ybuf[slot, 0] + w[:, 1:2] * ybuf[slot, 1]
    xo_ref[...] = x
    if h_ref:
        h_ref[0][...] = (_rms(x) * g_ref[...]).astype(BF16)


def _combine_call(pos_flat, x, y, wts_t, g_next, *, tm=256):
    m, d = x.shape
    with_norm = g_next is not None
    g = g_next.reshape(1, d) if with_norm else jnp.ones((1, d), F32)
    row = pl.BlockSpec((tm, d), lambda i, pos_r: (i, 0))
    out_specs = [row, row] if with_norm else [row]
    out_shape = [jax.ShapeDtypeStruct((m, d), F32)]
    if with_norm:
        out_shape.append(jax.ShapeDtypeStruct((m, d), BF16))
    grid_spec = pltpu.PrefetchScalarGridSpec(
        num_scalar_prefetch=1,
        grid=(m // tm,),
        in_specs=[row,
                  pl.BlockSpec((tm, TOP_K), lambda i, pos_r: (i, 0)),
                  pl.BlockSpec((1, d), lambda i, pos_r: (0, 0)),
                  pl.BlockSpec(memory_space=pl.ANY)],
        out_specs=out_specs,
        scratch_shapes=[pltpu.VMEM((2, TOP_K, tm, d), F32),
                        pltpu.SemaphoreType.DMA((2,))],
    )
    return pl.pallas_call(
        functools.partial(_combine_kernel, n_tok=m),
        grid_spec=grid_spec,
        out_shape=out_shape,
        compiler_params=_cparams(1, VMEM_LIMIT),
        name="moe_combine",
    )(pos_flat, x, wts_t, g, y)


def _moe_layer(x, norm_g, w_group, b_group, w_router, b_router, w_gate, w_up, w_down, g_next):
    n_tok = x.shape[0]
    pad_rows = LANES - N_GROUPS - N_EXPERTS
    w_t = jnp.concatenate([w_group.T, w_router.T, jnp.zeros((pad_rows, D_MODEL), F32)], axis=0)
    b_col = jnp.concatenate([b_group, b_router, jnp.zeros((pad_rows,), F32)]).reshape(LANES, 1)
    g_row = norm_g.reshape(1, D_MODEL)
    ids, wts = _route_call(x, g_row, w_t, b_col)
    pos, te, nu, rs, rc, rn = _sort_call(ids)
    pos_flat = pos.reshape(-1)
    inv = _invert_call(pos_flat, _sorted_rows(n_tok))
    y = _moe_call(te.reshape(-1), nu.reshape(-1)[:1], rs.reshape(-1), rc.reshape(-1),
                  rn.reshape(-1), inv, x, g_row,
                  w_gate.reshape(N_EXPERTS, D_MODEL, D_EXPERT),
                  w_up.reshape(N_EXPERTS, D_MODEL, D_EXPERT),
                  w_down.reshape(N_EXPERTS, D_EXPERT, D_MODEL))
    return _combine_call(pos_flat, x, y, wts.T, g_next)


def _rope_tables(pos_flat, dim):
    inv_freq = ROPE_THETA ** (-jnp.arange(0, dim, 2, dtype=F32) / dim)
    ang = pos_flat.astype(F32)[:, None] * inv_freq
    return jnp.cos(ang), jnp.sin(ang)


def kernel(x, mem, positions, mem_norm_g, w_mem_kv, mem_k_norm_g, l0_attn_norm_g, l0_w_in, l0_q_norm_g, l0_k_norm_g, l0_lambda_q1, l0_lambda_k1, l0_lambda_q2, l0_lambda_k2, l0_subln_g, l0_cross_q_norm_g, l0_w_out, l0_ffn_norm_g, l0_w_group, l0_b_group, l0_w_router, l0_b_router, l0_w_gate, l0_w_up, l0_w_down, l1_attn_norm_g, l1_w_in, l1_cq_norm_g, l1_ckv_norm_g, l1_w_uq, l1_w_ukv, l1_q_norm_g, l1_k_norm_g, l1_cross_q_norm_g, l1_w_out, l1_ffn_norm_g, l1_w_group, l1_b_group, l1_w_router, l1_b_router, l1_w_gate, l1_w_up, l1_w_down):
    batch, seq, d = x.shape
    n_tok = batch * seq
    xf = x.reshape(n_tok, d)
    pos_flat = positions.reshape(n_tok)
    posq = pos_flat.reshape(n_tok, 1)
    posk = pos_flat.reshape(1, n_tok)
    ones128 = jnp.ones((n_tok, LANES), F32)

    c64, s64 = _rope_tables(pos_flat, DIFF_HEAD_DIM)
    cos_full = jnp.concatenate([c64, c64], axis=1)
    sin_full = jnp.concatenate([-s64, s64], axis=1)
    c32, s32 = _rope_tables(pos_flat, MLA_ROPE_DIM)
    z32 = jnp.zeros_like(c32)
    cos_k = jnp.concatenate([c32, c32, z32, z32], axis=1)
    sin_k = jnp.concatenate([-s32, s32, z32, z32], axis=1)

    cross_scale = CROSS_HEAD_DIM ** -0.5 * LOG2E

    memn = _norm_call(mem.reshape(batch * N_MEM, d), mem_norm_g)
    mem_gain = jnp.concatenate([jnp.tile(mem_k_norm_g, CROSS_HEADS), jnp.ones((CROSS_WIDTH,), F32)])
    k_tiles = CROSS_WIDTH // 512
    memkv = _mm_call(memn, w_mem_kv, mem_gain.reshape(1, -1), ones128[:batch * N_MEM],
                     ones128[:batch * N_MEM], ((0, k_tiles, "head256"), (k_tiles, 2 * k_tiles, "plain")),
                     tm=batch * N_MEM, tn=512, name="mem_kv")

    h0 = _norm_call(xf, l0_attn_norm_g)
    qk_tiles = SELF_WIDTH // 512
    gain0 = jnp.concatenate([
        jnp.tile(l0_q_norm_g, 2 * DIFF_HEADS) * (DIFF_HEAD_DIM ** -0.5 * LOG2E),
        jnp.tile(l0_k_norm_g, 2 * DIFF_HEADS),
        jnp.ones((SELF_WIDTH,), F32),
        jnp.tile(l0_cross_q_norm_g, CROSS_HEADS) * cross_scale]).reshape(1, -1)
    proj0 = _mm_call(h0, l0_w_in, gain0, cos_full, sin_full,
                     ((0, 2 * qk_tiles, "head128_rope"), (2 * qk_tiles, 3 * qk_tiles, "plain"),
                      (3 * qk_tiles, 3 * qk_tiles + CROSS_WIDTH // 512, "head256")),
                     tm=1024, tn=512, name="l0_in_proj")
    lam_init = 0.8 - 0.6 * math.exp(-0.3 * 0)
    row = lambda v: v.reshape(1, -1)
    o_self = _attn_call(proj0, proj0, proj0, posq, posk,
                        (row(l0_lambda_q1), row(l0_lambda_k1), row(l0_lambda_q2),
                         row(l0_lambda_k2), row(l0_subln_g)),
                        batch=batch, seq=seq, heads=DIFF_HEADS, n_maps=2, dk=DIFF_HEAD_DIM,
                        dv=DIFF_V_DIM, qcol=0, kcol=DIFF_HEADS, vcol=2 * DIFF_HEADS,
                        lam_init=lam_init, name="diff_attn")
    o_cross = _cross_call(proj0, memkv, batch=batch, seq=seq, qcol=3 * SELF_WIDTH // CROSS_WIDTH)
    x1 = _outproj_call(o_self, o_cross, l0_w_out, xf)
    x2, h1 = _moe_layer(x1, l0_ffn_norm_g, l0_w_group, l0_b_group, l0_w_router, l0_b_router,
                        l0_w_gate, l0_w_up, l0_w_down, l1_attn_norm_g)

    a = MLA_Q_RANK
    b = a + MLA_KV_RANK
    c = b + MLA_ROPE_DIM
    w_in_b = l1_w_in.astype(BF16)
    wcat = jnp.concatenate([w_in_b[:, :b], w_in_b[:, c:], w_in_b[:, b:c],
                            jnp.zeros((d, LANES - MLA_ROPE_DIM), BF16)], axis=1)
    cq, ckv, qm, kpe = _mla_in_call(
        h1, wcat, row(l1_cq_norm_g), row(l1_ckv_norm_g),
        row(jnp.tile(l1_cross_q_norm_g, CROSS_HEADS) * cross_scale))
    w_uq3 = l1_w_uq.astype(BF16).reshape(MLA_Q_RANK, MLA_HEADS, MLA_QK_DIM)
    w_q_nope = w_uq3[:, :, :MLA_NOPE_DIM].reshape(MLA_Q_RANK, -1)
    partner = (jnp.arange(MLA_ROPE_DIM) + MLA_ROPE_DIM // 2) % MLA_ROPE_DIM
    w_pe3 = w_uq3[:, :, MLA_NOPE_DIM:]
    pad3 = jnp.zeros_like(w_pe3)
    w_q_pe = jnp.concatenate([w_pe3, pad3], axis=2).reshape(MLA_Q_RANK, -1)
    half = MLA_ROPE_DIM // 2
    w_q_rot = jnp.concatenate([w_pe3[:, :, half:], w_pe3[:, :, :half], pad3],
                              axis=2).reshape(MLA_Q_RANK, -1)
    q_scale = MLA_QK_DIM ** -0.5 * LOG2E
    zeros64 = jnp.zeros((MLA_ROPE_DIM,), F32)
    g_q_pe = l1_q_norm_g[MLA_NOPE_DIM:] * q_scale
    q_pad = _mla_q_call(cq, w_q_nope, w_q_pe, w_q_rot,
                        row(l1_q_norm_g[:MLA_NOPE_DIM] * q_scale),
                        row(jnp.concatenate([g_q_pe, zeros64])),
                        row(jnp.concatenate([g_q_pe[partner], zeros64])), cos_k, sin_k)
    k_pad, v1 = _mla_kv_call(ckv, l1_w_ukv.astype(BF16), kpe,
                             row(l1_k_norm_g[:MLA_NOPE_DIM]),
                             row(jnp.concatenate([l1_k_norm_g[MLA_NOPE_DIM:], zeros64])),
                             cos_k, sin_k)
    o_self1 = _attn_call(q_pad, k_pad, v1, posq, posk, (),
                         batch=batch, seq=seq, heads=MLA_HEADS, n_maps=1, dk=MLA_PAD_DIM,
                         dv=MLA_V_DIM, qcol=0, kcol=0, vcol=0, lam_init=0.0, name="mla_attn")
    o_cross1 = _cross_call(qm, memkv, batch=batch, seq=seq, qcol=0)
    x3 = _outproj_call(o_self1, o_cross1, l1_w_out, x2)
    (x4,) = _moe_layer(x3, l1_ffn_norm_g, l1_w_group, l1_b_group, l1_w_router, l1_b_router,
                       l1_w_gate, l1_w_up, l1_w_down, None)
    return x4.reshape(batch, seq, d)
```

```python
import functools
import math

import jax
import jax.numpy as jnp
from jax import lax
from jax.experimental import pallas as pl
from jax.experimental.pallas import tpu as pltpu

F32 = jnp.float32
BF16 = jnp.bfloat16
I32 = jnp.int32

D_MODEL = 4096
N_MEM = 256
ROPE_THETA = 10000.0
NORM_EPS = 1e-6
LOG2E = math.log2(math.e)

SELF_WIDTH = 3 * D_MODEL // 4
CROSS_HEADS = 4
CROSS_HEAD_DIM = (D_MODEL // 4) // CROSS_HEADS
CROSS_WIDTH = CROSS_HEADS * CROSS_HEAD_DIM

DIFF_HEAD_DIM = 128
DIFF_V_DIM = 2 * DIFF_HEAD_DIM
DIFF_HEADS = SELF_WIDTH // DIFF_V_DIM

MLA_NOPE_DIM = 128
MLA_ROPE_DIM = 64
MLA_V_DIM = 128
MLA_HEADS = SELF_WIDTH // MLA_V_DIM
MLA_QK_DIM = MLA_NOPE_DIM + MLA_ROPE_DIM
MLA_Q_RANK = 3 * D_MODEL // 16
MLA_KV_RANK = D_MODEL // 8
MLA_PAD_DIM = 256

N_GROUPS = 8
EXPERTS_PER_GROUP = 8
N_EXPERTS = N_GROUPS * EXPERTS_PER_GROUP
D_EXPERT = 3 * D_MODEL // 32
TOP_K = 2

LANES = 128
MOE_TM = 320
MOE_TK = 2048
DMA_GROUP = 8
ROW_ALIGN = 8
ATTN_TQ = 256
VMEM_LIMIT = 56 * 1024 * 1024


def _cparams(n_axes, vmem=None):
    return pltpu.CompilerParams(dimension_semantics=("arbitrary",) * n_axes,
                                vmem_limit_bytes=vmem)


def _rms(x):
    return x * lax.rsqrt(jnp.mean(x * x, axis=-1, keepdims=True) + NORM_EPS)


def _dot_nt(a, b):
    return lax.dot_general(a, b, (((1,), (1,)), ((), ())), preferred_element_type=F32)


def _norm_kernel(x_ref, g_ref, o_ref):
    o_ref[...] = (_rms(x_ref[...]) * g_ref[...]).astype(o_ref.dtype)


def _norm_call(x, g, tm=256):
    m, d = x.shape
    return pl.pallas_call(
        _norm_kernel,
        grid=(m // tm,),
        in_specs=[pl.BlockSpec((tm, d), lambda i: (i, 0)),
                  pl.BlockSpec((1, d), lambda i: (0, 0))],
        out_specs=pl.BlockSpec((tm, d), lambda i: (i, 0)),
        out_shape=jax.ShapeDtypeStruct((m, d), BF16),
        compiler_params=_cparams(1),
        name="rmsnorm",
    )(x, g.reshape(1, d))


def _epilogue(kind, acc, rows, gain_ref, cos_ref, sin_ref, o_ref):
    tn = acc.shape[1]
    if kind == "plain":
        o_ref[rows, :] = acc.astype(o_ref.dtype)
    elif kind == "head128_rope":
        c = cos_ref[rows, :]
        s = sin_ref[rows, :]
        for j in range(tn // 128):
            sl = slice(j * 128, (j + 1) * 128)
            y = _rms(acc[:, sl]) * gain_ref[:, sl]
            y = y * c + pltpu.roll(y, 64, axis=1) * s
            o_ref[rows, sl] = y.astype(o_ref.dtype)
    elif kind == "head256":
        for j in range(tn // 256):
            sl = slice(j * 256, (j + 1) * 256)
            o_ref[rows, sl] = (_rms(acc[:, sl]) * gain_ref[:, sl]).astype(o_ref.dtype)
    else:
        raise ValueError(kind)


MM_SUB = 256


def _mm_kernel(x_ref, w_ref, gain_ref, cos_ref, sin_ref, o_ref, wbf_ref, *, kinds):
    n = pl.program_id(0)

    @pl.when(pl.program_id(1) == 0)
    def _():
        wbf_ref[...] = w_ref[...].astype(BF16)

    def run(kind):
        for r in range(x_ref.shape[0] // MM_SUB):
            rows = slice(r * MM_SUB, (r + 1) * MM_SUB)
            acc = jnp.dot(x_ref[rows, :], wbf_ref[...], preferred_element_type=F32)
            _epilogue(kind, acc, rows, gain_ref, cos_ref, sin_ref, o_ref)

    if len(kinds) == 1:
        run(kinds[0][2])
    else:
        for lo, hi, kind in kinds:
            pl.when((n >= lo) & (n < hi))(functools.partial(run, kind))


def _mm_call(x, w, gain, cos, sin, kinds, *, tm, tn, name):
    m, k = x.shape
    n = w.shape[1]
    return pl.pallas_call(
        functools.partial(_mm_kernel, kinds=kinds),
        grid=(n // tn, m // tm),
        in_specs=[pl.BlockSpec((tm, k), lambda j, i: (i, 0)),
                  pl.BlockSpec((k, tn), lambda j, i: (0, j)),
                  pl.BlockSpec((1, tn), lambda j, i: (0, j)),
                  pl.BlockSpec((tm, LANES), lambda j, i: (i, 0)),
                  pl.BlockSpec((tm, LANES), lambda j, i: (i, 0))],
        out_specs=pl.BlockSpec((tm, tn), lambda j, i: (i, j)),
        out_shape=jax.ShapeDtypeStruct((m, n), BF16),
        scratch_shapes=[pltpu.VMEM((k, tn), BF16)],
        compiler_params=_cparams(2, VMEM_LIMIT),
        name=name,
    )(x, w, gain, cos, sin)


def _attn_kernel(*refs, n_maps, dk, tq, nq, lam_init):
    if n_maps == 2:
        (q_ref, k_ref, v_ref, posq_ref, posk_ref, lq1_ref, lk1_ref, lq2_ref, lk2_ref, subg_ref,
         o_ref) = refs
        lam = (jnp.exp(jnp.sum(lq1_ref[...] * lk1_ref[...], axis=-1, keepdims=True))
               - jnp.exp(jnp.sum(lq2_ref[...] * lk2_ref[...], axis=-1, keepdims=True))
               + lam_init)
    else:
        q_ref, k_ref, v_ref, posq_ref, posk_ref, o_ref = refs

    for i in range(nq):
        rows = slice(i * tq, (i + 1) * tq)
        n_past = i * tq
        mask = posk_ref[:, rows] <= posq_ref[rows, :]
        probs = []
        for mi in range(n_maps):
            cols = slice(mi * dk, (mi + 1) * dk)
            qm = q_ref[rows, cols]
            s_diag = jnp.where(mask, _dot_nt(qm, k_ref[rows, cols]), -jnp.inf)
            mx = jnp.max(s_diag, axis=-1, keepdims=True)
            if n_past:
                s_past = _dot_nt(qm, k_ref[0:n_past, cols])
                mx = jnp.maximum(mx, jnp.max(s_past, axis=-1, keepdims=True))
            p_diag = jnp.exp2(s_diag - mx)
            p_past = jnp.exp2(s_past - mx) if n_past else None
            probs.append((p_diag, p_past))

        def pv(w_diag, w_past):
            o = jnp.dot(w_diag.astype(BF16), v_ref[rows, :], preferred_element_type=F32)
            if n_past:
                o = o + jnp.dot(w_past.astype(BF16), v_ref[0:n_past, :],
                                preferred_element_type=F32)
            return o

        def row_sum(w_diag, w_past):
            den = jnp.sum(w_diag, axis=-1, keepdims=True)
            if n_past:
                den = den + jnp.sum(w_past, axis=-1, keepdims=True)
            return den

        if n_maps == 2:
            (d1, p1), (d2, p2) = probs
            r1 = 1.0 / row_sum(d1, p1)
            c2 = lam * (1.0 / row_sum(d2, p2))
            o = pv(d1, p1) * r1 - pv(d2, p2) * c2
            o = _rms(o) * subg_ref[...] * (1.0 - lam_init)
        else:
            (d1, p1), = probs
            o = pv(d1, p1) * (1.0 / row_sum(d1, p1))
        o_ref[rows, :] = o.astype(o_ref.dtype)


def _attn_call(q, k, v, posq, posk, extra, *, batch, seq, heads, n_maps, dk, dv,
               qcol, kcol, vcol, lam_init, name):
    tq = ATTN_TQ
    in_specs = [
        pl.BlockSpec((seq, n_maps * dk), lambda b, h: (b, qcol + h)),
        pl.BlockSpec((seq, n_maps * dk), lambda b, h: (b, kcol + h)),
        pl.BlockSpec((seq, dv), lambda b, h: (b, vcol + h)),
        pl.BlockSpec((seq, 1), lambda b, h: (b, 0)),
        pl.BlockSpec((1, seq), lambda b, h: (0, b)),
    ] + [pl.BlockSpec(e.shape, lambda b, h: (0, 0)) for e in extra]
    return pl.pallas_call(
        functools.partial(_attn_kernel, n_maps=n_maps, dk=dk, tq=tq, nq=seq // tq,
                          lam_init=lam_init),
        grid=(batch, heads),
        in_specs=in_specs,
        out_specs=pl.BlockSpec((seq, dv), lambda b, h: (b, h)),
        out_shape=jax.ShapeDtypeStruct((batch * seq, heads * dv), BF16),
        compiler_params=_cparams(2, VMEM_LIMIT),
        name=name,
    )(q, k, v, posq, posk, *extra)


def _cross_kernel(q_ref, k_ref, v_ref, o_ref):
    for h in range(CROSS_HEADS):
        sl = slice(h * CROSS_HEAD_DIM, (h + 1) * CROSS_HEAD_DIM)
        s = _dot_nt(q_ref[:, sl], k_ref[:, sl])
        p = jnp.exp2(s - jnp.max(s, axis=-1, keepdims=True))
        inv = 1.0 / jnp.sum(p, axis=-1, keepdims=True)
        o = jnp.dot(p.astype(BF16), v_ref[:, sl], preferred_element_type=F32)
        o_ref[:, sl] = (o * inv).astype(o_ref.dtype)


def _cross_call(q, memkv, *, batch, seq, qcol, tq=512):
    nq = seq // tq
    return pl.pallas_call(
        _cross_kernel,
        grid=(batch, nq),
        in_specs=[pl.BlockSpec((tq, CROSS_WIDTH), lambda b, i: (b * nq + i, qcol)),
                  pl.BlockSpec((N_MEM, CROSS_WIDTH), lambda b, i: (b, 0)),
                  pl.BlockSpec((N_MEM, CROSS_WIDTH), lambda b, i: (b, 1))],
        out_specs=pl.BlockSpec((tq, CROSS_WIDTH), lambda b, i: (b * nq + i, 0)),
        out_shape=jax.ShapeDtypeStruct((batch * seq, CROSS_WIDTH), BF16),
        compiler_params=_cparams(2),
        name="cross_attn",
    )(q, memkv, memkv)


def _outproj_kernel(os_ref, oc_ref, w_ref, x_ref, o_ref, wbf_ref):
    @pl.when(pl.program_id(1) == 0)
    def _():
        wbf_ref[...] = w_ref[...].astype(BF16)

    for r in range(x_ref.shape[0] // MM_SUB):
        rows = slice(r * MM_SUB, (r + 1) * MM_SUB)
        acc = jnp.dot(os_ref[rows, :], wbf_ref[:SELF_WIDTH, :], preferred_element_type=F32)
        acc = acc + jnp.dot(oc_ref[rows, :], wbf_ref[SELF_WIDTH:, :], preferred_element_type=F32)
        o_ref[rows, :] = x_ref[rows, :] + acc


def _outproj_call(o_self, o_cross, w_out, x, *, tm=1024, tn=512):
    m = x.shape[0]
    return pl.pallas_call(
        _outproj_kernel,
        grid=(D_MODEL // tn, m // tm),
        in_specs=[pl.BlockSpec((tm, SELF_WIDTH), lambda j, i: (i, 0)),
                  pl.BlockSpec((tm, CROSS_WIDTH), lambda j, i: (i, 0)),
                  pl.BlockSpec((D_MODEL, tn), lambda j, i: (0, j)),
                  pl.BlockSpec((tm, tn), lambda j, i: (i, j))],
        out_specs=pl.BlockSpec((tm, tn), lambda j, i: (i, j)),
        out_shape=jax.ShapeDtypeStruct((m, D_MODEL), F32),
        scratch_shapes=[pltpu.VMEM((D_MODEL, tn), BF16)],
        compiler_params=_cparams(2, VMEM_LIMIT),
        name="out_proj",
    )(o_self, o_cross, w_out, x)


def _mla_in_kernel(h_ref, w_ref, gcq_ref, gckv_ref, gqm_ref, cq_ref, ckv_ref, qm_ref, kpe_ref):
    a = MLA_Q_RANK
    b = a + MLA_KV_RANK
    c = b + CROSS_WIDTH
    sub = h_ref.shape[0] // 2
    for r in range(2):
        rows = slice(r * sub, (r + 1) * sub)
        acc = jnp.dot(h_ref[rows, :], w_ref[...], preferred_element_type=F32)
        cq_ref[rows, :] = (_rms(acc[:, :a]) * gcq_ref[...]).astype(cq_ref.dtype)
        ckv_ref[rows, :] = (_rms(acc[:, a:b]) * gckv_ref[...]).astype(ckv_ref.dtype)
        for j in range(CROSS_HEADS):
            sl = slice(j * CROSS_HEAD_DIM, (j + 1) * CROSS_HEAD_DIM)
            qm_ref[rows, sl] = (_rms(acc[:, b + j * CROSS_HEAD_DIM:b + (j + 1) * CROSS_HEAD_DIM])
                                * gqm_ref[:, sl]).astype(qm_ref.dtype)
        kpe_ref[rows, :] = acc[:, c:]


def _mla_in_call(h, wcat, gcq, gckv, gqm, *, tm=512):
    m = h.shape[0]
    ncat = wcat.shape[1]
    row = lambda w: pl.BlockSpec((tm, w), lambda i: (i, 0))
    full = lambda r, w: pl.BlockSpec((r, w), lambda i: (0, 0))
    w_spec = pl.BlockSpec((D_MODEL, ncat), lambda i: (0, 0), pipeline_mode=pl.Buffered(1))
    return pl.pallas_call(
        _mla_in_kernel,
        grid=(m // tm,),
        in_specs=[row(D_MODEL), w_spec, full(1, MLA_Q_RANK), full(1, MLA_KV_RANK),
                  full(1, CROSS_WIDTH)],
        out_specs=[row(MLA_Q_RANK), row(MLA_KV_RANK), row(CROSS_WIDTH), row(LANES)],
        out_shape=[jax.ShapeDtypeStruct((m, MLA_Q_RANK), BF16),
                   jax.ShapeDtypeStruct((m, MLA_KV_RANK), BF16),
                   jax.ShapeDtypeStruct((m, CROSS_WIDTH), BF16),
                   jax.ShapeDtypeStruct((m, LANES), F32)],
        compiler_params=_cparams(1, VMEM_LIMIT),
        name="mla_in_proj",
    )(h, wcat, gcq, gckv, gqm)


MLA_GROUP = 4


def _rope_pe(x, c, s):
    lane = lax.broadcasted_iota(I32, x.shape, 1)
    partner = jnp.where((lane & 63) < 32, pltpu.roll(x, 96, axis=1), pltpu.roll(x, 32, axis=1))
    return x * c + partner * s


def _mla_q_kernel(cq_ref, wn_ref, wp_ref, wr_ref, gn_ref, gp_ref, gr_ref, cos_ref, sin_ref, o_ref):
    x = cq_ref[...]
    an = jnp.dot(x, wn_ref[...], preferred_element_type=F32)
    ap = jnp.dot(x, wp_ref[...], preferred_element_type=F32)
    ar = jnp.dot(x, wr_ref[...], preferred_element_type=F32)
    gc = cos_ref[...] * gp_ref[...]
    gs = sin_ref[...] * gr_ref[...]
    for j in range(MLA_GROUP):
        sl = slice(j * 128, (j + 1) * 128)
        nj = an[:, sl]
        pj = ap[:, sl]
        ss = jnp.sum(nj * nj + pj * pj, axis=-1, keepdims=True)
        rs = lax.rsqrt(ss * (1.0 / MLA_QK_DIM) + NORM_EPS)
        o_ref[:, j * 256:j * 256 + 128] = (nj * rs * gn_ref[...]).astype(o_ref.dtype)
        o_ref[:, j * 256 + 128:(j + 1) * 256] = ((pj * gc + ar[:, sl] * gs) * rs).astype(o_ref.dtype)


def _mla_q_call(cq, w_nope, w_pe, w_rot, gn, gp, gr, cos, sin, *, tm=1024):
    m = cq.shape[0]
    g = MLA_HEADS // MLA_GROUP
    tm = min(tm, m)
    wspec = pl.BlockSpec((MLA_Q_RANK, MLA_GROUP * 128), lambda j, i: (0, j))
    gspec = pl.BlockSpec((1, LANES), lambda j, i: (0, 0))
    tspec = pl.BlockSpec((tm, LANES), lambda j, i: (i, 0))
    return pl.pallas_call(
        _mla_q_kernel,
        grid=(g, m // tm),
        in_specs=[pl.BlockSpec((tm, MLA_Q_RANK), lambda j, i: (i, 0)),
                  wspec, wspec, wspec, gspec, gspec, gspec, tspec, tspec],
        out_specs=pl.BlockSpec((tm, MLA_GROUP * MLA_PAD_DIM), lambda j, i: (i, j)),
        out_shape=jax.ShapeDtypeStruct((m, MLA_HEADS * MLA_PAD_DIM), BF16),
        compiler_params=_cparams(2),
        name="mla_q_up",
    )(cq, w_nope, w_pe, w_rot, gn, gp, gr, cos, sin)


def _mla_kv_kernel(ckv_ref, w_ref, kpe_ref, gn_ref, gp_ref, cos_ref, sin_ref, k_ref, v_ref):
    acc = jnp.dot(ckv_ref[...], w_ref[...], preferred_element_type=F32)
    kpe = kpe_ref[...]
    ss_pe = jnp.sum(kpe * kpe, axis=-1, keepdims=True)
    pe = _rope_pe(kpe * gp_ref[...], cos_ref[...], sin_ref[...])
    for j in range(MLA_GROUP):
        kn = acc[:, j * 256:j * 256 + 128]
        ss = jnp.sum(kn * kn, axis=-1, keepdims=True) + ss_pe
        rs = lax.rsqrt(ss * (1.0 / MLA_QK_DIM) + NORM_EPS)
        k_ref[:, j * 256:j * 256 + 128] = (kn * rs * gn_ref[...]).astype(k_ref.dtype)
        k_ref[:, j * 256 + 128:(j + 1) * 256] = (pe * rs).astype(k_ref.dtype)
        v_ref[:, j * 128:(j + 1) * 128] = acc[:, j * 256 + 128:(j + 1) * 256].astype(v_ref.dtype)


def _mla_kv_call(ckv, w_ukv, kpe, gn, gp, cos, sin, *, tm=1024):
    m = ckv.shape[0]
    g = MLA_HEADS // MLA_GROUP
    tm = min(tm, m)
    return pl.pallas_call(
        _mla_kv_kernel,
        grid=(m // tm, g),
        in_specs=[pl.BlockSpec((tm, MLA_KV_RANK), lambda i, j: (i, 0)),
                  pl.BlockSpec((MLA_KV_RANK, MLA_GROUP * 256), lambda i, j: (0, j)),
                  pl.BlockSpec((tm, LANES), lambda i, j: (i, 0)),
                  pl.BlockSpec((1, LANES), lambda i, j: (0, 0)),
                  pl.BlockSpec((1, LANES), lambda i, j: (0, 0)),
                  pl.BlockSpec((tm, LANES), lambda i, j: (i, 0)),
                  pl.BlockSpec((tm, LANES), lambda i, j: (i, 0))],
        out_specs=[pl.BlockSpec((tm, MLA_GROUP * MLA_PAD_DIM), lambda i, j: (i, j)),
                   pl.BlockSpec((tm, MLA_GROUP * MLA_V_DIM), lambda i, j: (i, j))],
        out_shape=[jax.ShapeDtypeStruct((m, MLA_HEADS * MLA_PAD_DIM), BF16),
                   jax.ShapeDtypeStruct((m, MLA_HEADS * MLA_V_DIM), BF16)],
        compiler_params=_cparams(2),
        name="mla_kv_up",
    )(ckv, w_ukv, kpe, gn, gp, cos, sin)


def _route_kernel(x_ref, g_ref, w_ref, b_ref, ids_ref, wts_ref):
    xn = _rms(x_ref[...]) * g_ref[...]
    x_hi = xn.astype(BF16)
    x_lo = (xn - x_hi.astype(F32)).astype(BF16)
    w = w_ref[...]
    w_hi = w.astype(BF16)
    w_lo = (w - w_hi.astype(F32)).astype(BF16)
    lg = _dot_nt(w_hi, x_hi) + _dot_nt(w_hi, x_lo) + _dot_nt(w_lo, x_hi) + b_ref[...]
    tm = lg.shape[1]
    iota = lax.broadcasted_iota(I32, (EXPERTS_PER_GROUP, tm), 0)

    def first_argmax(v):
        mx = jnp.max(v, axis=0, keepdims=True)
        idx = jnp.min(jnp.where(v == mx, iota, EXPERTS_PER_GROUP), axis=0, keepdims=True)
        return mx, idx

    gl = lg[0:N_GROUPS, :]
    gmax, gsel = first_argmax(gl)
    g_gate = 1.0 / jnp.sum(jnp.exp(gl - gmax), axis=0, keepdims=True)
    el = jnp.zeros((EXPERTS_PER_GROUP, tm), F32)
    for g in range(N_GROUPS):
        lo = N_GROUPS + g * EXPERTS_PER_GROUP
        el = jnp.where(gsel == g, lg[lo:lo + EXPERTS_PER_GROUP, :], el)
    v1, i1 = first_argmax(el)
    v2, i2 = first_argmax(jnp.where(iota == i1, -jnp.inf, el))
    e = jnp.exp(v2 - v1)
    w1 = g_gate / (1.0 + e)
    w2 = g_gate * e / (1.0 + e)
    ids_ref[...] = jnp.concatenate([gsel * EXPERTS_PER_GROUP + i1,
                                    gsel * EXPERTS_PER_GROUP + i2], axis=0)
    wts_ref[...] = jnp.concatenate([w1, w2], axis=0)


def _route_call(x, g, w_t, b_col, *, tm=256):
    m = x.shape[0]
    return pl.pallas_call(
        _route_kernel,
        grid=(m // tm,),
        in_specs=[pl.BlockSpec((tm, D_MODEL), lambda i: (i, 0)),
                  pl.BlockSpec((1, D_MODEL), lambda i: (0, 0)),
                  pl.BlockSpec((LANES, D_MODEL), lambda i: (0, 0)),
                  pl.BlockSpec((LANES, 1), lambda i: (0, 0))],
        out_specs=[pl.BlockSpec((TOP_K, tm), lambda i: (0, i)),
                   pl.BlockSpec((TOP_K, tm), lambda i: (0, i))],
        out_shape=[jax.ShapeDtypeStruct((TOP_K, m), I32),
                   jax.ShapeDtypeStruct((TOP_K, m), F32)],
        compiler_params=_cparams(1, VMEM_LIMIT),
        name="moe_route",
    )(x, g, w_t, b_col)


def _sort_kernel(ids_ref, pos_ref, te_ref, nu_ref, rs_ref, rc_ref, rn_ref, *, ts):
    n_tok = ids_ref.shape[1]
    nck = n_tok // ts
    iota_e = lax.broadcasted_iota(I32, (N_EXPERTS, ts), 0)

    def chunk(c):
        return pl.ds(pl.multiple_of(c * ts, ts), ts)

    def count_body(c, acc):
        for k in range(TOP_K):
            oh = jnp.where(iota_e == ids_ref[pl.ds(k, 1), chunk(c)], 1.0, 0.0)
            acc = acc + jnp.sum(oh, axis=1, keepdims=True)
        return acc

    counts = lax.fori_loop(0, nck, count_body, jnp.zeros((N_EXPERTS, 1), F32))
    er = lax.broadcasted_iota(I32, (N_EXPERTS, N_EXPERTS), 0)
    ec = lax.broadcasted_iota(I32, (N_EXPERTS, N_EXPERTS), 1)
    strict_lower = jnp.where(ec < er, 1.0, 0.0).astype(BF16)

    def excl_cumsum(v):
        vb = jnp.broadcast_to(v, (N_EXPERTS, LANES)).astype(BF16)
        return jnp.dot(strict_lower, vb, preferred_element_type=F32)

    padded = jnp.floor((counts + (ROW_ALIGN - 1)) * (1.0 / ROW_ALIGN)) * ROW_ALIGN
    hi = jnp.floor(padded * (1.0 / LANES))
    start = excl_cumsum(hi) * LANES + excl_cumsum(padded - hi * LANES)
    chunks = jnp.floor((counts + (MOE_TM - 0.5)) * (1.0 / MOE_TM))
    first = excl_cumsum(chunks)
    ends = first + chunks
    total = jnp.max(ends, axis=0, keepdims=True)
    chunk_i = lax.broadcasted_iota(I32, (N_EXPERTS, LANES), 1).astype(F32)
    chunk_c = jnp.minimum(chunk_i, total - 1.0)
    te = jnp.sum(jnp.where(ends <= chunk_c, 1.0, 0.0), axis=0, keepdims=True)
    mine = lax.broadcasted_iota(I32, (N_EXPERTS, LANES), 0).astype(F32) == te
    rs = jnp.sum(jnp.where(mine, start + (chunk_c - first) * MOE_TM, 0.0), axis=0, keepdims=True)
    te_ref[...] = te.astype(I32)
    nu_ref[...] = total.astype(I32)
    rs_ref[...] = rs.astype(I32)
    left = jnp.sum(jnp.where(mine, padded - (chunk_c - first) * MOE_TM, 0.0), axis=0, keepdims=True)
    rc_ref[...] = jnp.clip(left, 0.0, float(MOE_TM)).astype(I32)
    real = jnp.sum(jnp.where(mine, counts - (chunk_c - first) * MOE_TM, 0.0), axis=0, keepdims=True)
    rn_ref[...] = jnp.clip(real, 0.0, float(MOE_TM)).astype(I32)

    row_base = start[:, 0:1]
    ur = lax.broadcasted_iota(I32, (ts, ts), 0)
    uc = lax.broadcasted_iota(I32, (ts, ts), 1)
    upper = jnp.where(ur <= uc, 1.0, 0.0).astype(BF16)

    def pos_body(k):
        def body(c, carry):
            hit = iota_e == ids_ref[pl.ds(k, 1), chunk(c)]
            incl = jnp.dot(jnp.where(hit, 1.0, 0.0).astype(BF16), upper,
                           preferred_element_type=F32)
            val = row_base + carry + incl - 1.0
            p = jnp.sum(jnp.where(hit, val, 0.0), axis=0, keepdims=True)
            pos_ref[pl.ds(k, 1), chunk(c)] = p.astype(I32)
            return carry + incl[:, ts - 1:ts]
        return body

    carry = jnp.zeros((N_EXPERTS, 1), F32)
    for k in range(TOP_K):
        carry = lax.fori_loop(0, nck, pos_body(k), carry)


def _sort_call(ids, *, ts=512):
    n_tok = ids.shape[1]
    ts = min(ts, n_tok)
    return pl.pallas_call(
        functools.partial(_sort_kernel, ts=ts),
        out_shape=[jax.ShapeDtypeStruct((TOP_K, n_tok), I32),
                   jax.ShapeDtypeStruct((1, LANES), I32),
                   jax.ShapeDtypeStruct((1, LANES), I32),
                   jax.ShapeDtypeStruct((1, LANES), I32),
                   jax.ShapeDtypeStruct((1, LANES), I32),
                   jax.ShapeDtypeStruct((1, LANES), I32)],
        name="moe_sort",
    )(ids)


def _sorted_rows(n_tok):
    return TOP_K * n_tok + N_EXPERTS * ROW_ALIGN


def _moe_chunks(n_tok):
    return -(-(TOP_K * n_tok) // MOE_TM) + N_EXPERTS


INVERT_BATCH = 16


def _invert_kernel(pos_ref, inv_ref):
    def init(i, c):
        for u in range(INVERT_BATCH):
            inv_ref[i * INVERT_BATCH + u] = jnp.int32(0)
        return c
    lax.fori_loop(0, inv_ref.shape[0] // INVERT_BATCH, init, 0)

    def put(i, c):
        base = i * INVERT_BATCH
        dst = [pos_ref[base + u] for u in range(INVERT_BATCH)]
        for u in range(INVERT_BATCH):
            inv_ref[dst[u]] = base + u
        return c
    lax.fori_loop(0, pos_ref.shape[0] // INVERT_BATCH, put, 0)


def _invert_call(pos_flat, n_rows):
    assert n_rows % INVERT_BATCH == 0 and pos_flat.shape[0] % INVERT_BATCH == 0
    return pl.pallas_call(
        _invert_kernel,
        in_specs=[pl.BlockSpec(memory_space=pltpu.SMEM)],
        out_specs=pl.BlockSpec(memory_space=pltpu.SMEM),
        out_shape=jax.ShapeDtypeStruct((n_rows,), I32),
        name="moe_invert",
    )(pos_flat)


def _moe_kernel(te_ref, nu_ref, rs_ref, rc_ref, rn_ref, inv_ref, x_hbm, g_ref, wg_ref, wu_ref,
                wd_ref, y_hbm, xbuf, rscale, acc, gu, wgu, gsem, ssem, *, n_tok):
    i = pl.program_id(0)
    k = pl.program_id(1)
    nk = pl.num_programs(1)
    n_used = nu_ref[0]
    valid = i < n_used
    slot = i % 2
    n_slots = TOP_K * n_tok

    def x_row_copy(tok, r, sl):
        return pltpu.make_async_copy(x_hbm.at[pl.ds(tok, 1)], xbuf.at[sl, pl.ds(r, 1)], gsem.at[sl])

    def x_gather(c, sl):
        base = rs_ref[c]

        def one(r):
            s = inv_ref[base + r]
            x_row_copy(jnp.where(s >= n_tok, s - n_tok, s), r, sl).start()

        def group(g, carry):
            for u in range(DMA_GROUP):
                one(g * DMA_GROUP + u)
            return carry
        n = rn_ref[c]
        n_groups = n >> 3
        lax.fori_loop(0, n_groups, group, 0)

        def rest(r, carry):
            one(r)
            return carry
        lax.fori_loop(n_groups * DMA_GROUP, n, rest, 0)

    def x_wait(c, sl):
        n = rn_ref[c]
        n8 = pl.multiple_of((n >> 3) << 3, ROW_ALIGN)

        @pl.when(n8 > 0)
        def _():
            pltpu.make_async_copy(x_hbm.at[pl.ds(0, n8)], xbuf.at[sl, pl.ds(0, n8)],
                                  gsem.at[sl]).wait()

        def rest(r, carry):
            x_row_copy(0, 0, sl).wait()
            return carry
        lax.fori_loop(0, n - n8, rest, 0)

    def y_copy(c):
        n = pl.multiple_of(rc_ref[c], ROW_ALIGN)
        dst = pl.ds(pl.multiple_of(rs_ref[c], ROW_ALIGN), n)
        return pltpu.make_async_copy(acc.at[pl.ds(0, n)], y_hbm.at[dst], ssem.at[0])

    @pl.when((i == 0) & (k == 0))
    def _():
        xbuf[...] = jnp.zeros(xbuf.shape, xbuf.dtype)
        x_gather(0, 0)
        acc[...] = jnp.zeros(acc.shape, acc.dtype)
        n_rows = y_hbm.shape[0]
        tails = [(off, min(MOE_TM, n_rows - off)) for off in range(n_slots, n_rows, MOE_TM)]
        for off, size in tails:
            pltpu.make_async_copy(acc.at[pl.ds(0, size)], y_hbm.at[pl.ds(off, size)],
                                  ssem.at[0]).start()
        for off, size in tails:
            pltpu.make_async_copy(acc.at[pl.ds(0, size)], y_hbm.at[pl.ds(off, size)],
                                  ssem.at[0]).wait()

    @pl.when(valid & (k == 0))
    def _():
        x_wait(i, slot)

        @pl.when(i + 1 < n_used)
        def _():
            x_gather(i + 1, 1 - slot)

        x = xbuf[slot]
        rscale[...] = lax.rsqrt(jnp.mean(x * x, axis=-1, keepdims=True) + NORM_EPS)

    @pl.when(valid)
    def _():
        wgu[:, :D_EXPERT] = wg_ref[0].astype(BF16)
        wgu[:, D_EXPERT:] = wu_ref[0].astype(BF16)
        cols = pl.ds(pl.multiple_of(k * MOE_TK, MOE_TK), MOE_TK)
        xs = (xbuf[slot, :, cols] * rscale[...] * g_ref[:, cols]).astype(BF16)
        part = jnp.dot(xs, wgu[...], preferred_element_type=F32)

        @pl.when(k == 0)
        def _():
            gu[...] = part

        @pl.when(k > 0)
        def _():
            gu[...] += part

        @pl.when(k == nk - 1)
        def _():
            a = gu[:, :D_EXPERT]
            hid = (a * jax.nn.sigmoid(a)) * gu[:, D_EXPERT:]
            out = jnp.dot(hid.astype(BF16), wd_ref[0].astype(BF16), preferred_element_type=F32)

            @pl.when(i > 0)
            def _():
                y_copy(i - 1).wait()
            acc[...] = out
            y_copy(i).start()

            @pl.when(i == n_used - 1)
            def _():
                y_copy(i).wait()


def _moe_call(te, nu, rs, rc, rn, inv, x, g, w_gate, w_up, w_down):
    n_tok = x.shape[0]
    nk = D_MODEL // MOE_TK
    last_k = nk - 1

    def w_in_map(i, k, te_r, nu_r, *_):
        return te_r[i], jnp.where(i < nu_r[0], k, last_k), 0

    def w_out_map(i, k, te_r, *_):
        return te_r[i], 0, 0

    grid_spec = pltpu.PrefetchScalarGridSpec(
        num_scalar_prefetch=6,
        grid=(_moe_chunks(n_tok), nk),
        in_specs=[
            pl.BlockSpec(memory_space=pl.ANY),
            pl.BlockSpec((1, D_MODEL), lambda i, k, *_: (0, 0)),
            pl.BlockSpec((1, MOE_TK, D_EXPERT), w_in_map),
            pl.BlockSpec((1, MOE_TK, D_EXPERT), w_in_map),
            pl.BlockSpec((1, D_EXPERT, D_MODEL), w_out_map),
        ],
        out_specs=pl.BlockSpec(memory_space=pl.ANY),
        scratch_shapes=[pltpu.VMEM((2, MOE_TM, D_MODEL), F32),
                        pltpu.VMEM((MOE_TM, 1), F32),
                        pltpu.VMEM((MOE_TM, D_MODEL), F32),
                        pltpu.VMEM((MOE_TM, 2 * D_EXPERT), F32),
                        pltpu.VMEM((MOE_TK, 2 * D_EXPERT), BF16),
                        pltpu.SemaphoreType.DMA((2,)),
                        pltpu.SemaphoreType.DMA((1,))],
    )
    return pl.pallas_call(
        functools.partial(_moe_kernel, n_tok=n_tok),
        grid_spec=grid_spec,
        out_shape=jax.ShapeDtypeStruct((_sorted_rows(n_tok), D_MODEL), F32),
        compiler_params=_cparams(2, VMEM_LIMIT),
        name="moe_experts",
    )(te, nu, rs, rc, rn, inv, x, g, w_gate, w_up, w_down)


def _combine_kernel(pos_ref, x_ref, w_ref, g_ref, y_hbm, xo_ref, *rest, n_tok):
    *h_ref, ybuf, sem = rest
    i = pl.program_id(0)
    nb = pl.num_programs(0)
    tm = x_ref.shape[0]
    slot = i % 2

    def fetch(tile, sl):
        def group(g, c):
            for u in range(DMA_GROUP):
                r = g * DMA_GROUP + u
                for k in range(TOP_K):
                    src = pos_ref[k * n_tok + tile * tm + r]
                    pltpu.make_async_copy(y_hbm.at[pl.ds(src, 1)], ybuf.at[sl, k, pl.ds(r, 1)],
                                          sem.at[sl]).start()
            return c
        lax.fori_loop(0, tm // DMA_GROUP, group, 0)

    @pl.when(i == 0)
    def _():
        fetch(0, 0)

    @pl.when(i + 1 < nb)
    def _():
        fetch(i + 1, 1 - slot)

    for k in range(TOP_K):
        pltpu.make_async_copy(y_hbm.at[pl.ds(0, tm)], ybuf.at[slot, k], sem.at[slot]).wait()
    w = w_ref[...]
    x = x_ref[...] + w[:, 0:1] * ybuf[slot, 0] + w[:, 1:2] * ybuf[slot, 1]
    xo_ref[...] = x
    if h_ref:
        h_ref[0][...] = (_rms(x) * g_ref[...]).astype(BF16)


def _combine_call(pos_flat, x, y, wts_t, g_next, *, tm=256):
    m, d = x.shape
    with_norm = g_next is not None
    g = g_next.reshape(1, d) if with_norm else jnp.ones((1, d), F32)
    row = pl.BlockSpec((tm, d), lambda i, pos_r: (i, 0))
    out_specs = [row, row] if with_norm else [row]
    out_shape = [jax.ShapeDtypeStruct((m, d), F32)]
    if with_norm:
        out_shape.append(jax.ShapeDtypeStruct((m, d), BF16))
    grid_spec = pltpu.PrefetchScalarGridSpec(
        num_scalar_prefetch=1,
        grid=(m // tm,),
        in_specs=[row,
                  pl.BlockSpec((tm, TOP_K), lambda i, pos_r: (i, 0)),
                  pl.BlockSpec((1, d), lambda i, pos_r: (0, 0)),
                  pl.BlockSpec(memory_space=pl.ANY)],
        out_specs=out_specs,
        scratch_shapes=[pltpu.VMEM((2, TOP_K, tm, d), F32),
                        pltpu.SemaphoreType.DMA((2,))],
    )
    return pl.pallas_call(
        functools.partial(_combine_kernel, n_tok=m),
        grid_spec=grid_spec,
        out_shape=out_shape,
        compiler_params=_cparams(1, VMEM_LIMIT),
        name="moe_combine",
    )(pos_flat, x, wts_t, g, y)


def _moe_layer(x, norm_g, w_group, b_group, w_router, b_router, w_gate, w_up, w_down, g_next):
    n_tok = x.shape[0]
    pad_rows = LANES - N_GROUPS - N_EXPERTS
    w_t = jnp.concatenate([w_group.T, w_router.T, jnp.zeros((pad_rows, D_MODEL), F32)], axis=0)
    b_col = jnp.concatenate([b_group, b_router, jnp.zeros((pad_rows,), F32)]).reshape(LANES, 1)
    g_row = norm_g.reshape(1, D_MODEL)
    ids, wts = _route_call(x, g_row, w_t, b_col)
    pos, te, nu, rs, rc, rn = _sort_call(ids)
    pos_flat = pos.reshape(-1)
    inv = _invert_call(pos_flat, _sorted_rows(n_tok))
    y = _moe_call(te.reshape(-1), nu.reshape(-1)[:1], rs.reshape(-1), rc.reshape(-1),
                  rn.reshape(-1), inv, x, g_row,
                  w_gate.reshape(N_EXPERTS, D_MODEL, D_EXPERT),
                  w_up.reshape(N_EXPERTS, D_MODEL, D_EXPERT),
                  w_down.reshape(N_EXPERTS, D_EXPERT, D_MODEL))
    return _combine_call(pos_flat, x, y, wts.T, g_next)


def _rope_tables(pos_flat, dim):
    inv_freq = ROPE_THETA ** (-jnp.arange(0, dim, 2, dtype=F32) / dim)
    ang = pos_flat.astype(F32)[:, None] * inv_freq
    return jnp.cos(ang), jnp.sin(ang)


def kernel(x, mem, positions, mem_norm_g, w_mem_kv, mem_k_norm_g, l0_attn_norm_g, l0_w_in, l0_q_norm_g, l0_k_norm_g, l0_lambda_q1, l0_lambda_k1, l0_lambda_q2, l0_lambda_k2, l0_subln_g, l0_cross_q_norm_g, l0_w_out, l0_ffn_norm_g, l0_w_group, l0_b_group, l0_w_router, l0_b_router, l0_w_gate, l0_w_up, l0_w_down, l1_attn_norm_g, l1_w_in, l1_cq_norm_g, l1_ckv_norm_g, l1_w_uq, l1_w_ukv, l1_q_norm_g, l1_k_norm_g, l1_cross_q_norm_g, l1_w_out, l1_ffn_norm_g, l1_w_group, l1_b_group, l1_w_router, l1_b_router, l1_w_gate, l1_w_up, l1_w_down):
    batch, seq, d = x.shape
    n_tok = batch * seq
    xf = x.reshape(n_tok, d)
    pos_flat = positions.reshape(n_tok)
    posq = pos_flat.reshape(n_tok, 1)
    posk = pos_flat.reshape(1, n_tok)
    ones128 = jnp.ones((n_tok, LANES), F32)

    c64, s64 = _rope_tables(pos_flat, DIFF_HEAD_DIM)
    cos_full = jnp.concatenate([c64, c64], axis=1)
    sin_full = jnp.concatenate([-s64, s64], axis=1)
    c32, s32 = _rope_tables(pos_flat, MLA_ROPE_DIM)
    z32 = jnp.zeros_like(c32)
    cos_k = jnp.concatenate([c32, c32, z32, z32], axis=1)
    sin_k = jnp.concatenate([-s32, s32, z32, z32], axis=1)

    cross_scale = CROSS_HEAD_DIM ** -0.5 * LOG2E

    memn = _norm_call(mem.reshape(batch * N_MEM, d), mem_norm_g)
    mem_gain = jnp.concatenate([jnp.tile(mem_k_norm_g, CROSS_HEADS), jnp.ones((CROSS_WIDTH,), F32)])
    k_tiles = CROSS_WIDTH // 512
    memkv = _mm_call(memn, w_mem_kv, mem_gain.reshape(1, -1), ones128[:batch * N_MEM],
                     ones128[:batch * N_MEM], ((0, k_tiles, "head256"), (k_tiles, 2 * k_tiles, "plain")),
                     tm=batch * N_MEM, tn=512, name="mem_kv")

    h0 = _norm_call(xf, l0_attn_norm_g)
    qk_tiles = SELF_WIDTH // 512
    gain0 = jnp.concatenate([
        jnp.tile(l0_q_norm_g, 2 * DIFF_HEADS) * (DIFF_HEAD_DIM ** -0.5 * LOG2E),
        jnp.tile(l0_k_norm_g, 2 * DIFF_HEADS),
        jnp.ones((SELF_WIDTH,), F32),
        jnp.tile(l0_cross_q_norm_g, CROSS_HEADS) * cross_scale]).reshape(1, -1)
    proj0 = _mm_call(h0, l0_w_in, gain0, cos_full, sin_full,
                     ((0, 2 * qk_tiles, "head128_rope"), (2 * qk_tiles, 3 * qk_tiles, "plain"),
                      (3 * qk_tiles, 3 * qk_tiles + CROSS_WIDTH // 512, "head256")),
                     tm=1024, tn=512, name="l0_in_proj")
    lam_init = 0.8 - 0.6 * math.exp(-0.3 * 0)
    row = lambda v: v.reshape(1, -1)
    o_self = _attn_call(proj0, proj0, proj0, posq, posk,
                        (row(l0_lambda_q1), row(l0_lambda_k1), row(l0_lambda_q2),
                         row(l0_lambda_k2), row(l0_subln_g)),
                        batch=batch, seq=seq, heads=DIFF_HEADS, n_maps=2, dk=DIFF_HEAD_DIM,
                        dv=DIFF_V_DIM, qcol=0, kcol=DIFF_HEADS, vcol=2 * DIFF_HEADS,
                        lam_init=lam_init, name="diff_attn")
    o_cross = _cross_call(proj0, memkv, batch=batch, seq=seq, qcol=3 * SELF_WIDTH // CROSS_WIDTH)
    x1 = _outproj_call(o_self, o_cross, l0_w_out, xf)
    x2, h1 = _moe_layer(x1, l0_ffn_norm_g, l0_w_group, l0_b_group, l0_w_router, l0_b_router,
                        l0_w_gate, l0_w_up, l0_w_down, l1_attn_norm_g)

    a = MLA_Q_RANK
    b = a + MLA_KV_RANK
    c = b + MLA_ROPE_DIM
    w_in_b = l1_w_in.astype(BF16)
    wcat = jnp.concatenate([w_in_b[:, :b], w_in_b[:, c:], w_in_b[:, b:c],
                            jnp.zeros((d, LANES - MLA_ROPE_DIM), BF16)], axis=1)
    cq, ckv, qm, kpe = _mla_in_call(
        h1, wcat, row(l1_cq_norm_g), row(l1_ckv_norm_g),
        row(jnp.tile(l1_cross_q_norm_g, CROSS_HEADS) * cross_scale))
    w_uq3 = l1_w_uq.astype(BF16).reshape(MLA_Q_RANK, MLA_HEADS, MLA_QK_DIM)
    w_q_nope = w_uq3[:, :, :MLA_NOPE_DIM].reshape(MLA_Q_RANK, -1)
    partner = (jnp.arange(MLA_ROPE_DIM) + MLA_ROPE_DIM // 2) % MLA_ROPE_DIM
    w_pe3 = w_uq3[:, :, MLA_NOPE_DIM:]
    pad3 = jnp.zeros_like(w_pe3)
    w_q_pe = jnp.concatenate([w_pe3, pad3], axis=2).reshape(MLA_Q_RANK, -1)
    half = MLA_ROPE_DIM // 2
    w_q_rot = jnp.concatenate([w_pe3[:, :, half:], w_pe3[:, :, :half], pad3],
                              axis=2).reshape(MLA_Q_RANK, -1)
    q_scale = MLA_QK_DIM ** -0.5 * LOG2E
    zeros64 = jnp.zeros((MLA_ROPE_DIM,), F32)
    g_q_pe = l1_q_norm_g[MLA_NOPE_DIM:] * q_scale
    q_pad = _mla_q_call(cq, w_q_nope, w_q_pe, w_q_rot,
                        row(l1_q_norm_g[:MLA_NOPE_DIM] * q_scale),
                        row(jnp.concatenate([g_q_pe, zeros64])),
                        row(jnp.concatenate([g_q_pe[partner], zeros64])), cos_k, sin_k)
    k_pad, v1 = _mla_kv_call(ckv, l1_w_ukv.astype(BF16), kpe,
                             row(l1_k_norm_g[:MLA_NOPE_DIM]),
                             row(jnp.concatenate([l1_k_norm_g[MLA_NOPE_DIM:], zeros64])),
                             cos_k, sin_k)
    o_self1 = _attn_call(q_pad, k_pad, v1, posq, posk, (),
                         batch=batch, seq=seq, heads=MLA_HEADS, n_maps=1, dk=MLA_PAD_DIM,
                         dv=MLA_V_DIM, qcol=0, kcol=0, vcol=0, lam_init=0.0, name="mla_attn")
    o_cross1 = _cross_call(qm, memkv, batch=batch, seq=seq, qcol=0)
    x3 = _outproj_call(o_self1, o_cross1, l1_w_out, x2)
    (x4,) = _moe_layer(x3, l1_ffn_norm_g, l1_w_group, l1_b_group, l1_w_router, l1_b_router,
                       l1_w_gate, l1_w_up, l1_w_down, None)
    return x4.reshape(batch, seq, d)
```

```python
import functools
import math

import jax
import jax.numpy as jnp
from jax import lax
from jax.experimental import pallas as pl
from jax.experimental.pallas import tpu as pltpu

F32 = jnp.float32
BF16 = jnp.bfloat16
I32 = jnp.int32

D_MODEL = 4096
N_MEM = 256
ROPE_THETA = 10000.0
NORM_EPS = 1e-6
LOG2E = math.log2(math.e)

SELF_WIDTH = 3 * D_MODEL // 4
CROSS_HEADS = 4
CROSS_HEAD_DIM = (D_MODEL // 4) // CROSS_HEADS
CROSS_WIDTH = CROSS_HEADS * CROSS_HEAD_DIM

DIFF_HEAD_DIM = 128
DIFF_V_DIM = 2 * DIFF_HEAD_DIM
DIFF_HEADS = SELF_WIDTH // DIFF_V_DIM

MLA_NOPE_DIM = 128
MLA_ROPE_DIM = 64
MLA_V_DIM = 128
MLA_HEADS = SELF_WIDTH // MLA_V_DIM
MLA_QK_DIM = MLA_NOPE_DIM + MLA_ROPE_DIM
MLA_Q_RANK = 3 * D_MODEL // 16
MLA_KV_RANK = D_MODEL // 8
MLA_PAD_DIM = 256

N_GROUPS = 8
EXPERTS_PER_GROUP = 8
N_EXPERTS = N_GROUPS * EXPERTS_PER_GROUP
D_EXPERT = 3 * D_MODEL // 32
TOP_K = 2

LANES = 128
MOE_TM = 320
MOE_TK = 2048
DMA_GROUP = 8
ROW_ALIGN = 8
ATTN_TQ = 256
VMEM_LIMIT = 56 * 1024 * 1024


def _cparams(n_axes, vmem=None):
    return pltpu.CompilerParams(dimension_semantics=("arbitrary",) * n_axes,
                                vmem_limit_bytes=vmem)


def _rms(x):
    return x * lax.rsqrt(jnp.mean(x * x, axis=-1, keepdims=True) + NORM_EPS)


def _dot_nt(a, b):
    return lax.dot_general(a, b, (((1,), (1,)), ((), ())), preferred_element_type=F32)


def _norm_kernel(x_ref, g_ref, o_ref):
    o_ref[...] = (_rms(x_ref[...]) * g_ref[...]).astype(o_ref.dtype)


def _norm_call(x, g, tm=256):
    m, d = x.shape
    return pl.pallas_call(
        _norm_kernel,
        grid=(m // tm,),
        in_specs=[pl.BlockSpec((tm, d), lambda i: (i, 0)),
                  pl.BlockSpec((1, d), lambda i: (0, 0))],
        out_specs=pl.BlockSpec((tm, d), lambda i: (i, 0)),
        out_shape=jax.ShapeDtypeStruct((m, d), BF16),
        compiler_params=_cparams(1),
        name="rmsnorm",
    )(x, g.reshape(1, d))


def _epilogue(kind, acc, rows, gain_ref, cos_ref, sin_ref, o_ref):
    tn = acc.shape[1]
    if kind == "plain":
        o_ref[rows, :] = acc.astype(o_ref.dtype)
    elif kind == "head128_rope":
        c = cos_ref[rows, :]
        s = sin_ref[rows, :]
        for j in range(tn // 128):
            sl = slice(j * 128, (j + 1) * 128)
            y = _rms(acc[:, sl]) * gain_ref[:, sl]
            y = y * c + pltpu.roll(y, 64, axis=1) * s
            o_ref[rows, sl] = y.astype(o_ref.dtype)
    elif kind == "head256":
        for j in range(tn // 256):
            sl = slice(j * 256, (j + 1) * 256)
            o_ref[rows, sl] = (_rms(acc[:, sl]) * gain_ref[:, sl]).astype(o_ref.dtype)
    else:
        raise ValueError(kind)


MM_SUB = 256


def _mm_kernel(x_ref, w_ref, gain_ref, cos_ref, sin_ref, o_ref, wbf_ref, *, kinds):
    n = pl.program_id(0)

    @pl.when(pl.program_id(1) == 0)
    def _():
        wbf_ref[...] = w_ref[...].astype(BF16)

    def run(kind):
        for r in range(x_ref.shape[0] // MM_SUB):
            rows = slice(r * MM_SUB, (r + 1) * MM_SUB)
            acc = jnp.dot(x_ref[rows, :], wbf_ref[...], preferred_element_type=F32)
            _epilogue(kind, acc, rows, gain_ref, cos_ref, sin_ref, o_ref)

    if len(kinds) == 1:
        run(kinds[0][2])
    else:
        for lo, hi, kind in kinds:
            pl.when((n >= lo) & (n < hi))(functools.partial(run, kind))


def _mm_call(x, w, gain, cos, sin, kinds, *, tm, tn, name):
    m, k = x.shape
    n = w.shape[1]
    return pl.pallas_call(
        functools.partial(_mm_kernel, kinds=kinds),
        grid=(n // tn, m // tm),
        in_specs=[pl.BlockSpec((tm, k), lambda j, i: (i, 0)),
                  pl.BlockSpec((k, tn), lambda j, i: (0, j)),
                  pl.BlockSpec((1, tn), lambda j, i: (0, j)),
                  pl.BlockSpec((tm, LANES), lambda j, i: (i, 0)),
                  pl.BlockSpec((tm, LANES), lambda j, i: (i, 0))],
        out_specs=pl.BlockSpec((tm, tn), lambda j, i: (i, j)),
        out_shape=jax.ShapeDtypeStruct((m, n), BF16),
        scratch_shapes=[pltpu.VMEM((k, tn), BF16)],
        compiler_params=_cparams(2, VMEM_LIMIT),
        name=name,
    )(x, w, gain, cos, sin)


def _attn_kernel(*refs, n_maps, dk, tq, nq, lam_init):
    if n_maps == 2:
        (q_ref, k_ref, v_ref, posq_ref, posk_ref, lq1_ref, lk1_ref, lq2_ref, lk2_ref, subg_ref,
         o_ref) = refs
        lam = (jnp.exp(jnp.sum(lq1_ref[...] * lk1_ref[...], axis=-1, keepdims=True))
               - jnp.exp(jnp.sum(lq2_ref[...] * lk2_ref[...], axis=-1, keepdims=True))
               + lam_init)
    else:
        q_ref, k_ref, v_ref, posq_ref, posk_ref, o_ref = refs

    for i in range(nq):
        rows = slice(i * tq, (i + 1) * tq)
        n_past = i * tq
        mask = posk_ref[:, rows] <= posq_ref[rows, :]
        probs = []
        for mi in range(n_maps):
            cols = slice(mi * dk, (mi + 1) * dk)
            qm = q_ref[rows, cols]
            s_diag = jnp.where(mask, _dot_nt(qm, k_ref[rows, cols]), -jnp.inf)
            mx = jnp.max(s_diag, axis=-1, keepdims=True)
            if n_past:
                s_past = _dot_nt(qm, k_ref[0:n_past, cols])
                mx = jnp.maximum(mx, jnp.max(s_past, axis=-1, keepdims=True))
            p_diag = jnp.exp2(s_diag - mx)
            p_past = jnp.exp2(s_past - mx) if n_past else None
            probs.append((p_diag, p_past))

        def pv(w_diag, w_past):
            o = jnp.dot(w_diag.astype(BF16), v_ref[rows, :], preferred_element_type=F32)
            if n_past:
                o = o + jnp.dot(w_past.astype(BF16), v_ref[0:n_past, :],
                                preferred_element_type=F32)
            return o

        def row_sum(w_diag, w_past):
            den = jnp.sum(w_diag, axis=-1, keepdims=True)
            if n_past:
                den = den + jnp.sum(w_past, axis=-1, keepdims=True)
            return den

        if n_maps == 2:
            (d1, p1), (d2, p2) = probs
            r1 = 1.0 / row_sum(d1, p1)
            c2 = lam * (1.0 / row_sum(d2, p2))
            o = pv(d1, p1) * r1 - pv(d2, p2) * c2
            o = _rms(o) * subg_ref[...] * (1.0 - lam_init)
        else:
            (d1, p1), = probs
            o = pv(d1, p1) * (1.0 / row_sum(d1, p1))
        o_ref[rows, :] = o.astype(o_ref.dtype)


def _attn_call(q, k, v, posq, posk, extra, *, batch, seq, heads, n_maps, dk, dv,
               qcol, kcol, vcol, lam_init, name):
    tq = ATTN_TQ
    in_specs = [
        pl.BlockSpec((seq, n_maps * dk), lambda b, h: (b, qcol + h)),
        pl.BlockSpec((seq, n_maps * dk), lambda b, h: (b, kcol + h)),
        pl.BlockSpec((seq, dv), lambda b, h: (b, vcol + h)),
        pl.BlockSpec((seq, 1), lambda b, h: (b, 0)),
        pl.BlockSpec((1, seq), lambda b, h: (0, b)),
    ] + [pl.BlockSpec(e.shape, lambda b, h: (0, 0)) for e in extra]
    return pl.pallas_call(
        functools.partial(_attn_kernel, n_maps=n_maps, dk=dk, tq=tq, nq=seq // tq,
                          lam_init=lam_init),
        grid=(batch, heads),
        in_specs=in_specs,
        out_specs=pl.BlockSpec((seq, dv), lambda b, h: (b, h)),
        out_shape=jax.ShapeDtypeStruct((batch * seq, heads * dv), BF16),
        compiler_params=_cparams(2, VMEM_LIMIT),
        name=name,
    )(q, k, v, posq, posk, *extra)


def _cross_kernel(q_ref, k_ref, v_ref, o_ref):
    for h in range(CROSS_HEADS):
        sl = slice(h * CROSS_HEAD_DIM, (h + 1) * CROSS_HEAD_DIM)
        s = _dot_nt(q_ref[:, sl], k_ref[:, sl])
        p = jnp.exp2(s - jnp.max(s, axis=-1, keepdims=True))
        inv = 1.0 / jnp.sum(p, axis=-1, keepdims=True)
        o = jnp.dot(p.astype(BF16), v_ref[:, sl], preferred_element_type=F32)
        o_ref[:, sl] = (o * inv).astype(o_ref.dtype)


def _cross_call(q, memkv, *, batch, seq, qcol, tq=512):
    nq = seq // tq
    return pl.pallas_call(
        _cross_kernel,
        grid=(batch, nq),
        in_specs=[pl.BlockSpec((tq, CROSS_WIDTH), lambda b, i: (b * nq + i, qcol)),
                  pl.BlockSpec((N_MEM, CROSS_WIDTH), lambda b, i: (b, 0)),
                  pl.BlockSpec((N_MEM, CROSS_WIDTH), lambda b, i: (b, 1))],
        out_specs=pl.BlockSpec((tq, CROSS_WIDTH), lambda b, i: (b * nq + i, 0)),
        out_shape=jax.ShapeDtypeStruct((batch * seq, CROSS_WIDTH), BF16),
        compiler_params=_cparams(2),
        name="cross_attn",
    )(q, memkv, memkv)


def _outproj_kernel(os_ref, oc_ref, w_ref, x_ref, o_ref, wbf_ref):
    @pl.when(pl.program_id(1) == 0)
    def _():
        wbf_ref[...] = w_ref[...].astype(BF16)

    for r in range(x_ref.shape[0] // MM_SUB):
        rows = slice(r * MM_SUB, (r + 1) * MM_SUB)
        acc = jnp.dot(os_ref[rows, :], wbf_ref[:SELF_WIDTH, :], preferred_element_type=F32)
        acc = acc + jnp.dot(oc_ref[rows, :], wbf_ref[SELF_WIDTH:, :], preferred_element_type=F32)
        o_ref[rows, :] = x_ref[rows, :] + acc


def _outproj_call(o_self, o_cross, w_out, x, *, tm=1024, tn=512):
    m = x.shape[0]
    return pl.pallas_call(
        _outproj_kernel,
        grid=(D_MODEL // tn, m // tm),
        in_specs=[pl.BlockSpec((tm, SELF_WIDTH), lambda j, i: (i, 0)),
                  pl.BlockSpec((tm, CROSS_WIDTH), lambda j, i: (i, 0)),
                  pl.BlockSpec((D_MODEL, tn), lambda j, i: (0, j)),
                  pl.BlockSpec((tm, tn), lambda j, i: (i, j))],
        out_specs=pl.BlockSpec((tm, tn), lambda j, i: (i, j)),
        out_shape=jax.ShapeDtypeStruct((m, D_MODEL), F32),
        scratch_shapes=[pltpu.VMEM((D_MODEL, tn), BF16)],
        compiler_params=_cparams(2, VMEM_LIMIT),
        name="out_proj",
    )(o_self, o_cross, w_out, x)


def _mla_in_kernel(h_ref, w_ref, gcq_ref, gckv_ref, gqm_ref, cq_ref, ckv_ref, qm_ref, kpe_ref):
    a = MLA_Q_RANK
    b = a + MLA_KV_RANK
    c = b + CROSS_WIDTH
    sub = h_ref.shape[0] // 2
    for r in range(2):
        rows = slice(r * sub, (r + 1) * sub)
        acc = jnp.dot(h_ref[rows, :], w_ref[...], preferred_element_type=F32)
        cq_ref[rows, :] = (_rms(acc[:, :a]) * gcq_ref[...]).astype(cq_ref.dtype)
        ckv_ref[rows, :] = (_rms(acc[:, a:b]) * gckv_ref[...]).astype(ckv_ref.dtype)
        for j in range(CROSS_HEADS):
            sl = slice(j * CROSS_HEAD_DIM, (j + 1) * CROSS_HEAD_DIM)
            qm_ref[rows, sl] = (_rms(acc[:, b + j * CROSS_HEAD_DIM:b + (j + 1) * CROSS_HEAD_DIM])
                                * gqm_ref[:, sl]).astype(qm_ref.dtype)
        kpe_ref[rows, :] = acc[:, c:]


def _mla_in_call(h, wcat, gcq, gckv, gqm, *, tm=512):
    m = h.shape[0]
    ncat = wcat.shape[1]
    row = lambda w: pl.BlockSpec((tm, w), lambda i: (i, 0))
    full = lambda r, w: pl.BlockSpec((r, w), lambda i: (0, 0))
    w_spec = pl.BlockSpec((D_MODEL, ncat), lambda i: (0, 0), pipeline_mode=pl.Buffered(1))
    return pl.pallas_call(
        _mla_in_kernel,
        grid=(m // tm,),
        in_specs=[row(D_MODEL), w_spec, full(1, MLA_Q_RANK), full(1, MLA_KV_RANK),
                  full(1, CROSS_WIDTH)],
        out_specs=[row(MLA_Q_RANK), row(MLA_KV_RANK), row(CROSS_WIDTH), row(LANES)],
        out_shape=[jax.ShapeDtypeStruct((m, MLA_Q_RANK), BF16),
                   jax.ShapeDtypeStruct((m, MLA_KV_RANK), BF16),
                   jax.ShapeDtypeStruct((m, CROSS_WIDTH), BF16),
                   jax.ShapeDtypeStruct((m, LANES), F32)],
        compiler_params=_cparams(1, VMEM_LIMIT),
        name="mla_in_proj",
    )(h, wcat, gcq, gckv, gqm)


MLA_GROUP = 4


def _rope_pe(x, c, s):
    lane = lax.broadcasted_iota(I32, x.shape, 1)
    partner = jnp.where((lane & 63) < 32, pltpu.roll(x, 96, axis=1), pltpu.roll(x, 32, axis=1))
    return x * c + partner * s


def _mla_q_kernel(cq_ref, wn_ref, wp_ref, wr_ref, gn_ref, gp_ref, gr_ref, cos_ref, sin_ref, o_ref):
    x = cq_ref[...]
    an = jnp.dot(x, wn_ref[...], preferred_element_type=F32)
    ap = jnp.dot(x, wp_ref[...], preferred_element_type=F32)
    ar = jnp.dot(x, wr_ref[...], preferred_element_type=F32)
    gc = cos_ref[...] * gp_ref[...]
    gs = sin_ref[...] * gr_ref[...]
    for j in range(MLA_GROUP):
        sl = slice(j * 128, (j + 1) * 128)
        nj = an[:, sl]
        pj = ap[:, sl]
        ss = jnp.sum(nj * nj + pj * pj, axis=-1, keepdims=True)
        rs = lax.rsqrt(ss * (1.0 / MLA_QK_DIM) + NORM_EPS)
        o_ref[:, j * 256:j * 256 + 128] = (nj * rs * gn_ref[...]).astype(o_ref.dtype)
        o_ref[:, j * 256 + 128:(j + 1) * 256] = ((pj * gc + ar[:, sl] * gs) * rs).astype(o_ref.dtype)


def _mla_q_call(cq, w_nope, w_pe, w_rot, gn, gp, gr, cos, sin, *, tm=1024):
    m = cq.shape[0]
    g = MLA_HEADS // MLA_GROUP
    tm = min(tm, m)
    wspec = pl.BlockSpec((MLA_Q_RANK, MLA_GROUP * 128), lambda j, i: (0, j))
    gspec = pl.BlockSpec((1, LANES), lambda j, i: (0, 0))
    tspec = pl.BlockSpec((tm, LANES), lambda j, i: (i, 0))
    return pl.pallas_call(
        _mla_q_kernel,
        grid=(g, m // tm),
        in_specs=[pl.BlockSpec((tm, MLA_Q_RANK), lambda j, i: (i, 0)),
                  wspec, wspec, wspec, gspec, gspec, gspec, tspec, tspec],
        out_specs=pl.BlockSpec((tm, MLA_GROUP * MLA_PAD_DIM), lambda j, i: (i, j)),
        out_shape=jax.ShapeDtypeStruct((m, MLA_HEADS * MLA_PAD_DIM), BF16),
        compiler_params=_cparams(2),
        name="mla_q_up",
    )(cq, w_nope, w_pe, w_rot, gn, gp, gr, cos, sin)


def _mla_kv_kernel(ckv_ref, w_ref, kpe_ref, gn_ref, gp_ref, cos_ref, sin_ref, k_ref, v_ref):
    acc = jnp.dot(ckv_ref[...], w_ref[...], preferred_element_type=F32)
    kpe = kpe_ref[...]
    ss_pe = jnp.sum(kpe * kpe, axis=-1, keepdims=True)
    pe = _rope_pe(kpe * gp_ref[...], cos_ref[...], sin_ref[...])
    for j in range(MLA_GROUP):
        kn = acc[:, j * 256:j * 256 + 128]
        ss = jnp.sum(kn * kn, axis=-1, keepdims=True) + ss_pe
        rs = lax.rsqrt(ss * (1.0 / MLA_QK_DIM) + NORM_EPS)
        k_ref[:, j * 256:j * 256 + 128] = (kn * rs * gn_ref[...]).astype(k_ref.dtype)
        k_ref[:, j * 256 + 128:(j + 1) * 256] = (pe * rs).astype(k_ref.dtype)
        v_ref[:, j * 128:(j + 1) * 128] = acc[:, j * 256 + 128:(j + 1) * 256].astype(v_ref.dtype)


def _mla_kv_call(ckv, w_ukv, kpe, gn, gp, cos, sin, *, tm=1024):
    m = ckv.shape[0]
    g = MLA_HEADS // MLA_GROUP
    tm = min(tm, m)
    return pl.pallas_call(
        _mla_kv_kernel,
        grid=(m // tm, g),
        in_specs=[pl.BlockSpec((tm, MLA_KV_RANK), lambda i, j: (i, 0)),
                  pl.BlockSpec((MLA_KV_RANK, MLA_GROUP * 256), lambda i, j: (0, j)),
                  pl.BlockSpec((tm, LANES), lambda i, j: (i, 0)),
                  pl.BlockSpec((1, LANES), lambda i, j: (0, 0)),
                  pl.BlockSpec((1, LANES), lambda i, j: (0, 0)),
                  pl.BlockSpec((tm, LANES), lambda i, j: (i, 0)),
                  pl.BlockSpec((tm, LANES), lambda i, j: (i, 0))],
        out_specs=[pl.BlockSpec((tm, MLA_GROUP * MLA_PAD_DIM), lambda i, j: (i, j)),
                   pl.BlockSpec((tm, MLA_GROUP * MLA_V_DIM), lambda i, j: (i, j))],
        out_shape=[jax.ShapeDtypeStruct((m, MLA_HEADS * MLA_PAD_DIM), BF16),
                   jax.ShapeDtypeStruct((m, MLA_HEADS * MLA_V_DIM), BF16)],
        compiler_params=_cparams(2),
        name="mla_kv_up",
    )(ckv, w_ukv, kpe, gn, gp, cos, sin)


def _route_kernel(x_ref, g_ref, w_ref, b_ref, ids_ref, wts_ref):
    xn = _rms(x_ref[...]) * g_ref[...]
    x_hi = xn.astype(BF16)
    x_lo = (xn - x_hi.astype(F32)).astype(BF16)
    w = w_ref[...]
    w_hi = w.astype(BF16)
    w_lo = (w - w_hi.astype(F32)).astype(BF16)
    lg = _dot_nt(w_hi, x_hi) + _dot_nt(w_hi, x_lo) + _dot_nt(w_lo, x_hi) + b_ref[...]
    tm = lg.shape[1]
    iota = lax.broadcasted_iota(I32, (EXPERTS_PER_GROUP, tm), 0)

    def first_argmax(v):
        mx = jnp.max(v, axis=0, keepdims=True)
        idx = jnp.min(jnp.where(v == mx, iota, EXPERTS_PER_GROUP), axis=0, keepdims=True)
        return mx, idx

    gl = lg[0:N_GROUPS, :]
    gmax, gsel = first_argmax(gl)
    g_gate = 1.0 / jnp.sum(jnp.exp(gl - gmax), axis=0, keepdims=True)
    el = jnp.zeros((EXPERTS_PER_GROUP, tm), F32)
    for g in range(N_GROUPS):
        lo = N_GROUPS + g * EXPERTS_PER_GROUP
        el = jnp.where(gsel == g, lg[lo:lo + EXPERTS_PER_GROUP, :], el)
    v1, i1 = first_argmax(el)
    v2, i2 = first_argmax(jnp.where(iota == i1, -jnp.inf, el))
    e = jnp.exp(v2 - v1)
    w1 = g_gate / (1.0 + e)
    w2 = g_gate * e / (1.0 + e)
    ids_ref[...] = jnp.concatenate([gsel * EXPERTS_PER_GROUP + i1,
                                    gsel * EXPERTS_PER_GROUP + i2], axis=0)
    wts_ref[...] = jnp.concatenate([w1, w2], axis=0)


def _route_call(x, g, w_t, b_col, *, tm=256):
    m = x.shape[0]
    return pl.pallas_call(
        _route_kernel,
        grid=(m // tm,),
        in_specs=[pl.BlockSpec((tm, D_MODEL), lambda i: (i, 0)),
                  pl.BlockSpec((1, D_MODEL), lambda i: (0, 0)),
                  pl.BlockSpec((LANES, D_MODEL), lambda i: (0, 0)),
                  pl.BlockSpec((LANES, 1), lambda i: (0, 0))],
        out_specs=[pl.BlockSpec((TOP_K, tm), lambda i: (0, i)),
                   pl.BlockSpec((TOP_K, tm), lambda i: (0, i))],
        out_shape=[jax.ShapeDtypeStruct((TOP_K, m), I32),
                   jax.ShapeDtypeStruct((TOP_K, m), F32)],
        compiler_params=_cparams(1, VMEM_LIMIT),
        name="moe_route",
    )(x, g, w_t, b_col)


def _sort_kernel(ids_ref, pos_ref, te_ref, nu_ref, rs_ref, rc_ref, rn_ref, *, ts):
    n_tok = ids_ref.shape[1]
    nck = n_tok // ts
    iota_e = lax.broadcasted_iota(I32, (N_EXPERTS, ts), 0)

    def chunk(c):
        return pl.ds(pl.multiple_of(c * ts, ts), ts)

    def count_body(c, acc):
        for k in range(TOP_K):
            oh = jnp.where(iota_e == ids_ref[pl.ds(k, 1), chunk(c)], 1.0, 0.0)
            acc = acc + jnp.sum(oh, axis=1, keepdims=True)
        return acc

    counts = lax.fori_loop(0, nck, count_body, jnp.zeros((N_EXPERTS, 1), F32))
    er = lax.broadcasted_iota(I32, (N_EXPERTS, N_EXPERTS), 0)
    ec = lax.broadcasted_iota(I32, (N_EXPERTS, N_EXPERTS), 1)
    strict_lower = jnp.where(ec < er, 1.0, 0.0).astype(BF16)

    def excl_cumsum(v):
        vb = jnp.broadcast_to(v, (N_EXPERTS, LANES)).astype(BF16)
        return jnp.dot(strict_lower, vb, preferred_element_type=F32)

    padded = jnp.floor((counts + (ROW_ALIGN - 1)) * (1.0 / ROW_ALIGN)) * ROW_ALIGN
    hi = jnp.floor(padded * (1.0 / LANES))
    start = excl_cumsum(hi) * LANES + excl_cumsum(padded - hi * LANES)
    chunks = jnp.floor((counts + (MOE_TM - 0.5)) * (1.0 / MOE_TM))
    first = excl_cumsum(chunks)
    ends = first + chunks
    total = jnp.max(ends, axis=0, keepdims=True)
    chunk_i = lax.broadcasted_iota(I32, (N_EXPERTS, LANES), 1).astype(F32)
    chunk_c = jnp.minimum(chunk_i, total - 1.0)
    te = jnp.sum(jnp.where(ends <= chunk_c, 1.0, 0.0), axis=0, keepdims=True)
    mine = lax.broadcasted_iota(I32, (N_EXPERTS, LANES), 0).astype(F32) == te
    rs = jnp.sum(jnp.where(mine, start + (chunk_c - first) * MOE_TM, 0.0), axis=0, keepdims=True)
    te_ref[...] = te.astype(I32)
    nu_ref[...] = total.astype(I32)
    rs_ref[...] = rs.astype(I32)
    left = jnp.sum(jnp.where(mine, padded - (chunk_c - first) * MOE_TM, 0.0), axis=0, keepdims=True)
    rc_ref[...] = jnp.clip(left, 0.0, float(MOE_TM)).astype(I32)
    real = jnp.sum(jnp.where(mine, counts - (chunk_c - first) * MOE_TM, 0.0), axis=0, keepdims=True)
    rn_ref[...] = jnp.clip(real, 0.0, float(MOE_TM)).astype(I32)

    row_base = start[:, 0:1]
    ur = lax.broadcasted_iota(I32, (ts, ts), 0)
    uc = lax.broadcasted_iota(I32, (ts, ts), 1)
    upper = jnp.where(ur <= uc, 1.0, 0.0).astype(BF16)

    def pos_body(k):
        def body(c, carry):
            hit = iota_e == ids_ref[pl.ds(k, 1), chunk(c)]
            incl = jnp.dot(jnp.where(hit, 1.0, 0.0).astype(BF16), upper,
                           preferred_element_type=F32)
            val = row_base + carry + incl - 1.0
            p = jnp.sum(jnp.where(hit, val, 0.0), axis=0, keepdims=True)
            pos_ref[pl.ds(k, 1), chunk(c)] = p.astype(I32)
            return carry + incl[:, ts - 1:ts]
        return body

    carry = jnp.zeros((N_EXPERTS, 1), F32)
    for k in range(TOP_K):
        carry = lax.fori_loop(0, nck, pos_body(k), carry)


def _sort_call(ids, *, ts=512):
    n_tok = ids.shape[1]
    ts = min(ts, n_tok)
    return pl.pallas_call(
        functools.partial(_sort_kernel, ts=ts),
        out_shape=[jax.ShapeDtypeStruct((TOP_K, n_tok), I32),
                   jax.ShapeDtypeStruct((1, LANES), I32),
                   jax.ShapeDtypeStruct((1, LANES), I32),
                   jax.ShapeDtypeStruct((1, LANES), I32),
                   jax.ShapeDtypeStruct((1, LANES), I32),
                   jax.ShapeDtypeStruct((1, LANES), I32)],
        name="moe_sort",
    )(ids)


def _sorted_rows(n_tok):
    return TOP_K * n_tok + N_EXPERTS * ROW_ALIGN


def _moe_chunks(n_tok):
    return -(-(TOP_K * n_tok) // MOE_TM) + N_EXPERTS


INVERT_BATCH = 16


def _invert_kernel(pos_ref, inv_ref):
    def init(i, c):
        for u in range(INVERT_BATCH):
            inv_ref[i * INVERT_BATCH + u] = jnp.int32(0)
        return c
    lax.fori_loop(0, inv_ref.shape[0] // INVERT_BATCH, init, 0)

    def put(i, c):
        base = i * INVERT_BATCH
        dst = [pos_ref[base + u] for u in range(INVERT_BATCH)]
        for u in range(INVERT_BATCH):
            inv_ref[dst[u]] = base + u
        return c
    lax.fori_loop(0, pos_ref.shape[0] // INVERT_BATCH, put, 0)


def _invert_call(pos_flat, n_rows):
    assert n_rows % INVERT_BATCH == 0 and pos_flat.shape[0] % INVERT_BATCH == 0
    return pl.pallas_call(
        _invert_kernel,
        in_specs=[pl.BlockSpec(memory_space=pltpu.SMEM)],
        out_specs=pl.BlockSpec(memory_space=pltpu.SMEM),
        out_shape=jax.ShapeDtypeStruct((n_rows,), I32),
        name="moe_invert",
    )(pos_flat)


def _moe_kernel(te_ref, nu_ref, rs_ref, rc_ref, rn_ref, inv_ref, x_hbm, g_ref, wg_ref, wu_ref,
                wd_ref, y_hbm, xbuf, rscale, acc, gu, wgu, gsem, ssem, *, n_tok):
    i = pl.program_id(0)
    k = pl.program_id(1)
    nk = pl.num_programs(1)
    n_used = nu_ref[0]
    valid = i < n_used
    slot = i % 2
    n_slots = TOP_K * n_tok

    def x_row_copy(tok, r, sl):
        return pltpu.make_async_copy(x_hbm.at[pl.ds(tok, 1)], xbuf.at[sl, pl.ds(r, 1)], gsem.at[sl])

    def x_gather(c, sl):
        base = rs_ref[c]

        def one(r):
            s = inv_ref[base + r]
            x_row_copy(jnp.where(s >= n_tok, s - n_tok, s), r, sl).start()

        def group(g, carry):
            for u in range(DMA_GROUP):
                one(g * DMA_GROUP + u)
            return carry
        n = rn_ref[c]
        n_groups = n >> 3
        lax.fori_loop(0, n_groups, group, 0)

        def rest(r, carry):
            one(r)
            return carry
        lax.fori_loop(n_groups * DMA_GROUP, n, rest, 0)

    def x_wait(c, sl):
        n = rn_ref[c]
        n8 = pl.multiple_of((n >> 3) << 3, ROW_ALIGN)

        @pl.when(n8 > 0)
        def _():
            pltpu.make_async_copy(x_hbm.at[pl.ds(0, n8)], xbuf.at[sl, pl.ds(0, n8)],
                                  gsem.at[sl]).wait()

        def rest(r, carry):
            x_row_copy(0, 0, sl).wait()
            return carry
        lax.fori_loop(0, n - n8, rest, 0)

    def y_copy(c):
        n = pl.multiple_of(rc_ref[c], ROW_ALIGN)
        dst = pl.ds(pl.multiple_of(rs_ref[c], ROW_ALIGN), n)
        return pltpu.make_async_copy(acc.at[pl.ds(0, n)], y_hbm.at[dst], ssem.at[0])

    @pl.when((i == 0) & (k == 0))
    def _():
        xbuf[...] = jnp.zeros(xbuf.shape, xbuf.dtype)
        x_gather(0, 0)
        acc[...] = jnp.zeros(acc.shape, acc.dtype)
        n_rows = y_hbm.shape[0]
        tails = [(off, min(MOE_TM, n_rows - off)) for off in range(n_slots, n_rows, MOE_TM)]
        for off, size in tails:
            pltpu.make_async_copy(acc.at[pl.ds(0, size)], y_hbm.at[pl.ds(off, size)],
                                  ssem.at[0]).start()
        for off, size in tails:
            pltpu.make_async_copy(acc.at[pl.ds(0, size)], y_hbm.at[pl.ds(off, size)],
                                  ssem.at[0]).wait()

    @pl.when(valid & (k == 0))
    def _():
        x_wait(i, slot)

        @pl.when(i + 1 < n_used)
        def _():
            x_gather(i + 1, 1 - slot)

        x = xbuf[slot]
        rscale[...] = lax.rsqrt(jnp.mean(x * x, axis=-1, keepdims=True) + NORM_EPS)

    @pl.when(valid)
    def _():
        wgu[:, :D_EXPERT] = wg_ref[0].astype(BF16)
        wgu[:, D_EXPERT:] = wu_ref[0].astype(BF16)
        cols = pl.ds(pl.multiple_of(k * MOE_TK, MOE_TK), MOE_TK)
        xs = (xbuf[slot, :, cols] * rscale[...] * g_ref[:, cols]).astype(BF16)
        part = jnp.dot(xs, wgu[...], preferred_element_type=F32)

        @pl.when(k == 0)
        def _():
            gu[...] = part

        @pl.when(k > 0)
        def _():
            gu[...] += part

        @pl.when(k == nk - 1)
        def _():
            a = gu[:, :D_EXPERT]
            hid = (a * jax.nn.sigmoid(a)) * gu[:, D_EXPERT:]
            out = jnp.dot(hid.astype(BF16), wd_ref[0].astype(BF16), preferred_element_type=F32)

            @pl.when(i > 0)
            def _():
                y_copy(i - 1).wait()
            acc[...] = out
            y_copy(i).start()

            @pl.when(i == n_used - 1)
            def _():
                y_copy(i).wait()


def _moe_call(te, nu, rs, rc, rn, inv, x, g, w_gate, w_up, w_down):
    n_tok = x.shape[0]
    nk = D_MODEL // MOE_TK
    last_k = nk - 1

    def w_in_map(i, k, te_r, nu_r, *_):
        return te_r[i], jnp.where(i < nu_r[0], k, last_k), 0

    def w_out_map(i, k, te_r, *_):
        return te_r[i], 0, 0

    grid_spec = pltpu.PrefetchScalarGridSpec(
        num_scalar_prefetch=6,
        grid=(_moe_chunks(n_tok), nk),
        in_specs=[
            pl.BlockSpec(memory_space=pl.ANY),
            pl.BlockSpec((1, D_MODEL), lambda i, k, *_: (0, 0)),
            pl.BlockSpec((1, MOE_TK, D_EXPERT), w_in_map),
            pl.BlockSpec((1, MOE_TK, D_EXPERT), w_in_map),
            pl.BlockSpec((1, D_EXPERT, D_MODEL), w_out_map),
        ],
        out_specs=pl.BlockSpec(memory_space=pl.ANY),
        scratch_shapes=[pltpu.VMEM((2, MOE_TM, D_MODEL), F32),
                        pltpu.VMEM((MOE_TM, 1), F32),
                        pltpu.VMEM((MOE_TM, D_MODEL), F32),
                        pltpu.VMEM((MOE_TM, 2 * D_EXPERT), F32),
                        pltpu.VMEM((MOE_TK, 2 * D_EXPERT), BF16),
                        pltpu.SemaphoreType.DMA((2,)),
                        pltpu.SemaphoreType.DMA((1,))],
    )
    return pl.pallas_call(
        functools.partial(_moe_kernel, n_tok=n_tok),
        grid_spec=grid_spec,
        out_shape=jax.ShapeDtypeStruct((_sorted_rows(n_tok), D_MODEL), F32),
        compiler_params=_cparams(2, VMEM_LIMIT),
        name="moe_experts",
    )(te, nu, rs, rc, rn, inv, x, g, w_gate, w_up, w_down)


def _combine_kernel(pos_ref, x_ref, w_ref, g_ref, y_hbm, xo_ref, *rest, n_tok):
    *h_ref, ybuf, sem = rest
    i = pl.program_id(0)
    nb = pl.num_programs(0)
    tm = x_ref.shape[0]
    slot = i % 2

    def fetch(tile, sl):
        def group(g, c):
            for u in range(DMA_GROUP):
                r = g * DMA_GROUP + u
                for k in range(TOP_K):
                    src = pos_ref[k * n_tok + tile * tm + r]
                    pltpu.make_async_copy(y_hbm.at[pl.ds(src, 1)], ybuf.at[sl, k, pl.ds(r, 1)],
                                          sem.at[sl]).start(priority=(u + k) % 2)
            return c
        lax.fori_loop(0, tm // DMA_GROUP, group, 0)

    @pl.when(i == 0)
    def _():
        fetch(0, 0)

    @pl.when(i + 1 < nb)
    def _():
        fetch(i + 1, 1 - slot)

    for k in range(TOP_K):
        pltpu.make_async_copy(y_hbm.at[pl.ds(0, tm)], ybuf.at[slot, k], sem.at[slot]).wait()
    w = w_ref[...]
    x = x_ref[...] + w[:, 0:1] * ybuf[slot, 0] + w[:, 1:2] * ybuf[slot, 1]
    xo_ref[...] = x
    if h_ref:
        h_ref[0][...] = (_rms(x) * g_ref[...]).astype(BF16)


def _combine_call(pos_flat, x, y, wts_t, g_next, *, tm=256):
    m, d = x.shape
    with_norm = g_next is not None
    g = g_next.reshape(1, d) if with_norm else jnp.ones((1, d), F32)
    row = pl.BlockSpec((tm, d), lambda i, pos_r: (i, 0))
    out_specs = [row, row] if with_norm else [row]
    out_shape = [jax.ShapeDtypeStruct((m, d), F32)]
    if with_norm:
        out_shape.append(jax.ShapeDtypeStruct((m, d), BF16))
    grid_spec = pltpu.PrefetchScalarGridSpec(
        num_scalar_prefetch=1,
        grid=(m // tm,),
        in_specs=[row,
                  pl.BlockSpec((tm, TOP_K), lambda i, pos_r: (i, 0)),
                  pl.BlockSpec((1, d), lambda i, pos_r: (0, 0)),
                  pl.BlockSpec(memory_space=pl.ANY)],
        out_specs=out_specs,
        scratch_shapes=[pltpu.VMEM((2, TOP_K, tm, d), F32),
                        pltpu.SemaphoreType.DMA((2,))],
    )
    return pl.pallas_call(
        functools.partial(_combine_kernel, n_tok=m),
        grid_spec=grid_spec,
        out_shape=out_shape,
        compiler_params=_cparams(1, VMEM_LIMIT),
        name="moe_combine",
    )(pos_flat, x, wts_t, g, y)


def _moe_layer(x, norm_g, w_group, b_group, w_router, b_router, w_gate, w_up, w_down, g_next):
    n_tok = x.shape[0]
    pad_rows = LANES - N_GROUPS - N_EXPERTS
    w_t = jnp.concatenate([w_group.T, w_router.T, jnp.zeros((pad_rows, D_MODEL), F32)], axis=0)
    b_col = jnp.concatenate([b_group, b_router, jnp.zeros((pad_rows,), F32)]).reshape(LANES, 1)
    g_row = norm_g.reshape(1, D_MODEL)
    ids, wts = _route_call(x, g_row, w_t, b_col)
    pos, te, nu, rs, rc, rn = _sort_call(ids)
    pos_flat = pos.reshape(-1)
    inv = _invert_call(pos_flat, _sorted_rows(n_tok))
    y = _moe_call(te.reshape(-1), nu.reshape(-1)[:1], rs.reshape(-1), rc.reshape(-1),
                  rn.reshape(-1), inv, x, g_row,
                  w_gate.reshape(N_EXPERTS, D_MODEL, D_EXPERT),
                  w_up.reshape(N_EXPERTS, D_MODEL, D_EXPERT),
                  w_down.reshape(N_EXPERTS, D_EXPERT, D_MODEL))
    return _combine_call(pos_flat, x, y, wts.T, g_next)


def _rope_tables(pos_flat, dim):
    inv_freq = ROPE_THETA ** (-jnp.arange(0, dim, 2, dtype=F32) / dim)
    ang = pos_flat.astype(F32)[:, None] * inv_freq
    return jnp.cos(ang), jnp.sin(ang)


def kernel(x, mem, positions, mem_norm_g, w_mem_kv, mem_k_norm_g, l0_attn_norm_g, l0_w_in, l0_q_norm_g, l0_k_norm_g, l0_lambda_q1, l0_lambda_k1, l0_lambda_q2, l0_lambda_k2, l0_subln_g, l0_cross_q_norm_g, l0_w_out, l0_ffn_norm_g, l0_w_group, l0_b_group, l0_w_router, l0_b_router, l0_w_gate, l0_w_up, l0_w_down, l1_attn_norm_g, l1_w_in, l1_cq_norm_g, l1_ckv_norm_g, l1_w_uq, l1_w_ukv, l1_q_norm_g, l1_k_norm_g, l1_cross_q_norm_g, l1_w_out, l1_ffn_norm_g, l1_w_group, l1_b_group, l1_w_router, l1_b_router, l1_w_gate, l1_w_up, l1_w_down):
    batch, seq, d = x.shape
    n_tok = batch * seq
    xf = x.reshape(n_tok, d)
    pos_flat = positions.reshape(n_tok)
    posq = pos_flat.reshape(n_tok, 1)
    posk = pos_flat.reshape(1, n_tok)
    ones128 = jnp.ones((n_tok, LANES), F32)

    c64, s64 = _rope_tables(pos_flat, DIFF_HEAD_DIM)
    cos_full = jnp.concatenate([c64, c64], axis=1)
    sin_full = jnp.concatenate([-s64, s64], axis=1)
    c32, s32 = _rope_tables(pos_flat, MLA_ROPE_DIM)
    z32 = jnp.zeros_like(c32)
    cos_k = jnp.concatenate([c32, c32, z32, z32], axis=1)
    sin_k = jnp.concatenate([-s32, s32, z32, z32], axis=1)

    cross_scale = CROSS_HEAD_DIM ** -0.5 * LOG2E

    memn = _norm_call(mem.reshape(batch * N_MEM, d), mem_norm_g)
    mem_gain = jnp.concatenate([jnp.tile(mem_k_norm_g, CROSS_HEADS), jnp.ones((CROSS_WIDTH,), F32)])
    k_tiles = CROSS_WIDTH // 512
    memkv = _mm_call(memn, w_mem_kv, mem_gain.reshape(1, -1), ones128[:batch * N_MEM],
                     ones128[:batch * N_MEM], ((0, k_tiles, "head256"), (k_tiles, 2 * k_tiles, "plain")),
                     tm=batch * N_MEM, tn=512, name="mem_kv")

    h0 = _norm_call(xf, l0_attn_norm_g)
    qk_tiles = SELF_WIDTH // 512
    gain0 = jnp.concatenate([
        jnp.tile(l0_q_norm_g, 2 * DIFF_HEADS) * (DIFF_HEAD_DIM ** -0.5 * LOG2E),
        jnp.tile(l0_k_norm_g, 2 * DIFF_HEADS),
        jnp.ones((SELF_WIDTH,), F32),
        jnp.tile(l0_cross_q_norm_g, CROSS_HEADS) * cross_scale]).reshape(1, -1)
    proj0 = _mm_call(h0, l0_w_in, gain0, cos_full, sin_full,
                     ((0, 2 * qk_tiles, "head128_rope"), (2 * qk_tiles, 3 * qk_tiles, "plain"),
                      (3 * qk_tiles, 3 * qk_tiles + CROSS_WIDTH // 512, "head256")),
                     tm=1024, tn=512, name="l0_in_proj")
    lam_init = 0.8 - 0.6 * math.exp(-0.3 * 0)
    row = lambda v: v.reshape(1, -1)
    o_self = _attn_call(proj0, proj0, proj0, posq, posk,
                        (row(l0_lambda_q1), row(l0_lambda_k1), row(l0_lambda_q2),
                         row(l0_lambda_k2), row(l0_subln_g)),
                        batch=batch, seq=seq, heads=DIFF_HEADS, n_maps=2, dk=DIFF_HEAD_DIM,
                        dv=DIFF_V_DIM, qcol=0, kcol=DIFF_HEADS, vcol=2 * DIFF_HEADS,
                        lam_init=lam_init, name="diff_attn")
    o_cross = _cross_call(proj0, memkv, batch=batch, seq=seq, qcol=3 * SELF_WIDTH // CROSS_WIDTH)
    x1 = _outproj_call(o_self, o_cross, l0_w_out, xf)
    x2, h1 = _moe_layer(x1, l0_ffn_norm_g, l0_w_group, l0_b_group, l0_w_router, l0_b_router,
                        l0_w_gate, l0_w_up, l0_w_down, l1_attn_norm_g)

    a = MLA_Q_RANK
    b = a + MLA_KV_RANK
    c = b + MLA_ROPE_DIM
    w_in_b = l1_w_in.astype(BF16)
    wcat = jnp.concatenate([w_in_b[:, :b], w_in_b[:, c:], w_in_b[:, b:c],
                            jnp.zeros((d, LANES - MLA_ROPE_DIM), BF16)], axis=1)
    cq, ckv, qm, kpe = _mla_in_call(
        h1, wcat, row(l1_cq_norm_g), row(l1_ckv_norm_g),
        row(jnp.tile(l1_cross_q_norm_g, CROSS_HEADS) * cross_scale))
    w_uq3 = l1_w_uq.astype(BF16).reshape(MLA_Q_RANK, MLA_HEADS, MLA_QK_DIM)
    w_q_nope = w_uq3[:, :, :MLA_NOPE_DIM].reshape(MLA_Q_RANK, -1)
    partner = (jnp.arange(MLA_ROPE_DIM) + MLA_ROPE_DIM // 2) % MLA_ROPE_DIM
    w_pe3 = w_uq3[:, :, MLA_NOPE_DIM:]
    pad3 = jnp.zeros_like(w_pe3)
    w_q_pe = jnp.concatenate([w_pe3, pad3], axis=2).reshape(MLA_Q_RANK, -1)
    half = MLA_ROPE_DIM // 2
    w_q_rot = jnp.concatenate([w_pe3[:, :, half:], w_pe3[:, :, :half], pad3],
                              axis=2).reshape(MLA_Q_RANK, -1)
    q_scale = MLA_QK_DIM ** -0.5 * LOG2E
    zeros64 = jnp.zeros((MLA_ROPE_DIM,), F32)
    g_q_pe = l1_q_norm_g[MLA_NOPE_DIM:] * q_scale
    q_pad = _mla_q_call(cq, w_q_nope, w_q_pe, w_q_rot,
                        row(l1_q_norm_g[:MLA_NOPE_DIM] * q_scale),
                        row(jnp.concatenate([g_q_pe, zeros64])),
                        row(jnp.concatenate([g_q_pe[partner], zeros64])), cos_k, sin_k)
    k_pad, v1 = _mla_kv_call(ckv, l1_w_ukv.astype(BF16), kpe,
                             row(l1_k_norm_g[:MLA_NOPE_DIM]),
                             row(jnp.concatenate([l1_k_norm_g[MLA_NOPE_DIM:], zeros64])),
                             cos_k, sin_k)
    o_self1 = _attn_call(q_pad, k_pad, v1, posq, posk, (),
                         batch=batch, seq=seq, heads=MLA_HEADS, n_maps=1, dk=MLA_PAD_DIM,
                         dv=MLA_V_DIM, qcol=0, kcol=0, vcol=0, lam_init=0.0, name="mla_attn")
    o_cross1 = _cross_call(qm, memkv, batch=batch, seq=seq, qcol=0)
    x3 = _outproj_call(o_self1, o_cross1, l1_w_out, x2)
    (x4,) = _moe_layer(x3, l1_ffn_norm_g, l1_w_group, l1_b_group, l1_w_router, l1_b_router,
                       l1_w_gate, l1_w_up, l1_w_down, None)
    return x4.reshape(batch, seq, d)
```
